```python
import jax, jax.numpy as jnp
from jax import lax
import numpy as np

D_MODEL = 1024
BATCH = 4
SEQ = 4096
DEPTH = 1
DEC_BATCH = 32
DEC_SEQ = 1
PAST_LEN = 16384
PAGE_SIZE = 128

GLA_HEADS = 4
GLA_KEY_DIM = D_MODEL // 2
GLA_VAL_DIM = D_MODEL
GLA_HEAD_K = GLA_KEY_DIM // GLA_HEADS
GLA_HEAD_V = GLA_VAL_DIM // GLA_HEADS
GLA_GATE_RANK = 16
GLA_GATE_NORMALIZER = 16.0
GLA_CHUNK = 64
ATTN_GROUPS = ((128, 1), (512, 4), (2048, 16))
N_GROUPS = 3
ATTN_HEADS = 4
ATTN_HEAD_DIM = 128
ATTN_WIDTH = N_GROUPS * ATTN_HEADS * ATTN_HEAD_DIM
ATTN_OUT_DIM = ATTN_HEADS * ATTN_HEAD_DIM
PROJ_SPLITS = (GLA_KEY_DIM, GLA_KEY_DIM, GLA_VAL_DIM, GLA_VAL_DIM, GLA_GATE_RANK,
               ATTN_WIDTH, ATTN_WIDTH, ATTN_WIDTH, D_MODEL, D_MODEL)
PROJ_DIM = sum(PROJ_SPLITS)
N_EXPERTS = 32
TOP_K = 4
D_FF = D_MODEL
SWIGLU_LIMIT = 7.0
SWIGLU_ALPHA = 1.702
NORM_EPS = 1e-5

kernel_name = 'hybrid_gla_dilated_moe_step'


def rmsnorm(x, g):
    xf = x.astype(jnp.float32)
    y = xf * lax.rsqrt(jnp.mean(xf * xf, axis=-1, keepdims=True) + NORM_EPS)
    return (y * g.astype(jnp.float32)).astype(x.dtype)


def alibi_slopes():
    n = N_GROUPS * ATTN_HEADS
    return jnp.exp2(-8.0 * jnp.arange(1, n + 1, dtype=jnp.float32) / n).reshape(N_GROUPS, ATTN_HEADS)


def project_inputs(x, g_norm_mix, w_in, w_gk_up, b_gk_up):
    B, T, _ = x.shape
    h = rmsnorm(x, g_norm_mix)
    proj = h @ w_in
    offsets = np.cumsum(PROJ_SPLITS)[:-1].tolist()
    q_g, k_g, v_g, r_g, gk_low, q_a, k_a, v_a, gate_gla, gate_attn = jnp.split(proj, offsets, axis=-1)
    log_a = jax.nn.log_sigmoid((gk_low @ w_gk_up + b_gk_up).astype(jnp.float32)) / GLA_GATE_NORMALIZER
    heads = lambda a, dim: a.reshape(B, T, GLA_HEADS, dim)
    attn = lambda a: a.reshape(B, T, N_GROUPS, ATTN_HEADS, ATTN_HEAD_DIM)
    return (heads(q_g, GLA_HEAD_K) * GLA_HEAD_K ** -0.5, heads(k_g, GLA_HEAD_K), heads(v_g, GLA_HEAD_V),
            r_g, heads(log_a, GLA_HEAD_K), attn(q_a), attn(k_a), attn(v_a), gate_gla, gate_attn)


def gla_chunked(q, k, v, log_a, state0):
    B, T, H, K = q.shape
    V = v.shape[-1]
    C = min(GLA_CHUNK, T)
    nc = -(-T // C)
    pad = nc * C - T

    def chunks(a):
        a = jnp.pad(a.astype(jnp.float32), ((0, 0), (0, pad), (0, 0), (0, 0)))
        return a.reshape(B, nc, C, H, a.shape[-1]).transpose(1, 0, 3, 2, 4)

    causal = jnp.tril(jnp.ones((C, C), dtype=bool))

    def step(S, inp):
        qc, kc, vc, gc = inp
        b = jnp.cumsum(gc, axis=2)
        b_last = b[:, :, -1:, :]
        q_in = qc * jnp.exp(b)
        k_in = kc * jnp.exp(-b)
        a = jnp.where(causal, jnp.einsum('bhqk,bhsk->bhqs', q_in, k_in), 0.0)
        o = jnp.einsum('bhqk,bhkv->bhqv', q_in, S) + jnp.einsum('bhqs,bhsv->bhqv', a, vc)
        S = jnp.exp(b_last)[:, :, 0, :, None] * S + jnp.einsum('bhsk,bhsv->bhkv', kc * jnp.exp(b_last - b), vc)
        return S, o

    S, o = lax.scan(step, state0.astype(jnp.float32), (chunks(q), chunks(k), chunks(v), chunks(log_a)))
    o = o.transpose(1, 0, 3, 2, 4).reshape(B, nc * C, H, V)[:, :T]
    return o.astype(q.dtype), S.astype(state0.dtype)


def dilated_attention_prompt(q, k, v, window, dilation, slopes):
    B, T, H, E = q.shape
    L = window // dilation
    span = L * dilation
    n_sub = -(-T // span) * L
    pad = n_sub * dilation - T
    nb = n_sub // L

    def blocks(a):
        a = jnp.pad(a.astype(jnp.float32), ((0, 0), (0, pad), (0, 0), (0, 0)))
        a = a.reshape(B, n_sub, dilation, H, E).transpose(0, 2, 1, 3, 4)
        return a.reshape(B, dilation, nb, L, H, E)

    def with_prev(a):
        prev = jnp.pad(a, ((0, 0), (0, 0), (1, 0), (0, 0), (0, 0), (0, 0)))[:, :, :-1]
        return jnp.concatenate([prev, a], axis=3)

    qb = blocks(q)
    kk, vv = with_prev(blocks(k)), with_prev(blocks(v))
    scores = jnp.einsum('brnqhe,brnkhe->brnhqk', qb, kk) * E ** -0.5
    rel = jnp.arange(L)[:, None] + L - jnp.arange(2 * L)[None, :]
    first_pad = (jnp.arange(nb)[:, None, None] == 0) & (jnp.arange(2 * L)[None, None, :] < L)
    valid = (rel >= 0) & (rel <= L) & ~first_pad
    bias = -slopes.astype(jnp.float32)[:, None, None] * (dilation * rel).astype(jnp.float32)
    scores = jnp.where(valid[:, None], scores + bias, -jnp.inf)
    m = jnp.max(scores, axis=-1, keepdims=True)
    p = jnp.exp(scores - m)
    s = jnp.sum(p, axis=-1, keepdims=True)
    o = jnp.einsum('brnhqk,brnkhe->brnqhe', p / s, vv)
    lse = (m + jnp.log(s))[..., 0]
    o = o.reshape(B, dilation, n_sub, H, E).transpose(0, 2, 1, 3, 4).reshape(B, n_sub * dilation, H, E)[:, :T]
    lse = lse.transpose(0, 1, 2, 4, 3).reshape(B, dilation, n_sub, H).transpose(0, 2, 1, 3)
    lse = lse.reshape(B, n_sub * dilation, H)[:, :T]
    return o, lse


def dilated_attention_sample(q, k_all, v_all, window, dilation, n_buf, slopes):
    Bd, S, H, E = q.shape
    L = window // dilation
    j = jnp.arange(L + 1)
    idx = n_buf + jnp.arange(S)[:, None] - dilation * j[None, :]
    valid = idx >= 0
    idx = jnp.maximum(idx, 0)
    kg = k_all[:, idx].astype(jnp.float32)
    vg = v_all[:, idx].astype(jnp.float32)
    scores = jnp.einsum('bshe,bsjhe->bhsj', q.astype(jnp.float32), kg) * E ** -0.5
    scores = scores - slopes.astype(jnp.float32)[:, None, None] * (dilation * j).astype(jnp.float32)
    scores = jnp.where(valid[None, None], scores, -jnp.inf)
    m = jnp.max(scores, axis=-1, keepdims=True)
    p = jnp.exp(scores - m)
    s = jnp.sum(p, axis=-1, keepdims=True)
    o = jnp.einsum('bhsj,bsjhe->bshe', p / s, vg)
    lse = (m + jnp.log(s))[..., 0].transpose(0, 2, 1)
    return o, lse


def token_mixing(x, st_gla, kv_bufs, slopes, g_norm_mix, w_in, w_gk_up, b_gk_up, g_gla_norm,
                 w_branch_gla, w_branch_attn, w_out):
    B, T, _ = x.shape
    q_g, k_g, v_g, r_g, log_a, q_a, k_a, v_a, gate_gla, gate_attn = project_inputs(x, g_norm_mix, w_in, w_gk_up, b_gk_up)
    o_gla, st_new = gla_chunked(q_g, k_g, v_g, log_a, st_gla)
    outs, lses, new_bufs = [], [], []
    for g, (window, dilation) in enumerate(ATTN_GROUPS):
        q, k, v = q_a[:, :, g], k_a[:, :, g], v_a[:, :, g]
        kv_new = jnp.stack([k, v], axis=2)
        if kv_bufs is None:
            o, lse = dilated_attention_prompt(q, k, v, window, dilation, slopes[g])
            n_keep = min(window, T)
            new_bufs.append(kv_new[:, T - n_keep:])
        else:
            buf = kv_bufs[g]
            n_buf = buf.shape[1]
            kv_all = jnp.concatenate([buf, kv_new.astype(buf.dtype)], axis=1)
            o, lse = dilated_attention_sample(q, kv_all[:, :, 0], kv_all[:, :, 1], window, dilation, n_buf, slopes[g])
            new_bufs.append(kv_all[:, T:])
        outs.append(o)
        lses.append(lse)
    w_grp = jax.nn.softmax(jnp.stack(lses), axis=0)
    o_att = jnp.sum(w_grp[..., None] * jnp.stack(outs), axis=0).astype(x.dtype)
    u_gla = rmsnorm(o_gla, g_gla_norm).reshape(B, T, GLA_VAL_DIM) * jax.nn.silu(r_g)
    merged = (jax.nn.sigmoid(gate_gla) * (u_gla @ w_branch_gla)
              + jax.nn.sigmoid(gate_attn) * (o_att.reshape(B, T, ATTN_OUT_DIM) @ w_branch_attn))
    return x + merged @ w_out, st_new, tuple(new_bufs)


def moe_ffn(h, w_router, b_router, w_gate_up, b_gate_up, w_down, b_down):
    logits = (h @ w_router + b_router).astype(jnp.float32)
    top_logits, top_idx = lax.top_k(logits, TOP_K)
    top_w = jax.nn.softmax(top_logits, axis=-1)
    gates = jnp.sum(jax.nn.one_hot(top_idx, N_EXPERTS, dtype=jnp.float32) * top_w[..., None], axis=1)
    out = jnp.zeros((h.shape[0], D_MODEL), jnp.float32)
    for e in range(N_EXPERTS):
        gu = h @ w_gate_up[e] + b_gate_up[e]
        gate = jnp.minimum(gu[:, :D_FF], SWIGLU_LIMIT)
        up = jnp.clip(gu[:, D_FF:], -SWIGLU_LIMIT, SWIGLU_LIMIT)
        act = (up + 1.0) * gate * jax.nn.sigmoid(SWIGLU_ALPHA * gate)
        out = out + gates[:, e:e + 1] * (act @ w_down[e] + b_down[e]).astype(jnp.float32)
    return out


def channel_mixing(x_p, x_s, g_norm_ffn, w_router, b_router, w_gate_up, b_gate_up, w_down, b_down):
    n_p = x_p.shape[0] * x_p.shape[1]
    h = jnp.concatenate([rmsnorm(x_p, g_norm_ffn).reshape(-1, D_MODEL), rmsnorm(x_s, g_norm_ffn).reshape(-1, D_MODEL)], axis=0)
    y = moe_ffn(h, w_router, b_router, w_gate_up, b_gate_up, w_down, b_down).astype(x_p.dtype)
    return x_p + y[:n_p].reshape(x_p.shape), x_s + y[n_p:].reshape(x_s.shape)


def setup_inputs(seed: int = 0) -> dict:
    key = jax.random.key(seed)
    ks = jax.random.split(key, 24)
    f32 = jnp.float32

    def nrm(k, shape, scale):
        return jax.random.normal(k, shape, f32) * scale

    nb = [min(w, PAST_LEN) for w, _ in ATTN_GROUPS]
    kv_shape = lambda n: (DEPTH, DEC_BATCH, n, 2, ATTN_HEADS, ATTN_HEAD_DIM)
    return {
        'x_prompt': nrm(ks[0], (BATCH, SEQ, D_MODEL), 1.0),
        'x_sample': nrm(ks[1], (DEC_BATCH, DEC_SEQ, D_MODEL), 1.0),
        'state_gla': nrm(ks[2], (DEPTH, DEC_BATCH, GLA_HEADS, GLA_HEAD_K, GLA_HEAD_V), 1.0),
        'cache_kv_w128': nrm(ks[3], kv_shape(nb[0]), 1.0),
        'cache_kv_w512': nrm(ks[4], kv_shape(nb[1]), 1.0),
        'cache_kv_w2048': nrm(ks[5], kv_shape(nb[2]), 1.0),
        'g_norm_mix': 1.0 + nrm(ks[6], (DEPTH, D_MODEL), 0.02),
        'w_in': nrm(ks[7], (DEPTH, D_MODEL, PROJ_DIM), D_MODEL ** -0.5),
        'w_gk_up': nrm(ks[8], (DEPTH, GLA_GATE_RANK, GLA_KEY_DIM), GLA_GATE_RANK ** -0.5),
        'b_gk_up': nrm(ks[9], (DEPTH, GLA_KEY_DIM), 0.1),
        'g_gla_norm': 1.0 + nrm(ks[10], (DEPTH, GLA_HEAD_V), 0.02),
        'w_branch_gla': nrm(ks[11], (DEPTH, GLA_VAL_DIM, D_MODEL), GLA_VAL_DIM ** -0.5),
        'w_branch_attn': nrm(ks[12], (DEPTH, ATTN_OUT_DIM, D_MODEL), ATTN_OUT_DIM ** -0.5),
        'w_out': nrm(ks[13], (DEPTH, D_MODEL, D_MODEL), D_MODEL ** -0.5),
        'g_norm_ffn': 1.0 + nrm(ks[14], (DEPTH, D_MODEL), 0.02),
        'w_router': nrm(ks[15], (DEPTH, D_MODEL, N_EXPERTS), D_MODEL ** -0.5),
        'b_router': nrm(ks[16], (DEPTH, N_EXPERTS), 0.01),
        'w_gate_up': nrm(ks[17], (DEPTH, N_EXPERTS, D_MODEL, 2 * D_FF), D_MODEL ** -0.5),
        'b_gate_up': nrm(ks[18], (DEPTH, N_EXPERTS, 2 * D_FF), 0.01),
        'w_down': nrm(ks[19], (DEPTH, N_EXPERTS, D_FF, D_MODEL), D_FF ** -0.5),
        'b_down': nrm(ks[20], (DEPTH, N_EXPERTS, D_MODEL), 0.01),
        'g_final': 1.0 + nrm(ks[21], (D_MODEL,), 0.02),
    }


def reference(x_prompt, x_sample, state_gla, cache_kv_w128, cache_kv_w512, cache_kv_w2048,
              g_norm_mix, w_in, w_gk_up, b_gk_up, g_gla_norm, w_branch_gla, w_branch_attn, w_out,
              g_norm_ffn, w_router, b_router, w_gate_up, b_gate_up, w_down, b_down, g_final):
    slopes = alibi_slopes()
    x_p, x_s = x_prompt, x_sample
    st_p, st_s, kv_p, kv_s = [], [], [], []
    for l in range(DEPTH):
        mix_w = (g_norm_mix[l], w_in[l], w_gk_up[l], b_gk_up[l], g_gla_norm[l], w_branch_gla[l], w_branch_attn[l], w_out[l])
        zero_state = jnp.zeros((x_p.shape[0], GLA_HEADS, GLA_HEAD_K, GLA_HEAD_V), x_p.dtype)
        x_p, s_p, b_p = token_mixing(x_p, zero_state, None, slopes, *mix_w)
        x_s, s_s, b_s = token_mixing(x_s, state_gla[l], (cache_kv_w128[l], cache_kv_w512[l], cache_kv_w2048[l]), slopes, *mix_w)
        x_p, x_s = channel_mixing(x_p, x_s, g_norm_ffn[l], w_router[l], b_router[l], w_gate_up[l], b_gate_up[l], w_down[l], b_down[l])
        st_p.append(s_p)
        st_s.append(s_s)
        kv_p.append(b_p)
        kv_s.append(b_s)
    y_prompt = rmsnorm(x_p, g_final)
    y_sample = rmsnorm(x_s, g_final)
    state_gla_prompt = jnp.stack(st_p)
    kv_w128_prompt = jnp.stack([b[0] for b in kv_p])
    kv_w512_prompt = jnp.stack([b[1] for b in kv_p])
    kv_w2048_prompt = jnp.stack([b[2] for b in kv_p])
    state_gla_sample = jnp.stack(st_s)
    kv_w128_sample = jnp.stack([b[0] for b in kv_s])
    kv_w512_sample = jnp.stack([b[1] for b in kv_s])
    kv_w2048_sample = jnp.stack([b[2] for b in kv_s])
    return (y_prompt, y_sample, state_gla_prompt, kv_w128_prompt, kv_w512_prompt, kv_w2048_prompt,
            state_gla_sample, kv_w128_sample, kv_w512_sample, kv_w2048_sample)
```

```python
import functools

import jax
import jax.numpy as jnp
from jax import lax
from jax.experimental import pallas as pl
from jax.experimental.pallas import tpu as pltpu

F32 = jnp.float32
BF16 = jnp.bfloat16

D_MODEL = 1024
NORM_EPS = 1e-5
GLA_HEADS = 4
GLA_HEAD_K = 128
GLA_HEAD_V = 256
GLA_GATE_RANK = 16
GLA_GATE_NORMALIZER = 16.0
GLA_CHUNK = 64
ATTN_GROUPS = ((128, 1), (512, 4), (2048, 16))
N_GROUPS = 3
ATTN_HEADS = 4
ATTN_HEAD_DIM = 128
ATTN_GROUP_DIM = ATTN_HEADS * ATTN_HEAD_DIM
TOP_K = 4
SWIGLU_LIMIT = 7.0
SWIGLU_ALPHA = 1.702
NEG_BIG = -1e30
LANES = 128

OFF_QG, OFF_KG, OFF_VG, OFF_RG = 0, 512, 1024, 2048
OFF_GATE_GLA, OFF_GATE_ATTN = 3072, 4096
OFF_QA, OFF_KA, OFF_VA = 5120, 6656, 8192
OFF_GK = 9728
PROJ_W = 10240
PROJ_SPLITS = (512, 512, 1024, 1024, GLA_GATE_RANK, 1536, 1536, 1536, 1024, 1024)

VMEM_LIMIT = 48 * 1024 * 1024


def _cparams(*sem):
    return pltpu.CompilerParams(dimension_semantics=sem, vmem_limit_bytes=VMEM_LIMIT)


def _alibi_slope(group, head):
    n = N_GROUPS * ATTN_HEADS
    return 2.0 ** (-8.0 * (group * ATTN_HEADS + head + 1) / n)


def _log_sigmoid(x):
    return jnp.minimum(x, 0.0) - jnp.log1p(jnp.exp(-jnp.abs(x)))


def _dot_nt(a, b):
    return lax.dot_general(a, b, (((1,), (1,)), ((), ())), preferred_element_type=F32)


def _dot(a, b):
    return jnp.dot(a, b, preferred_element_type=F32)


def _split3(x):
    hi = x.astype(BF16)
    r1 = x - hi.astype(F32)
    mid = r1.astype(BF16)
    lo = (r1 - mid.astype(F32)).astype(BF16)
    return hi, mid, lo


def _proj_kernel(x_ref, g_ref, w_ref, o_ref, h_ref):
    @pl.when(pl.program_id(1) == 0)
    def _():
        x = x_ref[...]
        ms = jnp.mean(x * x, axis=-1, keepdims=True)
        h_ref[...] = (x * lax.rsqrt(ms + NORM_EPS) * g_ref[...]).astype(BF16)

    o_ref[...] = _dot(h_ref[...], w_ref[...]).astype(o_ref.dtype)


def _norm_proj(x2d, g_row, w_bf, tm, tn):
    m, d = x2d.shape
    n = w_bf.shape[1]
    return pl.pallas_call(
        _proj_kernel,
        grid=(m // tm, n // tn),
        in_specs=[pl.BlockSpec((tm, d), lambda i, j: (i, 0)),
                  pl.BlockSpec((1, d), lambda i, j: (0, 0)),
                  pl.BlockSpec((d, tn), lambda i, j: (0, j))],
        out_specs=pl.BlockSpec((tm, tn), lambda i, j: (i, j)),
        out_shape=jax.ShapeDtypeStruct((m, n), BF16),
        scratch_shapes=[pltpu.VMEM((tm, d), BF16)],
        compiler_params=_cparams("parallel", "arbitrary"),
        name="norm_proj",
    )(x2d, g_row, w_bf)


def _gla_kernel(q_ref, k_ref, v_ref, r_ref, gk_ref, wup_ref, bup_ref, gn_ref,
                u_ref, st_ref, s_scr, o_scr, *, n_chunks):
    t = pl.program_id(2)

    @pl.when(t == 0)
    def _():
        s_scr[...] = jnp.zeros_like(s_scr)

    gk = _dot(gk_ref[...], wup_ref[...]) + bup_ref[...]
    log_a = _log_sigmoid(gk) * (1.0 / GLA_GATE_NORMALIZER)
    c_sz = GLA_CHUNK
    row = lax.broadcasted_iota(jnp.int32, (c_sz, c_sz), 0)
    col = lax.broadcasted_iota(jnp.int32, (c_sz, c_sz), 1)
    tri = row >= col
    tri_bf = jnp.where(tri, 1.0, 0.0).astype(BF16)
    scale = GLA_HEAD_K ** -0.5

    st = s_scr[...]
    for c in range(n_chunks):
        sl = slice(c * c_sz, (c + 1) * c_sz)
        g_hi, g_mid, g_lo = _split3(log_a[sl])
        b = _dot(tri_bf, g_hi) + _dot(tri_bf, g_mid) + _dot(tri_bf, g_lo)
        b_last = b[c_sz - 1:c_sz, :]
        q = q_ref[sl, :].astype(F32) * scale
        k = k_ref[sl, :].astype(F32)
        v_bf = v_ref[sl, :]
        q_in = (q * jnp.exp(b)).astype(BF16)
        k_in = (k * jnp.exp(-b)).astype(BF16)
        k_out = (k * jnp.exp(b_last - b)).astype(BF16)
        a = jnp.where(tri, _dot_nt(q_in, k_in), 0.0)
        o = _dot_nt(q_in, st.astype(BF16)) + _dot(a.astype(BF16), v_bf)
        st = st * jnp.exp(b_last) + pl.dot(v_bf, k_out, trans_a=True)
        o_scr[sl, :] = o
    s_scr[...] = st

    o = o_scr[...]
    ms = jnp.mean(o * o, axis=-1, keepdims=True)
    r = r_ref[...].astype(F32)
    u = o * lax.rsqrt(ms + NORM_EPS) * gn_ref[...] * (r * jax.nn.sigmoid(r))
    u_ref[...] = u.astype(u_ref.dtype)

    @pl.when(t == pl.num_programs(2) - 1)
    def _():
        st_ref[0, 0] = st


def _gla_prompt(proj, w_up_bf, b_up, g_norm, batch, seq, tc):
    nt = seq // tc
    kb, vb = GLA_HEAD_K, GLA_HEAD_V
    rows = lambda b, h, t: b * nt + t
    kern = functools.partial(_gla_kernel, n_chunks=tc // GLA_CHUNK)
    return pl.pallas_call(
        kern,
        grid=(batch, GLA_HEADS, nt),
        in_specs=[
            pl.BlockSpec((tc, kb), lambda b, h, t: (rows(b, h, t), OFF_QG // kb + h)),
            pl.BlockSpec((tc, kb), lambda b, h, t: (rows(b, h, t), OFF_KG // kb + h)),
            pl.BlockSpec((tc, vb), lambda b, h, t: (rows(b, h, t), OFF_VG // vb + h)),
            pl.BlockSpec((tc, vb), lambda b, h, t: (rows(b, h, t), OFF_RG // vb + h)),
            pl.BlockSpec((tc, LANES), lambda b, h, t: (rows(b, h, t), OFF_GK // LANES)),
            pl.BlockSpec((LANES, kb), lambda b, h, t: (0, h)),
            pl.BlockSpec((1, kb), lambda b, h, t: (0, h)),
            pl.BlockSpec((1, vb), lambda b, h, t: (0, 0)),
        ],
        out_specs=[
            pl.BlockSpec((tc, vb), lambda b, h, t: (rows(b, h, t), h)),
            pl.BlockSpec((1, 1, vb, kb), lambda b, h, t: (b, h, 0, 0)),
        ],
        out_shape=[jax.ShapeDtypeStruct((batch * seq, GLA_HEADS * vb), BF16),
                   jax.ShapeDtypeStruct((batch, GLA_HEADS, vb, kb), F32)],
        scratch_shapes=[pltpu.VMEM((vb, kb), F32), pltpu.VMEM((tc, vb), F32)],
        compiler_params=_cparams("parallel", "parallel", "arbitrary"),
        name="gla_prompt",
    )(proj, proj, proj, proj, proj, w_up_bf, b_up, g_norm)


def _attn_kernel(q_ref, kc_ref, kp_ref, vc_ref, vp_ref, o_ref, l_ref, *, group, dilation, blk, n_qb):
    n = pl.program_id(2)
    e = ATTN_HEAD_DIM
    i_idx = lax.broadcasted_iota(jnp.int32, (blk, 2 * blk), 0)
    j_idx = lax.broadcasted_iota(jnp.int32, (blk, 2 * blk), 1)
    rel = i_idx + blk - j_idx
    band = (rel >= 0) & (rel <= blk)
    dist = (dilation * rel).astype(F32)
    for qb in range(n_qb):
        rows = slice(qb * blk, (qb + 1) * blk)
        q = q_ref[0, rows, :]
        if qb == 0:
            k_prev, v_prev = kp_ref[0], vp_ref[0]
            valid = band & ((j_idx >= blk) | (n > 0))
        else:
            prev = slice((qb - 1) * blk, qb * blk)
            k_prev, v_prev = kc_ref[0, prev, :], vc_ref[0, prev, :]
            valid = band
        kk = jnp.concatenate([k_prev, kc_ref[0, rows, :]], axis=0)
        vv = jnp.concatenate([v_prev, vc_ref[0, rows, :]], axis=0)
        for h in range(ATTN_HEADS):
            cols = slice(h * e, (h + 1) * e)
            s = _dot_nt(q[:, cols], kk[:, cols]) * (e ** -0.5) - _alibi_slope(group, h) * dist
            s = jnp.where(valid, s, NEG_BIG)
            m = jnp.max(s, axis=-1, keepdims=True)
            p = jnp.exp(s - m)
            den = jnp.sum(p, axis=-1, keepdims=True)
            o = _dot(p.astype(BF16), vv[:, cols]) / den
            o_ref[0, rows, cols] = o.astype(o_ref.dtype)
            l_ref[0, rows, cols] = jnp.broadcast_to(m + jnp.log(den), (blk, e))


def _attn_prompt(proj, group, batch, seq, tq):
    window, dilation = ATTN_GROUPS[group]
    blk = window // dilation
    n_sub = seq // dilation
    n_qb = tq // blk
    gw = ATTN_GROUP_DIM
    per_res = PROJ_W // gw
    proj3 = proj.reshape(batch, n_sub, dilation * PROJ_W)
    qc, kc, vc = OFF_QA // gw + group, OFF_KA // gw + group, OFF_VA // gw + group
    cur = lambda c: pl.BlockSpec((1, tq, gw), lambda b, r, n: (b, n, r * per_res + c))
    prev = lambda c: pl.BlockSpec((1, blk, gw), lambda b, r, n: (b, jnp.maximum(n * n_qb - 1, 0), r * per_res + c))
    kern = functools.partial(_attn_kernel, group=group, dilation=dilation, blk=blk, n_qb=n_qb)
    out_spec = pl.BlockSpec((1, tq, gw), lambda b, r, n: (b, n, r))
    o, lse = pl.pallas_call(
        kern,
        grid=(batch, dilation, n_sub // tq),
        in_specs=[cur(qc), cur(kc), prev(kc), cur(vc), prev(vc)],
        out_specs=[out_spec, out_spec],
        out_shape=[jax.ShapeDtypeStruct((batch, n_sub, dilation * gw), BF16),
                   jax.ShapeDtypeStruct((batch, n_sub, dilation * gw), F32)],
        compiler_params=_cparams("parallel", "parallel", "arbitrary"),
        name=f"attn_prompt_g{group}",
    )(proj3, proj3, proj3, proj3, proj3)
    return o.reshape(batch * seq, gw), lse.reshape(batch * seq, gw)


def _merge_kernel(x_ref, u_ref, gg_ref, ga_ref, o1_ref, o2_ref, o3_ref, l1_ref, l2_ref, l3_ref,
                  wbg_ref, wba_ref, wo_ref, gf_ref, wr_hi_ref, wr_lo_ref, br_ref,
                  x1_ref, h2_ref, gates_ref, *, n_experts):
    l1, l2, l3 = l1_ref[...], l2_ref[...], l3_ref[...]
    lm = jnp.maximum(jnp.maximum(l1, l2), l3)
    e1, e2, e3 = jnp.exp(l1 - lm), jnp.exp(l2 - lm), jnp.exp(l3 - lm)
    o_att = (e1 * o1_ref[...].astype(F32) + e2 * o2_ref[...].astype(F32) + e3 * o3_ref[...].astype(F32)) / (e1 + e2 + e3)
    merged = (jax.nn.sigmoid(gg_ref[...].astype(F32)) * _dot(u_ref[...], wbg_ref[...])
              + jax.nn.sigmoid(ga_ref[...].astype(F32)) * _dot(o_att.astype(BF16), wba_ref[...]))
    x1 = x_ref[...] + _dot(merged.astype(BF16), wo_ref[...])
    x1_ref[...] = x1
    ms = jnp.mean(x1 * x1, axis=-1, keepdims=True)
    h2 = x1 * lax.rsqrt(ms + NORM_EPS) * gf_ref[...]
    h2_ref[...] = h2.astype(BF16)

    h_hi, h_mid, _ = _split3(h2)
    logits = (_dot(h_hi, wr_hi_ref[...]) + _dot(h_mid, wr_hi_ref[...]) + _dot(h_hi, wr_lo_ref[...])) + br_ref[...]
    lane = lax.broadcasted_iota(jnp.int32, logits.shape, 1).astype(F32)
    cur = jnp.where(lane < n_experts, logits, -jnp.inf)
    tops, sels = [], []
    for _ in range(TOP_K):
        m = jnp.max(cur, axis=-1, keepdims=True)
        idx = jnp.min(jnp.where(cur == m, lane, float(LANES)), axis=-1, keepdims=True)
        sel = lane == idx
        tops.append(m)
        sels.append(sel)
        cur = jnp.where(sel, -jnp.inf, cur)
    ex = [jnp.exp(m - tops[0]) for m in tops]
    den = ex[0] + ex[1] + ex[2] + ex[3]
    gates = jnp.zeros_like(logits)
    for sel, w in zip(sels, ex):
        gates = gates + jnp.where(sel, w / den, 0.0)
    gates_ref[...] = gates


def _merge(x2d, u, proj, outs, lses, wbg, wba, wo, g_ffn, wr_hi, wr_lo, b_r, n_experts, tm):
    m, d = x2d.shape
    gw = ATTN_GROUP_DIM
    row = lambda w: pl.BlockSpec((tm, w), lambda i: (i, 0))
    full = lambda a: pl.BlockSpec(a.shape, lambda i: (0, 0))
    kern = functools.partial(_merge_kernel, n_experts=n_experts)
    return pl.pallas_call(
        kern,
        grid=(m // tm,),
        in_specs=[row(d), row(d),
                  pl.BlockSpec((tm, d), lambda i: (i, OFF_GATE_GLA // d)),
                  pl.BlockSpec((tm, d), lambda i: (i, OFF_GATE_ATTN // d)),
                  row(gw), row(gw), row(gw), row(gw), row(gw), row(gw),
                  full(wbg), full(wba), full(wo), full(g_ffn), full(wr_hi), full(wr_lo), full(b_r)],
        out_specs=[row(d), row(d), row(LANES)],
        out_shape=[jax.ShapeDtypeStruct((m, d), F32), jax.ShapeDtypeStruct((m, d), BF16),
                   jax.ShapeDtypeStruct((m, LANES), F32)],
        compiler_params=_cparams("parallel"),
        name="merge_router",
    )(x2d, u, proj, proj, *outs, *lses, wbg, wba, wo, g_ffn, wr_hi, wr_lo, b_r)


def _moe_kernel(h_ref, gates_ref, x1_ref, wgu_ref, bgu_ref, wd_ref, bd_ref, gfin_ref, y_ref, acc_ref, *, d_ff, sub):
    e = pl.program_id(1)

    @pl.when(e == 0)
    def _():
        acc_ref[...] = jnp.zeros_like(acc_ref)

    tm = h_ref.shape[0]
    lane = lax.broadcasted_iota(jnp.int32, (sub, LANES), 1)
    for s in range(tm // sub):
        rows = slice(s * sub, (s + 1) * sub)
        gcol = jnp.sum(jnp.where(lane == e, gates_ref[rows, :], 0.0), axis=-1, keepdims=True)
        gu = _dot(h_ref[rows, :], wgu_ref[0]) + bgu_ref[0]
        gate = jnp.minimum(gu[:, :d_ff], SWIGLU_LIMIT)
        up = jnp.clip(gu[:, d_ff:], -SWIGLU_LIMIT, SWIGLU_LIMIT)
        act = (up + 1.0) * gate * jax.nn.sigmoid(SWIGLU_ALPHA * gate)
        out = _dot(act.astype(BF16), wd_ref[0]) + bd_ref[0]
        acc_ref[rows, :] += gcol * out

    @pl.when(e == pl.num_programs(1) - 1)
    def _():
        x2 = x1_ref[...] + acc_ref[...]
        ms = jnp.mean(x2 * x2, axis=-1, keepdims=True)
        y_ref[...] = x2 * lax.rsqrt(ms + NORM_EPS) * gfin_ref[...]


def _moe_dense(h2, gates, x1, wgu_bf, bgu, wd_bf, bd, g_final, tm):
    m, d = h2.shape
    n_e, _, two_ff = wgu_bf.shape
    d_ff = two_ff // 2
    sub = min(tm, 512)
    kern = functools.partial(_moe_kernel, d_ff=d_ff, sub=sub)
    return pl.pallas_call(
        kern,
        grid=(m // tm, n_e),
        in_specs=[pl.BlockSpec((tm, d), lambda i, e: (i, 0)),
                  pl.BlockSpec((tm, LANES), lambda i, e: (i, 0)),
                  pl.BlockSpec((tm, d), lambda i, e: (i, 0)),
                  pl.BlockSpec((1, d, two_ff), lambda i, e: (e, 0, 0)),
                  pl.BlockSpec((1, 1, two_ff), lambda i, e: (e, 0, 0)),
                  pl.BlockSpec((1, d_ff, d), lambda i, e: (e, 0, 0)),
                  pl.BlockSpec((1, 1, d), lambda i, e: (e, 0, 0)),
                  pl.BlockSpec((1, d), lambda i, e: (0, 0))],
        out_specs=pl.BlockSpec((tm, d), lambda i, e: (i, 0)),
        out_shape=jax.ShapeDtypeStruct((m, d), F32),
        scratch_shapes=[pltpu.VMEM((tm, d), F32)],
        compiler_params=_cparams("parallel", "arbitrary"),
        name="moe_dense",
    )(h2, gates, x1, wgu_bf, bgu, wd_bf, bd, g_final)


def _column(x_row):
    return jnp.transpose(jnp.broadcast_to(x_row, (LANES, LANES)))


def _gla_step_kernel(p_ref, s_ref, wup_ref, bup_ref, gn_ref, u_ref, so_ref):
    row = p_ref[0]
    gk8 = jnp.broadcast_to(row[:, OFF_GK:OFF_GK + LANES], (8, LANES))
    gk = _dot(gk8, wup_ref[...]) + bup_ref[...]
    decay = jnp.exp(_log_sigmoid(gk[0:1, :]) * (1.0 / GLA_GATE_NORMALIZER))
    kd, vd = GLA_HEAD_K, GLA_HEAD_V
    for h in range(GLA_HEADS):
        q = row[:, OFF_QG + h * kd:OFF_QG + (h + 1) * kd].astype(F32) * (kd ** -0.5)
        k = row[:, OFF_KG + h * kd:OFF_KG + (h + 1) * kd].astype(F32)
        v = row[:, OFF_VG + h * vd:OFF_VG + (h + 1) * vd].astype(F32)
        r = row[:, OFF_RG + h * vd:OFF_RG + (h + 1) * vd].astype(F32)
        a_c = _column(decay[:, h * kd:(h + 1) * kd])
        k_c = _column(k)
        q_c = _column(q)
        wide = lambda c: jnp.concatenate([c, c], axis=1)
        s_new = s_ref[0, h] * wide(a_c) + wide(k_c) * v
        so_ref[0, h] = s_new
        o = jnp.sum(wide(q_c) * s_new, axis=0, keepdims=True)
        ms = jnp.mean(o * o, axis=-1, keepdims=True)
        u = o * lax.rsqrt(ms + NORM_EPS) * gn_ref[...] * (r * jax.nn.sigmoid(r))
        u_ref[0, :, h * vd:(h + 1) * vd] = u.astype(u_ref.dtype)


def _gla_sample(proj_s3, state, w_up_bf, b_up, g_norm):
    bd = proj_s3.shape[0]
    full = lambda a: pl.BlockSpec(a.shape, lambda b: (0,) * a.ndim)
    st_spec = pl.BlockSpec((1, GLA_HEADS, GLA_HEAD_K, GLA_HEAD_V), lambda b: (b, 0, 0, 0))
    return pl.pallas_call(
        _gla_step_kernel,
        grid=(bd,),
        in_specs=[pl.BlockSpec((1, 1, PROJ_W), lambda b: (b, 0, 0)), st_spec,
                  full(w_up_bf), full(b_up), full(g_norm)],
        out_specs=[pl.BlockSpec((1, 1, GLA_HEADS * GLA_HEAD_V), lambda b: (b, 0, 0)), st_spec],
        out_shape=[jax.ShapeDtypeStruct((bd, 1, GLA_HEADS * GLA_HEAD_V), BF16),
                   jax.ShapeDtypeStruct(state.shape, state.dtype)],
        compiler_params=_cparams("parallel"),
        name="gla_sample",
    )(proj_s3, state, w_up_bf, b_up, g_norm)


def _attn_step_kernel(p_ref, c1_ref, c2_ref, c3_ref, o_ref, l_ref):
    row = p_ref[0]
    gw, e = ATTN_GROUP_DIM, ATTN_HEAD_DIM
    nr = 16
    sub8 = lax.broadcasted_iota(jnp.int32, (nr, gw), 0)
    lane8 = lax.broadcasted_iota(jnp.int32, (nr, gw), 1)
    own = (lane8 // e) == sub8
    for g, c_ref in enumerate((c1_ref, c2_ref, c3_ref)):
        window, dilation = ATTN_GROUPS[g]
        blk = window // dilation
        q = row[:, OFF_QA + g * gw:OFF_QA + (g + 1) * gw].astype(F32)
        k_new = row[:, OFF_KA + g * gw:OFF_KA + (g + 1) * gw].astype(F32)
        v_new = row[:, OFF_VA + g * gw:OFF_VA + (g + 1) * gw].astype(F32)
        qm = jnp.where(own, jnp.broadcast_to(q, (nr, gw)), 0.0)
        cache = c_ref[0]
        k_c = cache[:, :gw].astype(BF16)
        v_c = cache[:, gw:].astype(BF16)
        key = lax.broadcasted_iota(jnp.int32, (nr, blk), 1)
        head = lax.broadcasted_iota(jnp.int32, (nr, blk), 0)
        slope = jnp.zeros((nr, blk), F32)
        for h in range(ATTN_HEADS):
            slope = jnp.where(head == h, _alibi_slope(g, h), slope)
        dist = (dilation * (blk - key)).astype(F32)
        s = _dot_nt(qm.astype(BF16), k_c) * (e ** -0.5) - slope * dist
        s0 = jnp.sum(qm * k_new, axis=-1, keepdims=True) * (e ** -0.5)
        m = jnp.maximum(jnp.max(s, axis=-1, keepdims=True), s0)
        p = jnp.exp(s - m)
        p0 = jnp.exp(s0 - m)
        den = jnp.sum(p, axis=-1, keepdims=True) + p0
        o8 = (_dot(p.astype(BF16), v_c) + p0 * v_new) / den
        lse8 = jnp.broadcast_to(m + jnp.log(den), (nr, gw))
        o_ref[0, :, g * gw:(g + 1) * gw] = jnp.sum(jnp.where(own, o8, 0.0), axis=0, keepdims=True).astype(o_ref.dtype)
        l_ref[0, :, g * gw:(g + 1) * gw] = jnp.sum(jnp.where(own, lse8, 0.0), axis=0, keepdims=True)


def _attn_sample(proj_s3, caches):
    bd = proj_s3.shape[0]
    gw = ATTN_GROUP_DIM
    views, specs = [], []
    for g, cache in enumerate(caches):
        window, dilation = ATTN_GROUPS[g]
        n_buf = cache.shape[1]
        assert n_buf == window, "cache must hold exactly one window of rows"
        views.append(cache.reshape(bd, n_buf // dilation, dilation * 2 * gw))
        specs.append(pl.BlockSpec((1, n_buf // dilation, 2 * gw), lambda b: (b, 0, 0)))
    out_spec = pl.BlockSpec((1, 1, N_GROUPS * gw), lambda b: (b, 0, 0))
    return pl.pallas_call(
        _attn_step_kernel,
        grid=(bd,),
        in_specs=[pl.BlockSpec((1, 1, PROJ_W), lambda b: (b, 0, 0))] + specs,
        out_specs=[out_spec, out_spec],
        out_shape=[jax.ShapeDtypeStruct((bd, 1, N_GROUPS * gw), BF16),
                   jax.ShapeDtypeStruct((bd, 1, N_GROUPS * gw), F32)],
        compiler_params=_cparams("parallel"),
        name="attn_sample",
    )(proj_s3, *views)


def _prep_w_in(w_in):
    offs = []
    acc = 0
    for s in PROJ_SPLITS[:-1]:
        acc += s
        offs.append(acc)
    q_g, k_g, v_g, r_g, gk, q_a, k_a, v_a, gg, ga = jnp.split(w_in, offs, axis=1)
    pad = jnp.zeros((w_in.shape[0], PROJ_W - OFF_GK - GLA_GATE_RANK), w_in.dtype)
    return jnp.concatenate([q_g, k_g, v_g, r_g, gg, ga, q_a, k_a, v_a, gk, pad], axis=1).astype(BF16)


def _kv_rows(proj3, group, n_keep):
    b, t, _ = proj3.shape
    gw = ATTN_GROUP_DIM
    k = proj3[:, t - n_keep:, OFF_KA + group * gw:OFF_KA + (group + 1) * gw]
    v = proj3[:, t - n_keep:, OFF_VA + group * gw:OFF_VA + (group + 1) * gw]
    kv = jnp.stack([k, v], axis=2).astype(F32)
    return kv.reshape(b, n_keep, 2, ATTN_HEADS, ATTN_HEAD_DIM)


def kernel(x_prompt, x_sample, state_gla, cache_kv_w128, cache_kv_w512, cache_kv_w2048, g_norm_mix, w_in, w_gk_up,
           b_gk_up, g_gla_norm, w_branch_gla, w_branch_attn, w_out, g_norm_ffn, w_router, b_router, w_gate_up,
           b_gate_up, w_down, b_down, g_final):
    depth = g_norm_mix.shape[0]
    assert depth == 1, "single-layer trunk"
    batch, seq, d = x_prompt.shape
    bd, dec_seq, _ = x_sample.shape
    assert d == D_MODEL and dec_seq == 1 and seq % ATTN_GROUPS[-1][0] == 0
    n_experts = w_router.shape[-1]
    caches = (cache_kv_w128[0], cache_kv_w512[0], cache_kv_w2048[0])

    w_in_bf = _prep_w_in(w_in[0])
    w_up_bf = jnp.zeros((LANES, GLA_HEADS * GLA_HEAD_K), BF16).at[:GLA_GATE_RANK].set(w_gk_up[0].astype(BF16))
    b_up = b_gk_up[0][None, :]
    g_mix = g_norm_mix[0][None, :]
    g_gla = g_gla_norm[0][None, :]
    wbg, wba, wo = w_branch_gla[0].astype(BF16), w_branch_attn[0].astype(BF16), w_out[0].astype(BF16)
    g_ffn = g_norm_ffn[0][None, :]
    w_r = jnp.zeros((d, LANES), F32).at[:, :n_experts].set(w_router[0])
    wr_hi = w_r.astype(BF16)
    wr_lo = (w_r - wr_hi.astype(F32)).astype(BF16)
    b_r = jnp.zeros((1, LANES), F32).at[0, :n_experts].set(b_router[0])
    wgu_bf, wd_bf = w_gate_up[0].astype(BF16), w_down[0].astype(BF16)
    bgu, bdn = b_gate_up[0][:, None, :], b_down[0][:, None, :]
    g_fin = g_final[None, :]

    xp = x_prompt.reshape(batch * seq, d)
    proj = _norm_proj(xp, g_mix, w_in_bf, tm=1024, tn=2048)
    u_p, st_p = _gla_prompt(proj, w_up_bf, b_up, g_gla, batch, seq, tc=512)
    outs, lses = [], []
    for g in range(N_GROUPS):
        o, lse = _attn_prompt(proj, g, batch, seq, tq=min(256, seq // ATTN_GROUPS[g][1]))
        outs.append(o)
        lses.append(lse)
    x1_p, h2_p, gates_p = _merge(xp, u_p, proj, outs, lses, wbg, wba, wo, g_ffn, wr_hi, wr_lo, b_r, n_experts, tm=512)

    xs = x_sample.reshape(bd, d)
    proj_s = _norm_proj(xs, g_mix, w_in_bf, tm=bd, tn=2048)
    proj_s3 = proj_s.reshape(bd, 1, PROJ_W)
    u_s, st_s = _gla_sample(proj_s3, state_gla[0], w_up_bf, b_up, g_gla)
    o_s, l_s = _attn_sample(proj_s3, caches)
    gw = ATTN_GROUP_DIM
    o_s, l_s = o_s.reshape(bd, N_GROUPS * gw), l_s.reshape(bd, N_GROUPS * gw)
    outs_s = [o_s[:, g * gw:(g + 1) * gw] for g in range(N_GROUPS)]
    lses_s = [l_s[:, g * gw:(g + 1) * gw] for g in range(N_GROUPS)]
    x1_s, h2_s, gates_s = _merge(xs, u_s.reshape(bd, d), proj_s, outs_s, lses_s, wbg, wba, wo, g_ffn,
                                 wr_hi, wr_lo, b_r, n_experts, tm=bd)

    y_p = _moe_dense(h2_p, gates_p, x1_p, wgu_bf, bgu, wd_bf, bdn, g_fin, tm=1024)
    y_s = _moe_dense(h2_s, gates_s, x1_s, wgu_bf, bgu, wd_bf, bdn, g_fin, tm=bd)

    proj_p3 = proj.reshape(batch, seq, PROJ_W)
    kv_p = [_kv_rows(proj_p3, g, min(ATTN_GROUPS[g][0], seq))[None] for g in range(N_GROUPS)]
    kv_new = proj_s.reshape(bd, 1, PROJ_W)
    kv_s = [jnp.concatenate([caches[g][:, 1:], _kv_rows(kv_new, g, 1).astype(caches[g].dtype)], axis=1)[None]
            for g in range(N_GROUPS)]
    return (y_p.reshape(batch, seq, d), y_s.reshape(bd, 1, d),
            jnp.swapaxes(st_p, -1, -2)[None], kv_p[0], kv_p[1], kv_p[2],
            st_s[None], kv_s[0], kv_s[1], kv_s[2])
```

```python
import functools

import jax
import jax.numpy as jnp
from jax import lax
from jax.experimental import pallas as pl
from jax.experimental.pallas import tpu as pltpu

F32 = jnp.float32
BF16 = jnp.bfloat16
HIGHEST = lax.Precision.HIGHEST

D_MODEL = 1024
NORM_EPS = 1e-5
GLA_HEADS = 4
GLA_HEAD_K = 128
GLA_HEAD_V = 256
GLA_GATE_RANK = 16
GLA_GATE_NORMALIZER = 16.0
GLA_CHUNK = 64
ATTN_GROUPS = ((128, 1), (512, 4), (2048, 16))
N_GROUPS = 3
ATTN_HEADS = 4
ATTN_HEAD_DIM = 128
ATTN_GROUP_DIM = ATTN_HEADS * ATTN_HEAD_DIM
TOP_K = 4
SWIGLU_LIMIT = 7.0
SWIGLU_ALPHA = 1.702
NEG_BIG = -1e30
LANES = 128

OFF_QG, OFF_KG, OFF_VG, OFF_RG = 0, 512, 1024, 2048
OFF_GATE_GLA, OFF_GATE_ATTN = 3072, 4096
OFF_QKV0 = 5120
OFF_GK = 6656
NAT_W = 7168
QKV_W = 3 * ATTN_GROUP_DIM
PROJ_W = NAT_W + 2 * QKV_W
OFF_QKV = (OFF_QKV0, NAT_W, NAT_W + QKV_W)
PROJ_SPLITS = (512, 512, 1024, 1024, GLA_GATE_RANK, 1536, 1536, 1536, 1024, 1024)

VMEM_LIMIT = 48 * 1024 * 1024


def _cparams(*sem):
    return pltpu.CompilerParams(dimension_semantics=sem, vmem_limit_bytes=VMEM_LIMIT)


def _alibi_slope(group, head):
    n = N_GROUPS * ATTN_HEADS
    return 2.0 ** (-8.0 * (group * ATTN_HEADS + head + 1) / n)


def _log_sigmoid(x):
    return jnp.minimum(x, 0.0) - jnp.log1p(jnp.exp(-jnp.abs(x)))


def _dot_nt(a, b):
    return lax.dot_general(a, b, (((1,), (1,)), ((), ())), preferred_element_type=F32)


def _dot(a, b):
    return jnp.dot(a, b, preferred_element_type=F32)


def _dot_hi(a, b):
    return jnp.dot(a, b, preferred_element_type=F32, precision=HIGHEST)


def _split3(x):
    hi = x.astype(BF16)
    r1 = x - hi.astype(F32)
    mid = r1.astype(BF16)
    lo = (r1 - mid.astype(F32)).astype(BF16)
    return hi, mid, lo


def _rmsnorm(x, g):
    return x * lax.rsqrt(jnp.mean(x * x, axis=-1, keepdims=True) + NORM_EPS) * g


def _proj_kernel(x_ref, g_ref, w_ref, o_ref, h_ref, *scr, dilation):
    @pl.when(pl.program_id(1) == 0)
    def _():
        h_ref[...] = _rmsnorm(x_ref[...], g_ref[...]).astype(BF16)

    acc = _dot(h_ref[...], w_ref[...])
    if dilation == 1:
        o_ref[0, 0] = acc.astype(o_ref.dtype)
    else:
        n = acc.shape[0] // dilation
        for c in range(acc.shape[1] // LANES):
            cols = slice(c * LANES, (c + 1) * LANES)
            scr[0][c] = acc[:, cols]
            for r in range(dilation):
                o_ref[0, r, :, cols] = scr[0][c, pl.ds(r, n, stride=dilation), :].astype(o_ref.dtype)


def _norm_proj(x2d, g_row, w_bf, batch, seq, dilation, tm, tn):
    m, d = x2d.shape
    n = w_bf.shape[1]
    tps = seq // tm
    scratch = [pltpu.VMEM((tm, d), BF16)]
    if dilation > 1:
        scratch.append(pltpu.VMEM((tn // LANES, tm, LANES), F32))
    return pl.pallas_call(
        functools.partial(_proj_kernel, dilation=dilation),
        grid=(m // tm, n // tn),
        in_specs=[pl.BlockSpec((tm, d), lambda i, j: (i, 0)),
                  pl.BlockSpec((1, d), lambda i, j: (0, 0)),
                  pl.BlockSpec((d, tn), lambda i, j: (0, j))],
        out_specs=pl.BlockSpec((1, dilation, tm // dilation, tn), lambda i, j: (i // tps, 0, i % tps, j)),
        out_shape=jax.ShapeDtypeStruct((batch, dilation, seq // dilation, n), BF16),
        scratch_shapes=scratch,
        compiler_params=_cparams("parallel", "arbitrary"),
        name=f"norm_proj_d{dilation}",
    )(x2d, g_row, w_bf)


def _proj_precise_kernel(x_ref, g_ref, w_ref, o_ref):
    o_ref[...] = _dot_hi(_rmsnorm(x_ref[...], g_ref[...]), w_ref[...])


def _norm_proj_precise(x2d, g_row, w_f32, tn):
    m, d = x2d.shape
    n = w_f32.shape[1]
    return pl.pallas_call(
        _proj_precise_kernel,
        grid=(n // tn,),
        in_specs=[pl.BlockSpec((m, d), lambda j: (0, 0)),
                  pl.BlockSpec((1, d), lambda j: (0, 0)),
                  pl.BlockSpec((d, tn), lambda j: (0, j))],
        out_specs=pl.BlockSpec((m, tn), lambda j: (0, j)),
        out_shape=jax.ShapeDtypeStruct((m, n), F32),
        compiler_params=_cparams("parallel"),
        name="norm_proj_sample",
    )(x2d, g_row, w_f32)


def _gla_kernel(q_ref, k_ref, v_ref, r_ref, gk_ref, wup_ref, bup_ref, gn_ref,
                u_ref, st_ref, s_scr, o_scr, *, n_chunks):
    t = pl.program_id(2)

    @pl.when(t == 0)
    def _():
        s_scr[...] = jnp.zeros_like(s_scr)

    gk = _dot(gk_ref[...], wup_ref[...]) + bup_ref[...]
    log_a = _log_sigmoid(gk) * (1.0 / GLA_GATE_NORMALIZER)
    c_sz = GLA_CHUNK
    row = lax.broadcasted_iota(jnp.int32, (c_sz, c_sz), 0)
    col = lax.broadcasted_iota(jnp.int32, (c_sz, c_sz), 1)
    tri = row >= col
    tri_bf = jnp.where(tri, 1.0, 0.0).astype(BF16)
    scale = GLA_HEAD_K ** -0.5

    st = s_scr[...]
    for c in range(n_chunks):
        sl = slice(c * c_sz, (c + 1) * c_sz)
        g_hi, g_mid, g_lo = _split3(log_a[sl])
        b = _dot(tri_bf, g_hi) + _dot(tri_bf, g_mid) + _dot(tri_bf, g_lo)
        b_last = b[c_sz - 1:c_sz, :]
        q = q_ref[sl, :].astype(F32) * scale
        k = k_ref[sl, :].astype(F32)
        v_bf = v_ref[sl, :]
        q_in = (q * jnp.exp(b)).astype(BF16)
        k_in = (k * jnp.exp(-b)).astype(BF16)
        k_out = (k * jnp.exp(b_last - b)).astype(BF16)
        a = jnp.where(tri, _dot_nt(q_in, k_in), 0.0)
        o = _dot_nt(q_in, st.astype(BF16)) + _dot(a.astype(BF16), v_bf)
        st = st * jnp.exp(b_last) + pl.dot(v_bf, k_out, trans_a=True)
        o_scr[sl, :] = o
    s_scr[...] = st

    r = r_ref[...].astype(F32)
    u = _rmsnorm(o_scr[...], gn_ref[...]) * (r * jax.nn.sigmoid(r))
    u_ref[...] = u.astype(u_ref.dtype)

    @pl.when(t == pl.num_programs(2) - 1)
    def _():
        st_ref[0, 0] = st


def _gla_prompt(proj, w_up_bf, b_up, g_norm, batch, seq, tc):
    nt = seq // tc
    kb, vb = GLA_HEAD_K, GLA_HEAD_V
    rows = lambda b, h, t: b * nt + t
    kern = functools.partial(_gla_kernel, n_chunks=tc // GLA_CHUNK)
    return pl.pallas_call(
        kern,
        grid=(batch, GLA_HEADS, nt),
        in_specs=[
            pl.BlockSpec((tc, kb), lambda b, h, t: (rows(b, h, t), OFF_QG // kb + h)),
            pl.BlockSpec((tc, kb), lambda b, h, t: (rows(b, h, t), OFF_KG // kb + h)),
            pl.BlockSpec((tc, vb), lambda b, h, t: (rows(b, h, t), OFF_VG // vb + h)),
            pl.BlockSpec((tc, vb), lambda b, h, t: (rows(b, h, t), OFF_RG // vb + h)),
            pl.BlockSpec((tc, LANES), lambda b, h, t: (rows(b, h, t), OFF_GK // LANES)),
            pl.BlockSpec((LANES, kb), lambda b, h, t: (0, h)),
            pl.BlockSpec((1, kb), lambda b, h, t: (0, h)),
            pl.BlockSpec((1, vb), lambda b, h, t: (0, 0)),
        ],
        out_specs=[
            pl.BlockSpec((tc, vb), lambda b, h, t: (rows(b, h, t), h)),
            pl.BlockSpec((1, 1, vb, kb), lambda b, h, t: (b, h, 0, 0)),
        ],
        out_shape=[jax.ShapeDtypeStruct((batch * seq, GLA_HEADS * vb), BF16),
                   jax.ShapeDtypeStruct((batch, GLA_HEADS, vb, kb), F32)],
        scratch_shapes=[pltpu.VMEM((vb, kb), F32), pltpu.VMEM((tc, vb), F32)],
        compiler_params=_cparams("parallel", "parallel", "arbitrary"),
        name="gla_prompt",
    )(proj, proj, proj, proj, proj, w_up_bf, b_up, g_norm)


def _attn_kernel(q_ref, kc_ref, kp_ref, vc_ref, vp_ref, o_ref, l_ref, *, group, dilation, blk, n_qb):
    n = pl.program_id(2)
    e = ATTN_HEAD_DIM
    i_idx = lax.broadcasted_iota(jnp.int32, (blk, 2 * blk), 0)
    j_idx = lax.broadcasted_iota(jnp.int32, (blk, 2 * blk), 1)
    rel = i_idx + blk - j_idx
    band = (rel >= 0) & (rel <= blk)
    dist = (dilation * rel).astype(F32)
    for qb in range(n_qb):
        rows = slice(qb * blk, (qb + 1) * blk)
        q = q_ref[0, 0, rows, :]
        if qb == 0:
            k_prev, v_prev = kp_ref[0, 0], vp_ref[0, 0]
            valid = band & ((j_idx >= blk) | (n > 0))
        else:
            prev = slice((qb - 1) * blk, qb * blk)
            k_prev, v_prev = kc_ref[0, 0, prev, :], vc_ref[0, 0, prev, :]
            valid = band
        kk = jnp.concatenate([k_prev, kc_ref[0, 0, rows, :]], axis=0)
        vv = jnp.concatenate([v_prev, vc_ref[0, 0, rows, :]], axis=0)
        for h in range(ATTN_HEADS):
            cols = slice(h * e, (h + 1) * e)
            s = _dot_nt(q[:, cols], kk[:, cols]) * (e ** -0.5) - _alibi_slope(group, h) * dist
            s = jnp.where(valid, s, NEG_BIG)
            m = jnp.max(s, axis=-1, keepdims=True)
            p = jnp.exp(s - m)
            den = jnp.sum(p, axis=-1, keepdims=True)
            o = _dot(p.astype(BF16), vv[:, cols]) / den
            o_ref[0, 0, rows, cols] = o.astype(o_ref.dtype)
            l_ref[0, 0, rows, cols] = jnp.broadcast_to(m + jnp.log(den), (blk, e))


def _attn_prompt(src, col0, group, tq):
    batch, dilation, n_sub, _ = src.shape
    window, dil = ATTN_GROUPS[group]
    assert dil == dilation
    blk = window // dilation
    n_qb = tq // blk
    gw = ATTN_GROUP_DIM
    qc, kc, vc = col0 // gw, col0 // gw + 1, col0 // gw + 2
    cur = lambda c: pl.BlockSpec((1, 1, tq, gw), lambda b, r, n: (b, r, n, c))
    prev = lambda c: pl.BlockSpec((1, 1, blk, gw), lambda b, r, n: (b, r, jnp.maximum(n * n_qb - 1, 0), c))
    kern = functools.partial(_attn_kernel, group=group, dilation=dilation, blk=blk, n_qb=n_qb)
    out_spec = pl.BlockSpec((1, 1, tq, gw), lambda b, r, n: (b, r, n, 0))
    return pl.pallas_call(
        kern,
        grid=(batch, dilation, n_sub // tq),
        in_specs=[cur(qc), cur(kc), prev(kc), cur(vc), prev(vc)],
        out_specs=[out_spec, out_spec],
        out_shape=[jax.ShapeDtypeStruct((batch, dilation, n_sub, gw), BF16),
                   jax.ShapeDtypeStruct((batch, dilation, n_sub, gw), F32)],
        compiler_params=_cparams("parallel", "parallel", "arbitrary"),
        name=f"attn_prompt_g{group}",
    )(src, src, src, src, src)


def _merge_kernel(x_ref, u_ref, gg_ref, ga_ref, o1_ref, o2_ref, o3_ref, l1_ref, l2_ref, l3_ref,
                  wbg_ref, wba_ref, wo_ref, gf_ref, wr_hi_ref, wr_lo_ref, br_ref,
                  x1_ref, h2_ref, gates_ref, *scr, n_experts, dilations, precise):
    tm = x_ref.shape[0]
    scr = list(scr)

    def natural(ref, dilation):
        if dilation == 1:
            return ref[0, 0].astype(F32)
        buf = scr.pop(0)
        n = tm // dilation
        for c in range(buf.shape[0]):
            for r in range(dilation):
                buf[c, pl.ds(r, n, stride=dilation), :] = ref[0, r, :, c * LANES:(c + 1) * LANES].astype(F32)
        return jnp.concatenate([buf[c] for c in range(buf.shape[0])], axis=1)

    o1, o2, o3 = (natural(r, d) for r, d in zip((o1_ref, o2_ref, o3_ref), dilations))
    l1, l2, l3 = (natural(r, d) for r, d in zip((l1_ref, l2_ref, l3_ref), dilations))
    mm = _dot_hi if precise else (lambda a, b: _dot(a.astype(BF16), b))
    lm = jnp.maximum(jnp.maximum(l1, l2), l3)
    e1, e2, e3 = jnp.exp(l1 - lm), jnp.exp(l2 - lm), jnp.exp(l3 - lm)
    o_att = (e1 * o1 + e2 * o2 + e3 * o3) / (e1 + e2 + e3)
    merged = (jax.nn.sigmoid(gg_ref[...].astype(F32)) * mm(u_ref[...], wbg_ref[...])
              + jax.nn.sigmoid(ga_ref[...].astype(F32)) * mm(o_att, wba_ref[...]))
    x1 = x_ref[...] + mm(merged, wo_ref[...])
    x1_ref[...] = x1
    h2 = _rmsnorm(x1, gf_ref[...])
    h2_ref[...] = h2.astype(BF16)

    if precise:
        logits = _dot_hi(h2, wr_hi_ref[...]) + br_ref[...]
    else:
        h_hi, h_mid, _ = _split3(h2)
        logits = (_dot(h_hi, wr_hi_ref[...]) + _dot(h_mid, wr_hi_ref[...]) + _dot(h_hi, wr_lo_ref[...])) + br_ref[...]
    lane = lax.broadcasted_iota(jnp.int32, logits.shape, 1).astype(F32)
    cur = jnp.where(lane < n_experts, logits, -jnp.inf)
    tops, sels = [], []
    for _ in range(TOP_K):
        m = jnp.max(cur, axis=-1, keepdims=True)
        idx = jnp.min(jnp.where(cur == m, lane, float(LANES)), axis=-1, keepdims=True)
        sel = lane == idx
        tops.append(m)
        sels.append(sel)
        cur = jnp.where(sel, -jnp.inf, cur)
    ex = [jnp.exp(m - tops[0]) for m in tops]
    den = ex[0] + ex[1] + ex[2] + ex[3]
    gates = jnp.zeros_like(logits)
    for sel, w in zip(sels, ex):
        gates = gates + jnp.where(sel, w / den, 0.0)
    gates_ref[...] = gates


def _merge(x2d, u, gate_src, outs, lses, wbg, wba, wo, g_ffn, wr_hi, wr_lo, b_r, n_experts, seq, tm, precise):
    m, d = x2d.shape
    gw = ATTN_GROUP_DIM
    tps = seq // tm
    dilations = tuple(o.shape[1] for o in outs)
    row = lambda w: pl.BlockSpec((tm, w), lambda i: (i, 0))
    full = lambda a: pl.BlockSpec(a.shape, lambda i: (0, 0))
    grp = lambda dil: pl.BlockSpec((1, dil, tm // dil, gw), lambda i: (i // tps, 0, i % tps, 0))
    kern = functools.partial(_merge_kernel, n_experts=n_experts, dilations=dilations, precise=precise)
    n_scr = 2 * sum(1 for dil in dilations if dil > 1)
    return pl.pallas_call(
        kern,
        grid=(m // tm,),
        in_specs=[row(d), row(d),
                  pl.BlockSpec((tm, d), lambda i: (i, OFF_GATE_GLA // d)),
                  pl.BlockSpec((tm, d), lambda i: (i, OFF_GATE_ATTN // d)),
                  *[grp(dil) for dil in dilations], *[grp(dil) for dil in dilations],
                  full(wbg), full(wba), full(wo), full(g_ffn), full(wr_hi), full(wr_lo), full(b_r)],
        out_specs=[row(d), row(d), row(LANES)],
        out_shape=[jax.ShapeDtypeStruct((m, d), F32), jax.ShapeDtypeStruct((m, d), BF16),
                   jax.ShapeDtypeStruct((m, LANES), F32)],
        scratch_shapes=[pltpu.VMEM((gw // LANES, tm, LANES), F32)] * n_scr,
        compiler_params=_cparams("parallel"),
        name="merge_router_sample" if precise else "merge_router",
    )(x2d, u, gate_src, gate_src, *outs, *lses, wbg, wba, wo, g_ffn, wr_hi, wr_lo, b_r)


def _moe_kernel(h_ref, gates_ref, x1_ref, wgu_ref, bgu_ref, wd_ref, bd_ref, gfin_ref, y_ref, acc_ref, *, d_ff, sub):
    e = pl.program_id(1)

    @pl.when(e == 0)
    def _():
        acc_ref[...] = jnp.zeros_like(acc_ref)

    tm = h_ref.shape[0]
    lane = lax.broadcasted_iota(jnp.int32, (sub, LANES), 1)
    for s in range(tm // sub):
        rows = slice(s * sub, (s + 1) * sub)
        gcol = jnp.sum(jnp.where(lane == e, gates_ref[rows, :], 0.0), axis=-1, keepdims=True)
        gu = _dot(h_ref[rows, :], wgu_ref[0]) + bgu_ref[0]
        gate = jnp.minimum(gu[:, :d_ff], SWIGLU_LIMIT)
        up = jnp.clip(gu[:, d_ff:], -SWIGLU_LIMIT, SWIGLU_LIMIT)
        act = (up + 1.0) * gate * jax.nn.sigmoid(SWIGLU_ALPHA * gate)
        out = _dot(act.astype(BF16), wd_ref[0]) + bd_ref[0]
        acc_ref[rows, :] += gcol * out

    @pl.when(e == pl.num_programs(1) - 1)
    def _():
        y_ref[...] = _rmsnorm(x1_ref[...] + acc_ref[...], gfin_ref[...])


def _moe_dense(h2, gates, x1, wgu_bf, bgu, wd_bf, bd, g_final, tm):
    m, d = h2.shape
    n_e, _, two_ff = wgu_bf.shape
    d_ff = two_ff // 2
    sub = min(tm, 512)
    kern = functools.partial(_moe_kernel, d_ff=d_ff, sub=sub)
    return pl.pallas_call(
        kern,
        grid=(m // tm, n_e),
        in_specs=[pl.BlockSpec((tm, d), lambda i, e: (i, 0)),
                  pl.BlockSpec((tm, LANES), lambda i, e: (i, 0)),
                  pl.BlockSpec((tm, d), lambda i, e: (i, 0)),
                  pl.BlockSpec((1, d, two_ff), lambda i, e: (e, 0, 0)),
                  pl.BlockSpec((1, 1, two_ff), lambda i, e: (e, 0, 0)),
                  pl.BlockSpec((1, d_ff, d), lambda i, e: (e, 0, 0)),
                  pl.BlockSpec((1, 1, d), lambda i, e: (e, 0, 0)),
                  pl.BlockSpec((1, d), lambda i, e: (0, 0))],
        out_specs=pl.BlockSpec((tm, d), lambda i, e: (i, 0)),
        out_shape=jax.ShapeDtypeStruct((m, d), F32),
        scratch_shapes=[pltpu.VMEM((tm, d), F32)],
        compiler_params=_cparams("parallel", "arbitrary"),
        name="moe_dense",
    )(h2, gates, x1, wgu_bf, bgu, wd_bf, bd, g_final)


def _column(x_row):
    return jnp.transpose(jnp.broadcast_to(x_row, (LANES, LANES)))


def _gla_step_kernel(p_ref, s_ref, wup_ref, bup_ref, gn_ref, u_ref, so_ref):
    row = p_ref[0]
    gk8 = jnp.broadcast_to(row[:, OFF_GK:OFF_GK + LANES], (8, LANES))
    gk = _dot_hi(gk8, wup_ref[...]) + bup_ref[...]
    decay = jnp.exp(_log_sigmoid(gk[0:1, :]) * (1.0 / GLA_GATE_NORMALIZER))
    kd, vd = GLA_HEAD_K, GLA_HEAD_V
    wide = lambda c: jnp.concatenate([c, c], axis=1)
    for h in range(GLA_HEADS):
        q = row[:, OFF_QG + h * kd:OFF_QG + (h + 1) * kd] * (kd ** -0.5)
        k = row[:, OFF_KG + h * kd:OFF_KG + (h + 1) * kd]
        v = row[:, OFF_VG + h * vd:OFF_VG + (h + 1) * vd]
        r = row[:, OFF_RG + h * vd:OFF_RG + (h + 1) * vd]
        s_new = s_ref[0, h] * wide(_column(decay[:, h * kd:(h + 1) * kd])) + wide(_column(k)) * v
        so_ref[0, h] = s_new
        o = jnp.sum(wide(_column(q)) * s_new, axis=0, keepdims=True)
        u_ref[0, :, h * vd:(h + 1) * vd] = _rmsnorm(o, gn_ref[...]) * (r * jax.nn.sigmoid(r))


def _gla_sample(proj_s3, state, w_up, b_up, g_norm):
    bd = proj_s3.shape[0]
    full = lambda a: pl.BlockSpec(a.shape, lambda b: (0,) * a.ndim)
    st_spec = pl.BlockSpec((1, GLA_HEADS, GLA_HEAD_K, GLA_HEAD_V), lambda b: (b, 0, 0, 0))
    return pl.pallas_call(
        _gla_step_kernel,
        grid=(bd,),
        in_specs=[pl.BlockSpec((1, 1, PROJ_W), lambda b: (b, 0, 0)), st_spec,
                  full(w_up), full(b_up), full(g_norm)],
        out_specs=[pl.BlockSpec((1, 1, GLA_HEADS * GLA_HEAD_V), lambda b: (b, 0, 0)), st_spec],
        out_shape=[jax.ShapeDtypeStruct((bd, 1, GLA_HEADS * GLA_HEAD_V), F32),
                   jax.ShapeDtypeStruct(state.shape, state.dtype)],
        compiler_params=_cparams("parallel"),
        name="gla_sample",
    )(proj_s3, state, w_up, b_up, g_norm)


def _attn_step_kernel(p_ref, c1_ref, c2_ref, c3_ref, o_ref, l_ref):
    row = p_ref[0]
    gw, e = ATTN_GROUP_DIM, ATTN_HEAD_DIM
    for g, c_ref in enumerate((c1_ref, c2_ref, c3_ref)):
        window, dilation = ATTN_GROUPS[g]
        blk = window // dilation
        back = (dilation * (blk - lax.broadcasted_iota(jnp.int32, (blk, 1), 0))).astype(F32)
        for h in range(ATTN_HEADS):
            lo = OFF_QKV[g] + h * e
            q = row[:, lo:lo + e]
            k_new = row[:, lo + gw:lo + gw + e]
            v_new = row[:, lo + 2 * gw:lo + 2 * gw + e]
            k_c = c_ref[0, :, 0, h, :]
            v_c = c_ref[0, :, 1, h, :]
            s = jnp.sum(k_c * q, axis=-1, keepdims=True) * (e ** -0.5) - _alibi_slope(g, h) * back
            s0 = jnp.sum(k_new * q, axis=-1, keepdims=True) * (e ** -0.5)
            m = jnp.maximum(jnp.max(s, axis=0, keepdims=True), s0)
            p = jnp.exp(s - m)
            p0 = jnp.exp(s0 - m)
            den = jnp.sum(p, axis=0, keepdims=True) + p0
            o = (jnp.sum(p * v_c, axis=0, keepdims=True) + p0 * v_new) / den
            cols = slice(g * gw + h * e, g * gw + (h + 1) * e)
            o_ref[0, :, cols] = o
            l_ref[0, :, cols] = jnp.broadcast_to(m + jnp.log(den), (1, e))


def _attn_sample(proj_s3, caches):
    bd = proj_s3.shape[0]
    gw = ATTN_GROUP_DIM
    views, specs = [], []
    for g, cache in enumerate(caches):
        window, dilation = ATTN_GROUPS[g]
        n_buf = cache.shape[1]
        assert n_buf == window, "cache must hold exactly one window of rows"
        blk = n_buf // dilation
        views.append(cache.reshape(bd, blk, dilation, 2, ATTN_HEADS, ATTN_HEAD_DIM))
        specs.append(pl.BlockSpec((1, blk, None, 2, ATTN_HEADS, ATTN_HEAD_DIM), lambda b: (b, 0, 0, 0, 0, 0)))
    out_spec = pl.BlockSpec((1, 1, N_GROUPS * gw), lambda b: (b, 0, 0))
    return pl.pallas_call(
        _attn_step_kernel,
        grid=(bd,),
        in_specs=[pl.BlockSpec((1, 1, PROJ_W), lambda b: (b, 0, 0))] + specs,
        out_specs=[out_spec, out_spec],
        out_shape=[jax.ShapeDtypeStruct((bd, 1, N_GROUPS * gw), F32),
                   jax.ShapeDtypeStruct((bd, 1, N_GROUPS * gw), F32)],
        compiler_params=_cparams("parallel"),
        name="attn_sample",
    )(proj_s3, *views)


def _prep_w_in(w_in):
    offs, acc = [], 0
    for s in PROJ_SPLITS[:-1]:
        acc += s
        offs.append(acc)
    q_g, k_g, v_g, r_g, gk, q_a, k_a, v_a, gg, ga = jnp.split(w_in, offs, axis=1)
    gw = ATTN_GROUP_DIM
    qkv = lambda g: [a[:, g * gw:(g + 1) * gw] for a in (q_a, k_a, v_a)]
    pad = jnp.zeros((w_in.shape[0], NAT_W - OFF_GK - GLA_GATE_RANK), w_in.dtype)
    return jnp.concatenate([q_g, k_g, v_g, r_g, gg, ga, *qkv(0), gk, pad, *qkv(1), *qkv(2)], axis=1)


def _kv_rows(src, col_k, n_keep):
    b, dil, n_sub, _ = src.shape
    kv = src[:, :, n_sub - n_keep // dil:, col_k:col_k + 2 * ATTN_GROUP_DIM]
    kv = jnp.swapaxes(kv, 1, 2).astype(F32)
    return kv.reshape(b, n_keep, 2, ATTN_HEADS, ATTN_HEAD_DIM)


def kernel(x_prompt, x_sample, state_gla, cache_kv_w128, cache_kv_w512, cache_kv_w2048, g_norm_mix, w_in, w_gk_up,
           b_gk_up, g_gla_norm, w_branch_gla, w_branch_attn, w_out, g_norm_ffn, w_router, b_router, w_gate_up,
           b_gate_up, w_down, b_down, g_final):
    depth = g_norm_mix.shape[0]
    assert depth == 1, "single-layer trunk"
    batch, seq, d = x_prompt.shape
    bd, dec_seq, _ = x_sample.shape
    assert d == D_MODEL and dec_seq == 1 and seq % ATTN_GROUPS[-1][0] == 0
    n_experts = w_router.shape[-1]
    caches = (cache_kv_w128[0], cache_kv_w512[0], cache_kv_w2048[0])
    gw = ATTN_GROUP_DIM

    w_in_f32 = _prep_w_in(w_in[0])
    w_in_bf = w_in_f32.astype(BF16)
    w_up = jnp.zeros((LANES, GLA_HEADS * GLA_HEAD_K), F32).at[:GLA_GATE_RANK].set(w_gk_up[0])
    w_up_bf = w_up.astype(BF16)
    b_up = b_gk_up[0][None, :]
    g_mix = g_norm_mix[0][None, :]
    g_gla = g_gla_norm[0][None, :]
    wbg, wba, wo = w_branch_gla[0], w_branch_attn[0], w_out[0]
    g_ffn = g_norm_ffn[0][None, :]
    w_r = jnp.zeros((d, LANES), F32).at[:, :n_experts].set(w_router[0])
    wr_hi = w_r.astype(BF16)
    wr_lo = (w_r - wr_hi.astype(F32)).astype(BF16)
    b_r = jnp.zeros((1, LANES), F32).at[0, :n_experts].set(b_router[0])
    wgu_bf, wd_bf = w_gate_up[0].astype(BF16), w_down[0].astype(BF16)
    bgu, bdn = b_gate_up[0][:, None, :], b_down[0][:, None, :]
    g_fin = g_final[None, :]

    xp = x_prompt.reshape(batch * seq, d)
    nat = _norm_proj(xp, g_mix, w_in_bf[:, :NAT_W], batch, seq, 1, tm=1024, tn=1024)
    srcs = [nat] + [_norm_proj(xp, g_mix, w_in_bf[:, OFF_QKV[g]:OFF_QKV[g] + QKV_W], batch, seq, ATTN_GROUPS[g][1],
                               tm=1024, tn=QKV_W) for g in (1, 2)]
    nat2d = nat.reshape(batch * seq, NAT_W)
    u_p, st_p = _gla_prompt(nat2d, w_up_bf, b_up, g_gla, batch, seq, tc=512)
    outs, lses = [], []
    for g in range(N_GROUPS):
        col0 = OFF_QKV0 if g == 0 else 0
        o, lse = _attn_prompt(srcs[g], col0, g, tq=min(256, seq // ATTN_GROUPS[g][1]))
        outs.append(o)
        lses.append(lse)
    x1_p, h2_p, gates_p = _merge(xp, u_p, nat2d, outs, lses, wbg.astype(BF16), wba.astype(BF16), wo.astype(BF16),
                                 g_ffn, wr_hi, wr_lo, b_r, n_experts, seq, tm=512, precise=False)

    xs = x_sample.reshape(bd, d)
    proj_s = _norm_proj_precise(xs, g_mix, w_in_f32, tn=1024)
    proj_s3 = proj_s.reshape(bd, 1, PROJ_W)
    u_s, st_s = _gla_sample(proj_s3, state_gla[0], w_up, b_up, g_gla)
    o_s, l_s = _attn_sample(proj_s3, caches)
    outs_s = [o_s[:, 0, g * gw:(g + 1) * gw].reshape(1, 1, bd, gw) for g in range(N_GROUPS)]
    lses_s = [l_s[:, 0, g * gw:(g + 1) * gw].reshape(1, 1, bd, gw) for g in range(N_GROUPS)]
    x1_s, h2_s, gates_s = _merge(xs, u_s.reshape(bd, d), proj_s, outs_s, lses_s, wbg, wba, wo, g_ffn,
                                 w_r, w_r, b_r, n_experts, bd, tm=bd, precise=True)

    y_p = _moe_dense(h2_p, gates_p, x1_p, wgu_bf, bgu, wd_bf, bdn, g_fin, tm=1024)
    y_s = _moe_dense(h2_s, gates_s, x1_s, wgu_bf, bgu, wd_bf, bdn, g_fin, tm=bd)

    kv_p = [_kv_rows(srcs[g], (OFF_QKV0 if g == 0 else 0) + gw, min(ATTN_GROUPS[g][0], seq))[None]
            for g in range(N_GROUPS)]
    kv_s = []
    for g in range(N_GROUPS):
        new = proj_s[:, OFF_QKV[g] + gw:OFF_QKV[g] + 3 * gw].reshape(bd, 1, 2, ATTN_HEADS, ATTN_HEAD_DIM)
        kv_s.append(jnp.concatenate([caches[g][:, 1:], new.astype(caches[g].dtype)], axis=1)[None])
    return (y_p.reshape(batch, seq, d), y_s.reshape(bd, 1, d),
            jnp.swapaxes(st_p, -1, -2)[None], kv_p[0], kv_p[1], kv_p[2],
            st_s[None], kv_s[0], kv_s[1], kv_s[2])
```

```python
import functools

import jax
import jax.numpy as jnp
from jax import lax
from jax.experimental import pallas as pl
from jax.experimental.pallas import tpu as pltpu

F32 = jnp.float32
BF16 = jnp.bfloat16
HIGHEST = lax.Precision.HIGHEST

D_MODEL = 1024
NORM_EPS = 1e-5
GLA_HEADS = 4
GLA_HEAD_K = 128
GLA_HEAD_V = 256
GLA_GATE_RANK = 16
GLA_GATE_NORMALIZER = 16.0
GLA_CHUNK = 64
ATTN_GROUPS = ((128, 1), (512, 4), (2048, 16))
N_GROUPS = 3
ATTN_HEADS = 4
ATTN_HEAD_DIM = 128
ATTN_GROUP_DIM = ATTN_HEADS * ATTN_HEAD_DIM
TOP_K = 4
SWIGLU_LIMIT = 7.0
SWIGLU_ALPHA = 1.702
NEG_BIG = -1e30
LANES = 128
ROW_ALIGN = 8
MOE_TOKEN_TILE = 256
MOE_EXPERT_TILE = 512

OFF_QG, OFF_KG, OFF_VG, OFF_RG = 0, 512, 1024, 2048
OFF_GATE_GLA, OFF_GATE_ATTN = 3072, 4096
OFF_QKV0 = 5120
OFF_GK = 6656
NAT_W = 7168
QKV_W = 3 * ATTN_GROUP_DIM
PROJ_W = NAT_W + 2 * QKV_W
OFF_QKV = (OFF_QKV0, NAT_W, NAT_W + QKV_W)
PROJ_SPLITS = (512, 512, 1024, 1024, GLA_GATE_RANK, 1536, 1536, 1536, 1024, 1024)

VMEM_LIMIT = 48 * 1024 * 1024


def _cparams(*sem):
    return pltpu.CompilerParams(dimension_semantics=sem, vmem_limit_bytes=VMEM_LIMIT)


def _alibi_slope(group, head):
    n = N_GROUPS * ATTN_HEADS
    return 2.0 ** (-8.0 * (group * ATTN_HEADS + head + 1) / n)


def _log_sigmoid(x):
    return jnp.minimum(x, 0.0) - jnp.log1p(jnp.exp(-jnp.abs(x)))


def _dot_nt(a, b):
    return lax.dot_general(a, b, (((1,), (1,)), ((), ())), preferred_element_type=F32)


def _dot(a, b):
    return jnp.dot(a, b, preferred_element_type=F32)


def _dot_hi(a, b):
    return jnp.dot(a, b, preferred_element_type=F32, precision=HIGHEST)


def _split3(x):
    hi = x.astype(BF16)
    r1 = x - hi.astype(F32)
    mid = r1.astype(BF16)
    lo = (r1 - mid.astype(F32)).astype(BF16)
    return hi, mid, lo


def _rmsnorm(x, g):
    return x * lax.rsqrt(jnp.mean(x * x, axis=-1, keepdims=True) + NORM_EPS) * g


def _proj_kernel(x_ref, g_ref, w_ref, o_ref, h_ref, *scr, dilation):
    @pl.when(pl.program_id(1) == 0)
    def _():
        h_ref[...] = _rmsnorm(x_ref[...], g_ref[...]).astype(BF16)

    acc = _dot(h_ref[...], w_ref[...])
    if dilation == 1:
        o_ref[0, 0] = acc.astype(o_ref.dtype)
    else:
        n = acc.shape[0] // dilation
        for c in range(acc.shape[1] // LANES):
            cols = slice(c * LANES, (c + 1) * LANES)
            scr[0][c] = acc[:, cols]
            for r in range(dilation):
                o_ref[0, r, :, cols] = scr[0][c, pl.ds(r, n, stride=dilation), :].astype(o_ref.dtype)


def _norm_proj(x2d, g_row, w_bf, batch, seq, dilation, tm, tn):
    m, d = x2d.shape
    n = w_bf.shape[1]
    tps = seq // tm
    scratch = [pltpu.VMEM((tm, d), BF16)]
    if dilation > 1:
        scratch.append(pltpu.VMEM((tn // LANES, tm, LANES), F32))
    return pl.pallas_call(
        functools.partial(_proj_kernel, dilation=dilation),
        grid=(m // tm, n // tn),
        in_specs=[pl.BlockSpec((tm, d), lambda i, j: (i, 0)),
                  pl.BlockSpec((1, d), lambda i, j: (0, 0)),
                  pl.BlockSpec((d, tn), lambda i, j: (0, j))],
        out_specs=pl.BlockSpec((1, dilation, tm // dilation, tn), lambda i, j: (i // tps, 0, i % tps, j)),
        out_shape=jax.ShapeDtypeStruct((batch, dilation, seq // dilation, n), BF16),
        scratch_shapes=scratch,
        compiler_params=_cparams("parallel", "arbitrary"),
        name=f"norm_proj_d{dilation}",
    )(x2d, g_row, w_bf)


def _proj_precise_kernel(x_ref, g_ref, w_ref, o_ref):
    o_ref[...] = _dot_hi(_rmsnorm(x_ref[...], g_ref[...]), w_ref[...])


def _norm_proj_precise(x2d, g_row, w_f32, tn):
    m, d = x2d.shape
    n = w_f32.shape[1]
    return pl.pallas_call(
        _proj_precise_kernel,
        grid=(n // tn,),
        in_specs=[pl.BlockSpec((m, d), lambda j: (0, 0)),
                  pl.BlockSpec((1, d), lambda j: (0, 0)),
                  pl.BlockSpec((d, tn), lambda j: (0, j))],
        out_specs=pl.BlockSpec((m, tn), lambda j: (0, j)),
        out_shape=jax.ShapeDtypeStruct((m, n), F32),
        compiler_params=_cparams("parallel"),
        name="norm_proj_sample",
    )(x2d, g_row, w_f32)


def _gla_kernel(q_ref, k_ref, v_ref, r_ref, gk_ref, wup_ref, bup_ref, gn_ref,
                u_ref, st_ref, s_scr, o_scr, *, n_chunks):
    t = pl.program_id(2)

    @pl.when(t == 0)
    def _():
        s_scr[...] = jnp.zeros_like(s_scr)

    gk = _dot(gk_ref[...], wup_ref[...]) + bup_ref[...]
    log_a = _log_sigmoid(gk) * (1.0 / GLA_GATE_NORMALIZER)
    c_sz = GLA_CHUNK
    row = lax.broadcasted_iota(jnp.int32, (c_sz, c_sz), 0)
    col = lax.broadcasted_iota(jnp.int32, (c_sz, c_sz), 1)
    tri = row >= col
    tri_bf = jnp.where(tri, 1.0, 0.0).astype(BF16)
    scale = GLA_HEAD_K ** -0.5

    st = s_scr[...]
    for c in range(n_chunks):
        sl = slice(c * c_sz, (c + 1) * c_sz)
        g_hi, g_mid, g_lo = _split3(log_a[sl])
        b = _dot(tri_bf, g_hi) + _dot(tri_bf, g_mid) + _dot(tri_bf, g_lo)
        b_last = b[c_sz - 1:c_sz, :]
        q = q_ref[sl, :].astype(F32) * scale
        k = k_ref[sl, :].astype(F32)
        v_bf = v_ref[sl, :]
        q_in = (q * jnp.exp(b)).astype(BF16)
        k_in = (k * jnp.exp(-b)).astype(BF16)
        k_out = (k * jnp.exp(b_last - b)).astype(BF16)
        a = jnp.where(tri, _dot_nt(q_in, k_in), 0.0)
        o = _dot_nt(q_in, st.astype(BF16)) + _dot(a.astype(BF16), v_bf)
        st = st * jnp.exp(b_last) + pl.dot(v_bf, k_out, trans_a=True)
        o_scr[sl, :] = o
    s_scr[...] = st

    r = r_ref[...].astype(F32)
    u = _rmsnorm(o_scr[...], gn_ref[...]) * (r * jax.nn.sigmoid(r))
    u_ref[...] = u.astype(u_ref.dtype)

    @pl.when(t == pl.num_programs(2) - 1)
    def _():
        st_ref[0, 0] = st


def _gla_prompt(proj, w_up_bf, b_up, g_norm, batch, seq, tc):
    nt = seq // tc
    kb, vb = GLA_HEAD_K, GLA_HEAD_V
    rows = lambda b, h, t: b * nt + t
    kern = functools.partial(_gla_kernel, n_chunks=tc // GLA_CHUNK)
    return pl.pallas_call(
        kern,
        grid=(batch, GLA_HEADS, nt),
        in_specs=[
            pl.BlockSpec((tc, kb), lambda b, h, t: (rows(b, h, t), OFF_QG // kb + h)),
            pl.BlockSpec((tc, kb), lambda b, h, t: (rows(b, h, t), OFF_KG // kb + h)),
            pl.BlockSpec((tc, vb), lambda b, h, t: (rows(b, h, t), OFF_VG // vb + h)),
            pl.BlockSpec((tc, vb), lambda b, h, t: (rows(b, h, t), OFF_RG // vb + h)),
            pl.BlockSpec((tc, LANES), lambda b, h, t: (rows(b, h, t), OFF_GK // LANES)),
            pl.BlockSpec((LANES, kb), lambda b, h, t: (0, h)),
            pl.BlockSpec((1, kb), lambda b, h, t: (0, h)),
            pl.BlockSpec((1, vb), lambda b, h, t: (0, 0)),
        ],
        out_specs=[
            pl.BlockSpec((tc, vb), lambda b, h, t: (rows(b, h, t), h)),
            pl.BlockSpec((1, 1, vb, kb), lambda b, h, t: (b, h, 0, 0)),
        ],
        out_shape=[jax.ShapeDtypeStruct((batch * seq, GLA_HEADS * vb), BF16),
                   jax.ShapeDtypeStruct((batch, GLA_HEADS, vb, kb), F32)],
        scratch_shapes=[pltpu.VMEM((vb, kb), F32), pltpu.VMEM((tc, vb), F32)],
        compiler_params=_cparams("parallel", "parallel", "arbitrary"),
        name="gla_prompt",
    )(proj, proj, proj, proj, proj, w_up_bf, b_up, g_norm)


def _attn_kernel(q_ref, kc_ref, kp_ref, vc_ref, vp_ref, o_ref, l_ref, *, group, dilation, blk, n_qb):
    n = pl.program_id(2)
    e = ATTN_HEAD_DIM
    i_idx = lax.broadcasted_iota(jnp.int32, (blk, 2 * blk), 0)
    j_idx = lax.broadcasted_iota(jnp.int32, (blk, 2 * blk), 1)
    rel = i_idx + blk - j_idx
    band = (rel >= 0) & (rel <= blk)
    dist = (dilation * rel).astype(F32)
    for qb in range(n_qb):
        rows = slice(qb * blk, (qb + 1) * blk)
        q = q_ref[0, 0, rows, :]
        if qb == 0:
            k_prev, v_prev = kp_ref[0, 0], vp_ref[0, 0]
            valid = band & ((j_idx >= blk) | (n > 0))
        else:
            prev = slice((qb - 1) * blk, qb * blk)
            k_prev, v_prev = kc_ref[0, 0, prev, :], vc_ref[0, 0, prev, :]
            valid = band
        kk = jnp.concatenate([k_prev, kc_ref[0, 0, rows, :]], axis=0)
        vv = jnp.concatenate([v_prev, vc_ref[0, 0, rows, :]], axis=0)
        for h in range(ATTN_HEADS):
            cols = slice(h * e, (h + 1) * e)
            s = _dot_nt(q[:, cols], kk[:, cols]) * (e ** -0.5) - _alibi_slope(group, h) * dist
            s = jnp.where(valid, s, NEG_BIG)
            m = jnp.max(s, axis=-1, keepdims=True)
            p = jnp.exp(s - m)
            den = jnp.sum(p, axis=-1, keepdims=True)
            o = _dot(p.astype(BF16), vv[:, cols]) / den
            o_ref[0, 0, rows, cols] = o.astype(o_ref.dtype)
            l_ref[0, 0, rows, cols] = jnp.broadcast_to(m + jnp.log(den), (blk, e))


def _attn_prompt(src, col0, group, tq):
    batch, dilation, n_sub, _ = src.shape
    window, dil = ATTN_GROUPS[group]
    assert dil == dilation
    blk = window // dilation
    n_qb = tq // blk
    gw = ATTN_GROUP_DIM
    qc, kc, vc = col0 // gw, col0 // gw + 1, col0 // gw + 2
    cur = lambda c: pl.BlockSpec((1, 1, tq, gw), lambda b, r, n: (b, r, n, c))
    prev = lambda c: pl.BlockSpec((1, 1, blk, gw), lambda b, r, n: (b, r, jnp.maximum(n * n_qb - 1, 0), c))
    kern = functools.partial(_attn_kernel, group=group, dilation=dilation, blk=blk, n_qb=n_qb)
    out_spec = pl.BlockSpec((1, 1, tq, gw), lambda b, r, n: (b, r, n, 0))
    return pl.pallas_call(
        kern,
        grid=(batch, dilation, n_sub // tq),
        in_specs=[cur(qc), cur(kc), prev(kc), cur(vc), prev(vc)],
        out_specs=[out_spec, out_spec],
        out_shape=[jax.ShapeDtypeStruct((batch, dilation, n_sub, gw), BF16),
                   jax.ShapeDtypeStruct((batch, dilation, n_sub, gw), F32)],
        compiler_params=_cparams("parallel", "parallel", "arbitrary"),
        name=f"attn_prompt_g{group}",
    )(src, src, src, src, src)


def _merge_kernel(x_ref, u_ref, gg_ref, ga_ref, o1_ref, o2_ref, o3_ref, l1_ref, l2_ref, l3_ref,
                  wbg_ref, wba_ref, wo_ref, gf_ref, wr_hi_ref, wr_lo_ref, br_ref,
                  x1_ref, h2_ref, route_ref, cpad_ref, *scr, n_experts, dilations, precise, ts):
    tm = x_ref.shape[0]
    scr = list(scr)

    def natural(ref, dilation):
        if dilation == 1:
            return ref[0, 0].astype(F32)
        buf = scr.pop(0)
        n = tm // dilation
        for c in range(buf.shape[0]):
            for r in range(dilation):
                buf[c, pl.ds(r, n, stride=dilation), :] = ref[0, r, :, c * LANES:(c + 1) * LANES].astype(F32)
        return jnp.concatenate([buf[c] for c in range(buf.shape[0])], axis=1)

    o1, o2, o3 = (natural(r, d) for r, d in zip((o1_ref, o2_ref, o3_ref), dilations))
    l1, l2, l3 = (natural(r, d) for r, d in zip((l1_ref, l2_ref, l3_ref), dilations))
    mm = _dot_hi if precise else (lambda a, b: _dot(a.astype(BF16), b))
    lm = jnp.maximum(jnp.maximum(l1, l2), l3)
    e1, e2, e3 = jnp.exp(l1 - lm), jnp.exp(l2 - lm), jnp.exp(l3 - lm)
    o_att = (e1 * o1 + e2 * o2 + e3 * o3) / (e1 + e2 + e3)
    merged = (jax.nn.sigmoid(gg_ref[...].astype(F32)) * mm(u_ref[...], wbg_ref[...])
              + jax.nn.sigmoid(ga_ref[...].astype(F32)) * mm(o_att, wba_ref[...]))
    x1 = x_ref[...] + mm(merged, wo_ref[...])
    x1_ref[...] = x1
    h2 = _rmsnorm(x1, gf_ref[...])
    h2_ref[...] = h2.astype(BF16)

    if precise:
        logits = _dot_hi(h2, wr_hi_ref[...]) + br_ref[...]
    else:
        h_hi, h_mid, _ = _split3(h2)
        logits = (_dot(h_hi, wr_hi_ref[...]) + _dot(h_mid, wr_hi_ref[...]) + _dot(h_hi, wr_lo_ref[...])) + br_ref[...]
    lane = lax.broadcasted_iota(jnp.int32, (ts, LANES), 1).astype(F32)
    ri = lax.broadcasted_iota(jnp.int32, (ts, ts), 0)
    ci = lax.broadcasted_iota(jnp.int32, (ts, ts), 1)
    earlier = jnp.where(ci < ri, 1.0, 0.0).astype(BF16)
    li = lax.broadcasted_iota(jnp.int32, (LANES, LANES), 0)
    lj = lax.broadcasted_iota(jnp.int32, (LANES, LANES), 1)
    before = jnp.where(li < lj, 1.0, 0.0).astype(BF16)
    for s in range(tm // ts):
        rows = slice(s * ts, (s + 1) * ts)
        cur = jnp.where(lane < n_experts, logits[rows, :], -jnp.inf)
        tops, sels = [], []
        for _ in range(TOP_K):
            m = jnp.max(cur, axis=-1, keepdims=True)
            idx = jnp.min(jnp.where(cur == m, lane, float(LANES)), axis=-1, keepdims=True)
            sel = lane == idx
            tops.append(m)
            sels.append(sel)
            cur = jnp.where(sel, -jnp.inf, cur)
        ex = [jnp.exp(m - tops[0]) for m in tops]
        den = ex[0] + ex[1] + ex[2] + ex[3]
        mem = jnp.zeros((ts, LANES), F32)
        for sel in sels:
            mem = jnp.where(sel, 1.0, mem)
        rank = _dot(earlier, mem.astype(BF16))
        count = jnp.sum(mem, axis=0, keepdims=True)
        cpad = jnp.ceil(count * (1.0 / ROW_ALIGN)) * ROW_ALIGN
        cpad8 = jnp.broadcast_to(cpad, (8, LANES))
        seg_start = _dot(cpad8.astype(BF16), before)[0:1]
        pos = seg_start + rank
        route = jnp.zeros((ts, LANES), F32)
        for k in range(TOP_K):
            dest = jnp.sum(jnp.where(sels[k], pos, 0.0), axis=-1, keepdims=True)
            route = jnp.where(lane == float(k), dest, route)
            route = jnp.where(lane == float(TOP_K + k), ex[k] / den, route)
        route_ref[rows, :] = route
        cpad_ref[s * 8:(s + 1) * 8, :] = cpad8


def _merge(x2d, u, gate_src, outs, lses, wbg, wba, wo, g_ffn, wr_hi, wr_lo, b_r, n_experts, seq, tm, ts, precise):
    m, d = x2d.shape
    gw = ATTN_GROUP_DIM
    tps = seq // tm
    dilations = tuple(o.shape[1] for o in outs)
    row = lambda w: pl.BlockSpec((tm, w), lambda i: (i, 0))
    full = lambda a: pl.BlockSpec(a.shape, lambda i: (0, 0))
    grp = lambda dil: pl.BlockSpec((1, dil, tm // dil, gw), lambda i: (i // tps, 0, i % tps, 0))
    kern = functools.partial(_merge_kernel, n_experts=n_experts, dilations=dilations, precise=precise, ts=ts)
    n_scr = 2 * sum(1 for dil in dilations if dil > 1)
    return pl.pallas_call(
        kern,
        grid=(m // tm,),
        in_specs=[row(d), row(d),
                  pl.BlockSpec((tm, d), lambda i: (i, OFF_GATE_GLA // d)),
                  pl.BlockSpec((tm, d), lambda i: (i, OFF_GATE_ATTN // d)),
                  *[grp(dil) for dil in dilations], *[grp(dil) for dil in dilations],
                  full(wbg), full(wba), full(wo), full(g_ffn), full(wr_hi), full(wr_lo), full(b_r)],
        out_specs=[row(d), row(d), row(LANES), pl.BlockSpec((tm // ts * 8, LANES), lambda i: (i, 0))],
        out_shape=[jax.ShapeDtypeStruct((m, d), F32), jax.ShapeDtypeStruct((m, d), BF16),
                   jax.ShapeDtypeStruct((m, LANES), F32), jax.ShapeDtypeStruct((m // ts * 8, LANES), F32)],
        scratch_shapes=[pltpu.VMEM((gw // LANES, tm, LANES), F32)] * n_scr,
        compiler_params=_cparams("parallel"),
        name="merge_router_sample" if precise else "merge_router",
    )(x2d, u, gate_src, gate_src, *outs, *lses, wbg, wba, wo, g_ffn, wr_hi, wr_lo, b_r)


def _round_up(x, mult):
    return (x + mult - 1) // mult * mult


def _tile_rows(n_tokens, n_experts):
    return _round_up(n_tokens * TOP_K + n_experts * (ROW_ALIGN - 1), ROW_ALIGN)


def _moe_plan(cpad, nc, n_experts, tm_e, n_tiles_max):
    tot = jnp.sum(cpad, axis=0)
    gsize = (tot + tm_e - 1) // tm_e * tm_e
    g_end = jnp.cumsum(gsize)
    g_start = g_end - gsize
    n_used = (g_end[-1] // tm_e).astype(jnp.int32)
    o_start = jnp.cumsum(cpad, axis=0) - cpad
    seg_end = jnp.cumsum(cpad, axis=1)
    seg_start = seg_end - cpad

    row0 = jnp.arange(nc, dtype=jnp.int32) * ROW_ALIGN
    e_of = jnp.sum((seg_end[:, None, :] <= row0[None, :, None]).astype(jnp.int32), axis=-1)
    e_of = jnp.minimum(e_of, n_experts - 1)
    take = lambda a: jnp.take_along_axis(a, e_of, axis=1)
    dst = g_start[e_of] + take(o_start) + row0[None, :] - take(seg_start)
    nch = (seg_end[:, -1] // ROW_ALIGN).astype(jnp.int32)
    dst = jnp.where(jnp.arange(nc)[None, :] < nch[:, None], dst, 0).astype(jnp.int32)

    t0 = jnp.arange(n_tiles_max, dtype=jnp.int32) * tm_e
    tile_expert = jnp.minimum(jnp.sum((g_end[None, :] <= t0[:, None]).astype(jnp.int32), axis=-1), n_experts - 1)
    last = tile_expert[jnp.maximum(n_used - 1, 0)]
    tile_expert = jnp.where(jnp.arange(n_tiles_max) < n_used, tile_expert, last).astype(jnp.int32)
    gap_start = (g_start + tot).astype(jnp.int32)
    gap_chunks = ((gsize - tot) // ROW_ALIGN).astype(jnp.int32)
    return dst.reshape(-1), nch, tile_expert, n_used.reshape(1), gap_start, gap_chunks


def _loop(n, fn):
    lax.fori_loop(0, n, lambda i, carry: (fn(i), carry)[1], 0)


def _compact_kernel(dst_ref, nch_ref, gap_start_ref, gap_chunks_ref, nu_ref, hp_ref, rp_ref, hs_ref, rs_ref,
                    xs_ref, xc_scr, zero_scr, sem, *, nc):
    s = pl.program_id(0)
    last = pl.num_programs(0) - 1
    is_sample = s == last
    ts, r_pad = hp_ref.shape[0], xc_scr.shape[0]
    h = jnp.where(is_sample, hs_ref[...], hp_ref[...])
    route = jnp.where(is_sample, rs_ref[...], rp_ref[...])
    dest_t = jnp.transpose(route)
    row_id = lax.broadcasted_iota(jnp.int32, (r_pad, ts), 0).astype(F32)
    perm = jnp.zeros((r_pad, ts), F32)
    for k in range(TOP_K):
        perm = jnp.where(row_id == dest_t[k:k + 1, :], 1.0, perm)
    xc_scr[...] = _dot(perm.astype(BF16), h)

    def chunk(c):
        src = xc_scr.at[pl.ds(pl.multiple_of(c * ROW_ALIGN, ROW_ALIGN), ROW_ALIGN)]
        dst = xs_ref.at[pl.ds(pl.multiple_of(dst_ref[s * nc + c], ROW_ALIGN), ROW_ALIGN)]
        return pltpu.make_async_copy(src, dst, sem)

    _loop(nch_ref[s], lambda c: chunk(c).start())
    _loop(nch_ref[s], lambda c: chunk(c).wait())

    @pl.when(is_sample)
    def _():
        zero_scr[...] = jnp.zeros_like(zero_scr)
        tile = zero_scr.shape[0]

        def gap(e, j):
            row = pl.multiple_of(gap_start_ref[e] + j * ROW_ALIGN, ROW_ALIGN)
            return pltpu.make_async_copy(zero_scr.at[pl.ds(0, ROW_ALIGN)], xs_ref.at[pl.ds(row, ROW_ALIGN)], sem)

        def tail(t):
            row = pl.multiple_of(t * tile, tile)
            return pltpu.make_async_copy(zero_scr, xs_ref.at[pl.ds(row, tile)], sem)

        n_exp = gap_start_ref.shape[0]
        n_tail = xs_ref.shape[0] // tile - nu_ref[0]
        _loop(n_exp, lambda e: _loop(gap_chunks_ref[e], lambda j: gap(e, j).start()))
        _loop(n_tail, lambda t: tail(nu_ref[0] + t).start())
        _loop(n_exp, lambda e: _loop(gap_chunks_ref[e], lambda j: gap(e, j).wait()))
        _loop(n_tail, lambda t: tail(nu_ref[0] + t).wait())


def _compact(h2_p, route_p, h2_s, route_s, plan, ts, r_pad, nc, n_tiles_e, tm_e):
    m, d = h2_p.shape
    n_p = m // ts
    dst, nch, _, n_used, gap_start, gap_chunks = plan
    prompt = lambda w: pl.BlockSpec((ts, w), lambda i, *_: (jnp.minimum(i, n_p - 1), 0))
    sample = lambda w: pl.BlockSpec((ts, w), lambda i, *_: (0, 0))
    return pl.pallas_call(
        functools.partial(_compact_kernel, nc=nc),
        grid_spec=pltpu.PrefetchScalarGridSpec(
            num_scalar_prefetch=5,
            grid=(n_p + 1,),
            in_specs=[prompt(d), prompt(LANES), sample(d), sample(LANES)],
            out_specs=pl.BlockSpec(memory_space=pl.ANY),
            scratch_shapes=[pltpu.VMEM((r_pad, d), F32), pltpu.VMEM((tm_e, d), F32), pltpu.SemaphoreType.DMA(())]),
        out_shape=jax.ShapeDtypeStruct((n_tiles_e * tm_e, d), F32),
        compiler_params=_cparams("arbitrary"),
        name="moe_compact",
    )(dst, nch, gap_start, gap_chunks, n_used, h2_p, route_p, h2_s, route_s)


def _expert_kernel(te_ref, nu_ref, x_ref, wgu_ref, bgu_ref, wd_ref, bd_ref, o_ref, wgu_bf, wd_bf, *, d_ff, sub):
    t = pl.program_id(0)
    used = t < nu_ref[0]

    @pl.when(used)
    def _():
        @pl.when((t == 0) | (te_ref[t] != te_ref[jnp.maximum(t - 1, 0)]))
        def _():
            step = 128
            for i in range(wgu_bf.shape[0] // step):
                rows = slice(i * step, (i + 1) * step)
                wgu_bf[rows, :] = wgu_ref[0, rows, :].astype(BF16)
            for i in range(wd_bf.shape[0] // step):
                rows = slice(i * step, (i + 1) * step)
                wd_bf[rows, :] = wd_ref[0, rows, :].astype(BF16)

        for s in range(x_ref.shape[0] // sub):
            rows = slice(s * sub, (s + 1) * sub)
            gu = _dot(x_ref[rows, :].astype(BF16), wgu_bf[...]) + bgu_ref[0]
            gate = jnp.minimum(gu[:, :d_ff], SWIGLU_LIMIT)
            up = jnp.clip(gu[:, d_ff:], -SWIGLU_LIMIT, SWIGLU_LIMIT)
            act = (up + 1.0) * gate * jax.nn.sigmoid(SWIGLU_ALPHA * gate)
            o_ref[rows, :] = _dot(act.astype(BF16), wd_bf[...]) + bd_ref[0]

    @pl.when(jnp.logical_not(used))
    def _():
        o_ref[...] = jnp.zeros_like(o_ref)


def _experts(xs, tile_expert, n_used, wgu, bgu, wd, bd, tm_e):
    n_rows, d = xs.shape
    _, _, two_ff = wgu.shape
    d_ff = two_ff // 2
    kern = functools.partial(_expert_kernel, d_ff=d_ff, sub=256)
    by_expert = lambda shape: pl.BlockSpec(shape, lambda t, te, nu: (te[t], 0, 0))
    return pl.pallas_call(
        kern,
        grid_spec=pltpu.PrefetchScalarGridSpec(
            num_scalar_prefetch=2,
            grid=(n_rows // tm_e,),
            in_specs=[pl.BlockSpec((tm_e, d), lambda t, te, nu: (jnp.minimum(t, nu[0] - 1), 0)),
                      by_expert((1, d, two_ff)), by_expert((1, 1, two_ff)),
                      by_expert((1, d_ff, d)), by_expert((1, 1, d))],
            out_specs=pl.BlockSpec((tm_e, d), lambda t, te, nu: (t, 0)),
            scratch_shapes=[pltpu.VMEM((d, two_ff), BF16), pltpu.VMEM((d_ff, d), BF16)]),
        out_shape=jax.ShapeDtypeStruct((n_rows, d), F32),
        compiler_params=pltpu.CompilerParams(dimension_semantics=("arbitrary",), vmem_limit_bytes=56 * 1024 * 1024),
        name="moe_experts",
    )(tile_expert, n_used, xs, wgu, bgu, wd, bd)


def _combine_kernel(dst_ref, nch_ref, route_ref, x1_ref, gfin_ref, ys_ref, y_ref, yc_scr, sem, *, nc, tile0):
    @pl.when(pl.program_id(0) == 0)
    def _():
        yc_scr[...] = jnp.zeros_like(yc_scr)

    s = pl.program_id(0) + tile0

    def chunk(c):
        src = ys_ref.at[pl.ds(pl.multiple_of(dst_ref[s * nc + c], ROW_ALIGN), ROW_ALIGN)]
        dst = yc_scr.at[pl.ds(pl.multiple_of(c * ROW_ALIGN, ROW_ALIGN), ROW_ALIGN)]
        return pltpu.make_async_copy(src, dst, sem)

    _loop(nch_ref[s], lambda c: chunk(c).start())
    _loop(nch_ref[s], lambda c: chunk(c).wait())

    ts, r_pad = x1_ref.shape[0], yc_scr.shape[0]
    route = route_ref[...]
    col_id = lax.broadcasted_iota(jnp.int32, (ts, r_pad), 1).astype(F32)
    weights = jnp.zeros((ts, r_pad), F32)
    for k in range(TOP_K):
        weights = jnp.where(col_id == route[:, k:k + 1], route[:, TOP_K + k:TOP_K + k + 1], weights)
    moe = _dot(weights.astype(BF16), yc_scr[...].astype(BF16))
    y_ref[...] = _rmsnorm(x1_ref[...] + moe, gfin_ref[...])


def _combine(route, x1, g_final, ys, plan, tile0, ts, r_pad, nc):
    m, d = x1.shape
    dst, nch = plan[0], plan[1]
    return pl.pallas_call(
        functools.partial(_combine_kernel, nc=nc, tile0=tile0),
        grid_spec=pltpu.PrefetchScalarGridSpec(
            num_scalar_prefetch=2,
            grid=(m // ts,),
            in_specs=[pl.BlockSpec((ts, LANES), lambda i, *_: (i, 0)),
                      pl.BlockSpec((ts, d), lambda i, *_: (i, 0)),
                      pl.BlockSpec((1, d), lambda i, *_: (0, 0)),
                      pl.BlockSpec(memory_space=pl.ANY)],
            out_specs=pl.BlockSpec((ts, d), lambda i, *_: (i, 0)),
            scratch_shapes=[pltpu.VMEM((r_pad, d), F32), pltpu.SemaphoreType.DMA(())]),
        out_shape=jax.ShapeDtypeStruct((m, d), F32),
        compiler_params=_cparams("arbitrary"),
        name="moe_combine",
    )(dst, nch, route, x1, g_final, ys)


def _column(x_row):
    return jnp.transpose(jnp.broadcast_to(x_row, (LANES, LANES)))


def _gla_step_kernel(p_ref, s_ref, wup_ref, bup_ref, gn_ref, u_ref, so_ref):
    row = p_ref[0]
    gk8 = jnp.broadcast_to(row[:, OFF_GK:OFF_GK + LANES], (8, LANES))
    gk = _dot_hi(gk8, wup_ref[...]) + bup_ref[...]
    decay = jnp.exp(_log_sigmoid(gk[0:1, :]) * (1.0 / GLA_GATE_NORMALIZER))
    kd, vd = GLA_HEAD_K, GLA_HEAD_V
    wide = lambda c: jnp.concatenate([c, c], axis=1)
    for h in range(GLA_HEADS):
        q = row[:, OFF_QG + h * kd:OFF_QG + (h + 1) * kd] * (kd ** -0.5)
        k = row[:, OFF_KG + h * kd:OFF_KG + (h + 1) * kd]
        v = row[:, OFF_VG + h * vd:OFF_VG + (h + 1) * vd]
        r = row[:, OFF_RG + h * vd:OFF_RG + (h + 1) * vd]
        s_new = s_ref[0, h] * wide(_column(decay[:, h * kd:(h + 1) * kd])) + wide(_column(k)) * v
        so_ref[0, h] = s_new
        o = jnp.sum(wide(_column(q)) * s_new, axis=0, keepdims=True)
        u_ref[0, :, h * vd:(h + 1) * vd] = _rmsnorm(o, gn_ref[...]) * (r * jax.nn.sigmoid(r))


def _gla_sample(proj_s3, state, w_up, b_up, g_norm):
    bd = proj_s3.shape[0]
    full = lambda a: pl.BlockSpec(a.shape, lambda b: (0,) * a.ndim)
    st_spec = pl.BlockSpec((1, GLA_HEADS, GLA_HEAD_K, GLA_HEAD_V), lambda b: (b, 0, 0, 0))
    return pl.pallas_call(
        _gla_step_kernel,
        grid=(bd,),
        in_specs=[pl.BlockSpec((1, 1, PROJ_W), lambda b: (b, 0, 0)), st_spec,
                  full(w_up), full(b_up), full(g_norm)],
        out_specs=[pl.BlockSpec((1, 1, GLA_HEADS * GLA_HEAD_V), lambda b: (b, 0, 0)), st_spec],
        out_shape=[jax.ShapeDtypeStruct((bd, 1, GLA_HEADS * GLA_HEAD_V), F32),
                   jax.ShapeDtypeStruct(state.shape, state.dtype)],
        compiler_params=_cparams("parallel"),
        name="gla_sample",
    )(proj_s3, state, w_up, b_up, g_norm)


def _attn_step_kernel(p_ref, c1_ref, c2_ref, c3_ref, o_ref, l_ref):
    row = p_ref[0]
    gw, e = ATTN_GROUP_DIM, ATTN_HEAD_DIM
    for g, c_ref in enumerate((c1_ref, c2_ref, c3_ref)):
        window, dilation = ATTN_GROUPS[g]
        blk = window // dilation
        back = (dilation * (blk - lax.broadcasted_iota(jnp.int32, (blk, 1), 0))).astype(F32)
        for h in range(ATTN_HEADS):
            lo = OFF_QKV[g] + h * e
            q = row[:, lo:lo + e]
            k_new = row[:, lo + gw:lo + gw + e]
            v_new = row[:, lo + 2 * gw:lo + 2 * gw + e]
            k_c = c_ref[0, :, 0, h, :]
            v_c = c_ref[0, :, 1, h, :]
            s = jnp.sum(k_c * q, axis=-1, keepdims=True) * (e ** -0.5) - _alibi_slope(g, h) * back
            s0 = jnp.sum(k_new * q, axis=-1, keepdims=True) * (e ** -0.5)
            m = jnp.maximum(jnp.max(s, axis=0, keepdims=True), s0)
            p = jnp.exp(s - m)
            p0 = jnp.exp(s0 - m)
            den = jnp.sum(p, axis=0, keepdims=True) + p0
            o = (jnp.sum(p * v_c, axis=0, keepdims=True) + p0 * v_new) / den
            cols = slice(g * gw + h * e, g * gw + (h + 1) * e)
            o_ref[0, :, cols] = o
            l_ref[0, :, cols] = jnp.broadcast_to(m + jnp.log(den), (1, e))


def _attn_sample(proj_s3, caches):
    bd = proj_s3.shape[0]
    gw = ATTN_GROUP_DIM
    views, specs = [], []
    for g, cache in enumerate(caches):
        window, dilation = ATTN_GROUPS[g]
        n_buf = cache.shape[1]
        assert n_buf == window, "cache must hold exactly one window of rows"
        blk = n_buf // dilation
        views.append(cache.reshape(bd, blk, dilation, 2, ATTN_HEADS, ATTN_HEAD_DIM))
        specs.append(pl.BlockSpec((1, blk, None, 2, ATTN_HEADS, ATTN_HEAD_DIM), lambda b: (b, 0, 0, 0, 0, 0)))
    out_spec = pl.BlockSpec((1, 1, N_GROUPS * gw), lambda b: (b, 0, 0))
    return pl.pallas_call(
        _attn_step_kernel,
        grid=(bd,),
        in_specs=[pl.BlockSpec((1, 1, PROJ_W), lambda b: (b, 0, 0))] + specs,
        out_specs=[out_spec, out_spec],
        out_shape=[jax.ShapeDtypeStruct((bd, 1, N_GROUPS * gw), F32),
                   jax.ShapeDtypeStruct((bd, 1, N_GROUPS * gw), F32)],
        compiler_params=_cparams("parallel"),
        name="attn_sample",
    )(proj_s3, *views)


def _prep_w_in(w_in):
    offs, acc = [], 0
    for s in PROJ_SPLITS[:-1]:
        acc += s
        offs.append(acc)
    q_g, k_g, v_g, r_g, gk, q_a, k_a, v_a, gg, ga = jnp.split(w_in, offs, axis=1)
    gw = ATTN_GROUP_DIM
    qkv = lambda g: [a[:, g * gw:(g + 1) * gw] for a in (q_a, k_a, v_a)]
    pad = jnp.zeros((w_in.shape[0], NAT_W - OFF_GK - GLA_GATE_RANK), w_in.dtype)
    return jnp.concatenate([q_g, k_g, v_g, r_g, gg, ga, *qkv(0), gk, pad, *qkv(1), *qkv(2)], axis=1)


def _kv_rows(src, col_k, n_keep):
    b, dil, n_sub, _ = src.shape
    kv = src[:, :, n_sub - n_keep // dil:, col_k:col_k + 2 * ATTN_GROUP_DIM]
    kv = jnp.swapaxes(kv, 1, 2).astype(F32)
    return kv.reshape(b, n_keep, 2, ATTN_HEADS, ATTN_HEAD_DIM)


def kernel(x_prompt, x_sample, state_gla, cache_kv_w128, cache_kv_w512, cache_kv_w2048, g_norm_mix, w_in, w_gk_up,
           b_gk_up, g_gla_norm, w_branch_gla, w_branch_attn, w_out, g_norm_ffn, w_router, b_router, w_gate_up,
           b_gate_up, w_down, b_down, g_final):
    depth = g_norm_mix.shape[0]
    assert depth == 1, "single-layer trunk"
    batch, seq, d = x_prompt.shape
    bd, dec_seq, _ = x_sample.shape
    assert d == D_MODEL and dec_seq == 1 and seq % ATTN_GROUPS[-1][0] == 0
    n_experts = w_router.shape[-1]
    caches = (cache_kv_w128[0], cache_kv_w512[0], cache_kv_w2048[0])
    gw = ATTN_GROUP_DIM

    w_in_f32 = _prep_w_in(w_in[0])
    w_in_bf = w_in_f32.astype(BF16)
    w_up = jnp.zeros((LANES, GLA_HEADS * GLA_HEAD_K), F32).at[:GLA_GATE_RANK].set(w_gk_up[0])
    w_up_bf = w_up.astype(BF16)
    b_up = b_gk_up[0][None, :]
    g_mix = g_norm_mix[0][None, :]
    g_gla = g_gla_norm[0][None, :]
    wbg, wba, wo = w_branch_gla[0], w_branch_attn[0], w_out[0]
    g_ffn = g_norm_ffn[0][None, :]
    w_r = jnp.zeros((d, LANES), F32).at[:, :n_experts].set(w_router[0])
    wr_hi = w_r.astype(BF16)
    wr_lo = (w_r - wr_hi.astype(F32)).astype(BF16)
    b_r = jnp.zeros((1, LANES), F32).at[0, :n_experts].set(b_router[0])
    bgu, bdn = b_gate_up[0][:, None, :], b_down[0][:, None, :]
    g_fin = g_final[None, :]

    xp = x_prompt.reshape(batch * seq, d)
    nat = _norm_proj(xp, g_mix, w_in_bf[:, :NAT_W], batch, seq, 1, tm=1024, tn=1024)
    srcs = [nat] + [_norm_proj(xp, g_mix, w_in_bf[:, OFF_QKV[g]:OFF_QKV[g] + QKV_W], batch, seq, ATTN_GROUPS[g][1],
                               tm=1024, tn=QKV_W) for g in (1, 2)]
    nat2d = nat.reshape(batch * seq, NAT_W)
    u_p, st_p = _gla_prompt(nat2d, w_up_bf, b_up, g_gla, batch, seq, tc=512)
    outs, lses = [], []
    for g in range(N_GROUPS):
        col0 = OFF_QKV0 if g == 0 else 0
        o, lse = _attn_prompt(srcs[g], col0, g, tq=min(256, seq // ATTN_GROUPS[g][1]))
        outs.append(o)
        lses.append(lse)
    x1_p, h2_p, route_p, cpad_p = _merge(xp, u_p, nat2d, outs, lses, wbg.astype(BF16), wba.astype(BF16), wo.astype(BF16),
                                 g_ffn, wr_hi, wr_lo, b_r, n_experts, seq, tm=512, ts=MOE_TOKEN_TILE, precise=False)

    xs = x_sample.reshape(bd, d)
    proj_s = _norm_proj_precise(xs, g_mix, w_in_f32, tn=1024)
    proj_s3 = proj_s.reshape(bd, 1, PROJ_W)
    u_s, st_s = _gla_sample(proj_s3, state_gla[0], w_up, b_up, g_gla)
    o_s, l_s = _attn_sample(proj_s3, caches)
    outs_s = [o_s[:, 0, g * gw:(g + 1) * gw].reshape(1, 1, bd, gw) for g in range(N_GROUPS)]
    lses_s = [l_s[:, 0, g * gw:(g + 1) * gw].reshape(1, 1, bd, gw) for g in range(N_GROUPS)]
    x1_s, h2_s, route_s, cpad_s = _merge(xs, u_s.reshape(bd, d), proj_s, outs_s, lses_s, wbg, wba, wo, g_ffn,
                                 w_r, w_r, b_r, n_experts, bd, tm=bd, ts=bd, precise=True)

    ts, tm_e = MOE_TOKEN_TILE, MOE_EXPERT_TILE
    assert bd <= ts
    n_tiles_p = batch * seq // ts
    rows = _tile_rows(ts, n_experts)
    r_pad, nc = _round_up(rows, LANES), rows // ROW_ALIGN
    n_tiles_e = -(-(n_tiles_p * rows + _tile_rows(bd, n_experts) + n_experts * (tm_e - 1)) // tm_e)
    cpad = jnp.concatenate([cpad_p[::8, :n_experts], cpad_s[::8, :n_experts]], axis=0).astype(jnp.int32)
    plan = _moe_plan(cpad, nc, n_experts, tm_e, n_tiles_e)
    pad_rows = lambda a, fill: jnp.concatenate([a, jnp.full((ts - bd, a.shape[1]), fill, a.dtype)], axis=0)
    h2_s, x1_s = pad_rows(h2_s, 0), pad_rows(x1_s, 0)
    route_s = pad_rows(route_s, -1.0)
    xs = _compact(h2_p, route_p, h2_s, route_s, plan, ts, r_pad, nc, n_tiles_e, tm_e)
    ys = _experts(xs, plan[2], plan[3], w_gate_up[0], bgu, w_down[0], bdn, tm_e)
    y_p = _combine(route_p, x1_p, g_fin, ys, plan, 0, ts, r_pad, nc)
    y_s = _combine(route_s, x1_s, g_fin, ys, plan, n_tiles_p, ts, r_pad, nc)[:bd]

    kv_p = [_kv_rows(srcs[g], (OFF_QKV0 if g == 0 else 0) + gw, min(ATTN_GROUPS[g][0], seq))[None]
            for g in range(N_GROUPS)]
    kv_s = []
    for g in range(N_GROUPS):
        new = proj_s[:, OFF_QKV[g] + gw:OFF_QKV[g] + 3 * gw].reshape(bd, 1, 2, ATTN_HEADS, ATTN_HEAD_DIM)
        kv_s.append(jnp.concatenate([caches[g][:, 1:], new.astype(caches[g].dtype)], axis=1)[None])
    return (y_p.reshape(batch, seq, d), y_s.reshape(bd, 1, d),
            jnp.swapaxes(st_p, -1, -2)[None], kv_p[0], kv_p[1], kv_p[2],
            st_s[None], kv_s[0], kv_s[1], kv_s[2])
```

```python
import functools

import jax
import jax.numpy as jnp
from jax import lax
from jax.experimental import pallas as pl
from jax.experimental.pallas import tpu as pltpu

F32 = jnp.float32
BF16 = jnp.bfloat16
HIGHEST = lax.Precision.HIGHEST

D_MODEL = 1024
NORM_EPS = 1e-5
GLA_HEADS = 4
GLA_HEAD_K = 128
GLA_HEAD_V = 256
GLA_GATE_RANK = 16
GLA_GATE_NORMALIZER = 16.0
GLA_CHUNK = 64
ATTN_GROUPS = ((128, 1), (512, 4), (2048, 16))
N_GROUPS = 3
ATTN_HEADS = 4
ATTN_HEAD_DIM = 128
ATTN_GROUP_DIM = ATTN_HEADS * ATTN_HEAD_DIM
TOP_K = 4
SWIGLU_LIMIT = 7.0
SWIGLU_ALPHA = 1.702
NEG_BIG = -1e30
LANES = 128
ROW_ALIGN = 8
MOE_TOKEN_TILE = 256
MOE_EXPERT_TILE = 512

OFF_QG, OFF_KG, OFF_VG, OFF_RG = 0, 512, 1024, 2048
OFF_GATE_GLA, OFF_GATE_ATTN = 3072, 4096
OFF_QKV0 = 5120
OFF_GK = 6656
NAT_W = 7168
QKV_W = 3 * ATTN_GROUP_DIM
PROJ_W = NAT_W + 2 * QKV_W
OFF_QKV = (OFF_QKV0, NAT_W, NAT_W + QKV_W)
PROJ_SPLITS = (512, 512, 1024, 1024, GLA_GATE_RANK, 1536, 1536, 1536, 1024, 1024)

VMEM_LIMIT = 48 * 1024 * 1024


def _cparams(*sem):
    return pltpu.CompilerParams(dimension_semantics=sem, vmem_limit_bytes=VMEM_LIMIT)


def _alibi_slope(group, head):
    n = N_GROUPS * ATTN_HEADS
    return 2.0 ** (-8.0 * (group * ATTN_HEADS + head + 1) / n)


def _log_sigmoid(x):
    return jnp.minimum(x, 0.0) - jnp.log1p(jnp.exp(-jnp.abs(x)))


def _dot_nt(a, b):
    return lax.dot_general(a, b, (((1,), (1,)), ((), ())), preferred_element_type=F32)


def _dot(a, b):
    return jnp.dot(a, b, preferred_element_type=F32)


def _dot_hi(a, b):
    return jnp.dot(a, b, preferred_element_type=F32, precision=HIGHEST)


def _split3(x):
    hi = x.astype(BF16)
    r1 = x - hi.astype(F32)
    mid = r1.astype(BF16)
    lo = (r1 - mid.astype(F32)).astype(BF16)
    return hi, mid, lo


def _rmsnorm(x, g):
    return x * lax.rsqrt(jnp.mean(x * x, axis=-1, keepdims=True) + NORM_EPS) * g


def _proj_kernel(x_ref, g_ref, w_ref, o_ref, h_ref, *scr, dilation):
    @pl.when(pl.program_id(1) == 0)
    def _():
        h_ref[...] = _rmsnorm(x_ref[...], g_ref[...]).astype(BF16)

    acc = _dot(h_ref[...], w_ref[...])
    if dilation == 1:
        o_ref[0, 0] = acc.astype(o_ref.dtype)
    else:
        n = acc.shape[0] // dilation
        for c in range(acc.shape[1] // LANES):
            cols = slice(c * LANES, (c + 1) * LANES)
            scr[0][c] = acc[:, cols]
            for r in range(dilation):
                o_ref[0, r, :, cols] = scr[0][c, pl.ds(r, n, stride=dilation), :].astype(o_ref.dtype)


def _norm_proj(x2d, g_row, w_bf, batch, seq, dilation, tm, tn):
    m, d = x2d.shape
    n = w_bf.shape[1]
    tps = seq // tm
    scratch = [pltpu.VMEM((tm, d), BF16)]
    if dilation > 1:
        scratch.append(pltpu.VMEM((tn // LANES, tm, LANES), F32))
    return pl.pallas_call(
        functools.partial(_proj_kernel, dilation=dilation),
        grid=(m // tm, n // tn),
        in_specs=[pl.BlockSpec((tm, d), lambda i, j: (i, 0)),
                  pl.BlockSpec((1, d), lambda i, j: (0, 0)),
                  pl.BlockSpec((d, tn), lambda i, j: (0, j))],
        out_specs=pl.BlockSpec((1, dilation, tm // dilation, tn), lambda i, j: (i // tps, 0, i % tps, j)),
        out_shape=jax.ShapeDtypeStruct((batch, dilation, seq // dilation, n), BF16),
        scratch_shapes=scratch,
        compiler_params=_cparams("parallel", "arbitrary"),
        name=f"norm_proj_d{dilation}",
    )(x2d, g_row, w_bf)


def _proj_precise_kernel(x_ref, g_ref, w_ref, o_ref):
    o_ref[...] = _dot_hi(_rmsnorm(x_ref[...], g_ref[...]), w_ref[...])


def _norm_proj_precise(x2d, g_row, w_f32, tn):
    m, d = x2d.shape
    n = w_f32.shape[1]
    return pl.pallas_call(
        _proj_precise_kernel,
        grid=(n // tn,),
        in_specs=[pl.BlockSpec((m, d), lambda j: (0, 0)),
                  pl.BlockSpec((1, d), lambda j: (0, 0)),
                  pl.BlockSpec((d, tn), lambda j: (0, j))],
        out_specs=pl.BlockSpec((m, tn), lambda j: (0, j)),
        out_shape=jax.ShapeDtypeStruct((m, n), F32),
        compiler_params=_cparams("parallel"),
        name="norm_proj_sample",
    )(x2d, g_row, w_f32)


def _gla_kernel(q_ref, k_ref, v_ref, r_ref, gk_ref, wup_ref, bup_ref, gn_ref,
                u_ref, st_ref, s_scr, o_scr, *, n_chunks):
    t = pl.program_id(1)
    kd, vd = GLA_HEAD_K, GLA_HEAD_V

    @pl.when(t == 0)
    def _():
        s_scr[...] = jnp.zeros_like(s_scr)

    gk = _dot(gk_ref[...], wup_ref[...]) + bup_ref[...]
    log_a = _log_sigmoid(gk) * (1.0 / GLA_GATE_NORMALIZER)
    c_sz = GLA_CHUNK
    row = lax.broadcasted_iota(jnp.int32, (c_sz, c_sz), 0)
    col = lax.broadcasted_iota(jnp.int32, (c_sz, c_sz), 1)
    tri = row >= col
    tri_bf = jnp.where(tri, 1.0, 0.0).astype(BF16)
    scale = kd ** -0.5

    for c in range(n_chunks):
        sl = slice(c * c_sz, (c + 1) * c_sz)
        g_hi, g_mid, g_lo = _split3(log_a[sl])
        b_all = _dot(tri_bf, g_hi) + _dot(tri_bf, g_mid) + _dot(tri_bf, g_lo)
        for h in range(GLA_HEADS):
            kc, vc = slice(h * kd, (h + 1) * kd), slice(h * vd, (h + 1) * vd)
            b = b_all[:, kc]
            b_last = b[c_sz - 1:c_sz, :]
            q = q_ref[sl, kc].astype(F32) * scale
            k = k_ref[sl, kc].astype(F32)
            v_bf = v_ref[sl, vc]
            q_in = (q * jnp.exp(b)).astype(BF16)
            k_in = (k * jnp.exp(-b)).astype(BF16)
            k_out = (k * jnp.exp(b_last - b)).astype(BF16)
            a = jnp.where(tri, _dot_nt(q_in, k_in), 0.0)
            st = s_scr[h]
            o_scr[sl, vc] = _dot_nt(q_in, st.astype(BF16)) + _dot(a.astype(BF16), v_bf)
            s_scr[h] = st * jnp.exp(b_last) + pl.dot(v_bf, k_out, trans_a=True)

    for h in range(GLA_HEADS):
        vc = slice(h * vd, (h + 1) * vd)
        r = r_ref[:, vc].astype(F32)
        u_ref[:, vc] = (_rmsnorm(o_scr[:, vc], gn_ref[...]) * (r * jax.nn.sigmoid(r))).astype(u_ref.dtype)

    @pl.when(t == pl.num_programs(1) - 1)
    def _():
        st_ref[0] = s_scr[...]


def _gla_prompt(proj, w_up_bf, b_up, g_norm, batch, seq, tc):
    nt = seq // tc
    kw, vw = GLA_HEADS * GLA_HEAD_K, GLA_HEADS * GLA_HEAD_V
    rows = lambda b, t: b * nt + t
    kern = functools.partial(_gla_kernel, n_chunks=tc // GLA_CHUNK)
    full = lambda a: pl.BlockSpec(a.shape, lambda b, t: (0, 0))
    return pl.pallas_call(
        kern,
        grid=(batch, nt),
        in_specs=[
            pl.BlockSpec((tc, kw), lambda b, t: (rows(b, t), OFF_QG // kw)),
            pl.BlockSpec((tc, kw), lambda b, t: (rows(b, t), OFF_KG // kw)),
            pl.BlockSpec((tc, vw), lambda b, t: (rows(b, t), OFF_VG // vw)),
            pl.BlockSpec((tc, vw), lambda b, t: (rows(b, t), OFF_RG // vw)),
            pl.BlockSpec((tc, LANES), lambda b, t: (rows(b, t), OFF_GK // LANES)),
            full(w_up_bf), full(b_up), full(g_norm),
        ],
        out_specs=[
            pl.BlockSpec((tc, vw), lambda b, t: (rows(b, t), 0)),
            pl.BlockSpec((1, GLA_HEADS, GLA_HEAD_V, GLA_HEAD_K), lambda b, t: (b, 0, 0, 0)),
        ],
        out_shape=[jax.ShapeDtypeStruct((batch * seq, vw), BF16),
                   jax.ShapeDtypeStruct((batch, GLA_HEADS, GLA_HEAD_V, GLA_HEAD_K), F32)],
        scratch_shapes=[pltpu.VMEM((GLA_HEADS, GLA_HEAD_V, GLA_HEAD_K), F32), pltpu.VMEM((tc, vw), F32)],
        compiler_params=_cparams("parallel", "arbitrary"),
        name="gla_prompt",
    )(proj, proj, proj, proj, proj, w_up_bf, b_up, g_norm)


def _attn_kernel(q_ref, kc_ref, kp_ref, vc_ref, vp_ref, o_ref, l_ref, *, group, dilation, blk, n_qb):
    n = pl.program_id(2)
    e = ATTN_HEAD_DIM
    i_idx = lax.broadcasted_iota(jnp.int32, (blk, 2 * blk), 0)
    j_idx = lax.broadcasted_iota(jnp.int32, (blk, 2 * blk), 1)
    rel = i_idx + blk - j_idx
    band = (rel >= 0) & (rel <= blk)
    dist = (dilation * rel).astype(F32)
    for qb in range(n_qb):
        rows = slice(qb * blk, (qb + 1) * blk)
        q = q_ref[0, 0, rows, :]
        if qb == 0:
            k_prev, v_prev = kp_ref[0, 0], vp_ref[0, 0]
            valid = band & ((j_idx >= blk) | (n > 0))
        else:
            prev = slice((qb - 1) * blk, qb * blk)
            k_prev, v_prev = kc_ref[0, 0, prev, :], vc_ref[0, 0, prev, :]
            valid = band
        kk = jnp.concatenate([k_prev, kc_ref[0, 0, rows, :]], axis=0)
        vv = jnp.concatenate([v_prev, vc_ref[0, 0, rows, :]], axis=0)
        for h in range(ATTN_HEADS):
            cols = slice(h * e, (h + 1) * e)
            s = _dot_nt(q[:, cols], kk[:, cols]) * (e ** -0.5) - _alibi_slope(group, h) * dist
            s = jnp.where(valid, s, NEG_BIG)
            m = jnp.max(s, axis=-1, keepdims=True)
            p = jnp.exp(s - m)
            den = jnp.sum(p, axis=-1, keepdims=True)
            o = _dot(p.astype(BF16), vv[:, cols]) / den
            o_ref[0, 0, rows, cols] = o.astype(o_ref.dtype)
            l_ref[0, 0, rows, cols] = jnp.broadcast_to(m + jnp.log(den), (blk, e))


def _attn_prompt(src, col0, group, tq):
    batch, dilation, n_sub, _ = src.shape
    window, dil = ATTN_GROUPS[group]
    assert dil == dilation
    blk = window // dilation
    n_qb = tq // blk
    gw = ATTN_GROUP_DIM
    qc, kc, vc = col0 // gw, col0 // gw + 1, col0 // gw + 2
    cur = lambda c: pl.BlockSpec((1, 1, tq, gw), lambda b, r, n: (b, r, n, c))
    prev = lambda c: pl.BlockSpec((1, 1, blk, gw), lambda b, r, n: (b, r, jnp.maximum(n * n_qb - 1, 0), c))
    kern = functools.partial(_attn_kernel, group=group, dilation=dilation, blk=blk, n_qb=n_qb)
    out_spec = pl.BlockSpec((1, 1, tq, gw), lambda b, r, n: (b, r, n, 0))
    return pl.pallas_call(
        kern,
        grid=(batch, dilation, n_sub // tq),
        in_specs=[cur(qc), cur(kc), prev(kc), cur(vc), prev(vc)],
        out_specs=[out_spec, out_spec],
        out_shape=[jax.ShapeDtypeStruct((batch, dilation, n_sub, gw), BF16),
                   jax.ShapeDtypeStruct((batch, dilation, n_sub, gw), F32)],
        compiler_params=_cparams("parallel", "parallel", "arbitrary"),
        name=f"attn_prompt_g{group}",
    )(src, src, src, src, src)


def _merge_kernel(x_ref, u_ref, gg_ref, ga_ref, o1_ref, o2_ref, o3_ref, l1_ref, l2_ref, l3_ref,
                  wbg_ref, wba_ref, wo_ref, gf_ref, wr_hi_ref, wr_lo_ref, br_ref,
                  x1_ref, h2_ref, route_ref, cpad_ref, *scr, n_experts, dilations, precise, ts):
    tm = x_ref.shape[0]
    scr = list(scr)

    def natural(ref, dilation):
        if dilation == 1:
            return ref[0, 0].astype(F32)
        buf = scr.pop(0)
        n = tm // dilation
        for c in range(buf.shape[0]):
            for r in range(dilation):
                buf[c, pl.ds(r, n, stride=dilation), :] = ref[0, r, :, c * LANES:(c + 1) * LANES].astype(F32)
        return jnp.concatenate([buf[c] for c in range(buf.shape[0])], axis=1)

    o1, o2, o3 = (natural(r, d) for r, d in zip((o1_ref, o2_ref, o3_ref), dilations))
    l1, l2, l3 = (natural(r, d) for r, d in zip((l1_ref, l2_ref, l3_ref), dilations))
    mm = _dot_hi if precise else (lambda a, b: _dot(a.astype(BF16), b))
    lm = jnp.maximum(jnp.maximum(l1, l2), l3)
    e1, e2, e3 = jnp.exp(l1 - lm), jnp.exp(l2 - lm), jnp.exp(l3 - lm)
    o_att = (e1 * o1 + e2 * o2 + e3 * o3) / (e1 + e2 + e3)
    merged = (jax.nn.sigmoid(gg_ref[...].astype(F32)) * mm(u_ref[...], wbg_ref[...])
              + jax.nn.sigmoid(ga_ref[...].astype(F32)) * mm(o_att, wba_ref[...]))
    x1 = x_ref[...] + mm(merged, wo_ref[...])
    x1_ref[...] = x1
    h2 = _rmsnorm(x1, gf_ref[...])
    h2_ref[...] = h2.astype(BF16)

    if precise:
        logits = _dot_hi(h2, wr_hi_ref[...]) + br_ref[...]
    else:
        h_hi, h_mid, _ = _split3(h2)
        logits = (_dot(h_hi, wr_hi_ref[...]) + _dot(h_mid, wr_hi_ref[...]) + _dot(h_hi, wr_lo_ref[...])) + br_ref[...]
    lane = lax.broadcasted_iota(jnp.int32, (ts, LANES), 1).astype(F32)
    ri = lax.broadcasted_iota(jnp.int32, (ts, ts), 0)
    ci = lax.broadcasted_iota(jnp.int32, (ts, ts), 1)
    earlier = jnp.where(ci < ri, 1.0, 0.0).astype(BF16)
    li = lax.broadcasted_iota(jnp.int32, (LANES, LANES), 0)
    lj = lax.broadcasted_iota(jnp.int32, (LANES, LANES), 1)
    before = jnp.where(li < lj, 1.0, 0.0).astype(BF16)
    for s in range(tm // ts):
        rows = slice(s * ts, (s + 1) * ts)
        cur = jnp.where(lane < n_experts, logits[rows, :], -jnp.inf)
        tops, sels = [], []
        for _ in range(TOP_K):
            m = jnp.max(cur, axis=-1, keepdims=True)
            idx = jnp.min(jnp.where(cur == m, lane, float(LANES)), axis=-1, keepdims=True)
            sel = lane == idx
            tops.append(m)
            sels.append(sel)
            cur = jnp.where(sel, -jnp.inf, cur)
        ex = [jnp.exp(m - tops[0]) for m in tops]
        den = ex[0] + ex[1] + ex[2] + ex[3]
        mem = jnp.zeros((ts, LANES), F32)
        for sel in sels:
            mem = jnp.where(sel, 1.0, mem)
        rank = _dot(earlier, mem.astype(BF16))
        count = jnp.sum(mem, axis=0, keepdims=True)
        cpad = jnp.ceil(count * (1.0 / ROW_ALIGN)) * ROW_ALIGN
        cpad8 = jnp.broadcast_to(cpad, (8, LANES))
        seg_start = _dot(cpad8.astype(BF16), before)[0:1]
        pos = seg_start + rank
        route = jnp.zeros((ts, LANES), F32)
        for k in range(TOP_K):
            dest = jnp.sum(jnp.where(sels[k], pos, 0.0), axis=-1, keepdims=True)
            route = jnp.where(lane == float(k), dest, route)
            route = jnp.where(lane == float(TOP_K + k), ex[k] / den, route)
        route_ref[rows, :] = route
        cpad_ref[s * 8:(s + 1) * 8, :] = cpad8


def _merge(x2d, u, gate_src, outs, lses, wbg, wba, wo, g_ffn, wr_hi, wr_lo, b_r, n_experts, seq, tm, ts, precise):
    m, d = x2d.shape
    gw = ATTN_GROUP_DIM
    tps = seq // tm
    dilations = tuple(o.shape[1] for o in outs)
    row = lambda w: pl.BlockSpec((tm, w), lambda i: (i, 0))
    full = lambda a: pl.BlockSpec(a.shape, lambda i: (0, 0))
    grp = lambda dil: pl.BlockSpec((1, dil, tm // dil, gw), lambda i: (i // tps, 0, i % tps, 0))
    kern = functools.partial(_merge_kernel, n_experts=n_experts, dilations=dilations, precise=precise, ts=ts)
    n_scr = 2 * sum(1 for dil in dilations if dil > 1)
    return pl.pallas_call(
        kern,
        grid=(m // tm,),
        in_specs=[row(d), row(d),
                  pl.BlockSpec((tm, d), lambda i: (i, OFF_GATE_GLA // d)),
                  pl.BlockSpec((tm, d), lambda i: (i, OFF_GATE_ATTN // d)),
                  *[grp(dil) for dil in dilations], *[grp(dil) for dil in dilations],
                  full(wbg), full(wba), full(wo), full(g_ffn), full(wr_hi), full(wr_lo), full(b_r)],
        out_specs=[row(d), row(d), row(LANES), pl.BlockSpec((tm // ts * 8, LANES), lambda i: (i, 0))],
        out_shape=[jax.ShapeDtypeStruct((m, d), F32), jax.ShapeDtypeStruct((m, d), BF16),
                   jax.ShapeDtypeStruct((m, LANES), F32), jax.ShapeDtypeStruct((m // ts * 8, LANES), F32)],
        scratch_shapes=[pltpu.VMEM((gw // LANES, tm, LANES), F32)] * n_scr,
        compiler_params=_cparams("parallel"),
        name="merge_router_sample" if precise else "merge_router",
    )(x2d, u, gate_src, gate_src, *outs, *lses, wbg, wba, wo, g_ffn, wr_hi, wr_lo, b_r)


def _round_up(x, mult):
    return (x + mult - 1) // mult * mult


def _tile_rows(n_tokens, n_experts):
    return _round_up(n_tokens * TOP_K + n_experts * (ROW_ALIGN - 1), ROW_ALIGN)


def _moe_plan(cpad, nc, n_experts, tm_e, n_tiles_max):
    tot = jnp.sum(cpad, axis=0)
    gsize = (tot + tm_e - 1) // tm_e * tm_e
    g_end = jnp.cumsum(gsize)
    g_start = g_end - gsize
    n_used = (g_end[-1] // tm_e).astype(jnp.int32)
    o_start = jnp.cumsum(cpad, axis=0) - cpad
    seg_end = jnp.cumsum(cpad, axis=1)
    seg_start = seg_end - cpad

    row0 = jnp.arange(nc, dtype=jnp.int32) * ROW_ALIGN
    e_of = jnp.sum((seg_end[:, None, :] <= row0[None, :, None]).astype(jnp.int32), axis=-1)
    e_of = jnp.minimum(e_of, n_experts - 1)
    pick = (e_of[:, :, None] == jnp.arange(n_experts)[None, None, :]).astype(jnp.int32)
    base = g_start[None, :] + o_start - seg_start
    dst = jnp.sum(pick * base[:, None, :], axis=-1) + row0[None, :]
    nch = (seg_end[:, -1] // ROW_ALIGN).astype(jnp.int32)
    dst = jnp.where(jnp.arange(nc)[None, :] < nch[:, None], dst, 0).astype(jnp.int32)

    t0 = jnp.arange(n_tiles_max, dtype=jnp.int32) * tm_e
    tile_expert = jnp.minimum(jnp.sum((g_end[None, :] <= t0[:, None]).astype(jnp.int32), axis=-1), n_experts - 1)
    tile_id = jnp.arange(n_tiles_max)
    last = jnp.sum(jnp.where(tile_id == n_used - 1, tile_expert, 0))
    tile_expert = jnp.where(tile_id < n_used, tile_expert, last).astype(jnp.int32)
    gap_start = (g_start + tot).astype(jnp.int32)
    gap_chunks = ((gsize - tot) // ROW_ALIGN).astype(jnp.int32)
    return dst.reshape(-1), nch, tile_expert, n_used.reshape(1), gap_start, gap_chunks


def _loop(n, fn):
    lax.fori_loop(0, n, lambda i, carry: (fn(i), carry)[1], 0)


def _compact_kernel(dst_ref, nch_ref, gap_start_ref, gap_chunks_ref, nu_ref, hp_ref, rp_ref, hs_ref, rs_ref,
                    xs_ref, xc_scr, zero_scr, sem, *, nc):
    s = pl.program_id(0)
    last = pl.num_programs(0) - 1
    is_sample = s == last
    ts, r_pad = hp_ref.shape[0], xc_scr.shape[1]

    def chunk(tile, c):
        slot = tile % 2
        src = xc_scr.at[slot, pl.ds(pl.multiple_of(c * ROW_ALIGN, ROW_ALIGN), ROW_ALIGN)]
        dst = xs_ref.at[pl.ds(pl.multiple_of(dst_ref[tile * nc + c], ROW_ALIGN), ROW_ALIGN)]
        return pltpu.make_async_copy(src, dst, sem.at[slot])

    def drain(tile):
        _loop(nch_ref[tile], lambda c: chunk(tile, c).wait())

    @pl.when(s >= 2)
    def _():
        drain(s - 2)

    h = jnp.where(is_sample, hs_ref[...], hp_ref[...])
    route = jnp.where(is_sample, rs_ref[...], rp_ref[...])
    dest_t = jnp.transpose(route)
    row_id = lax.broadcasted_iota(jnp.int32, (r_pad, ts), 0).astype(F32)
    perm = jnp.zeros((r_pad, ts), F32)
    for k in range(TOP_K):
        perm = jnp.where(row_id == dest_t[k:k + 1, :], 1.0, perm)
    xc_scr[s % 2] = _dot(perm.astype(BF16), h)
    _loop(nch_ref[s], lambda c: chunk(s, c).start())

    @pl.when(is_sample)
    def _():
        @pl.when(s >= 1)
        def _():
            drain(s - 1)
        drain(s)

        zero_scr[...] = jnp.zeros_like(zero_scr)
        tile = zero_scr.shape[0]

        def gap(e, j):
            row = pl.multiple_of(gap_start_ref[e] + j * ROW_ALIGN, ROW_ALIGN)
            return pltpu.make_async_copy(zero_scr.at[pl.ds(0, ROW_ALIGN)], xs_ref.at[pl.ds(row, ROW_ALIGN)], sem.at[0])

        def tail(t):
            row = pl.multiple_of(t * tile, tile)
            return pltpu.make_async_copy(zero_scr, xs_ref.at[pl.ds(row, tile)], sem.at[0])

        n_exp = gap_start_ref.shape[0]
        n_tail = xs_ref.shape[0] // tile - nu_ref[0]
        _loop(n_exp, lambda e: _loop(gap_chunks_ref[e], lambda j: gap(e, j).start()))
        _loop(n_tail, lambda t: tail(nu_ref[0] + t).start())
        _loop(n_exp, lambda e: _loop(gap_chunks_ref[e], lambda j: gap(e, j).wait()))
        _loop(n_tail, lambda t: tail(nu_ref[0] + t).wait())


def _compact(h2_p, route_p, h2_s, route_s, plan, ts, r_pad, nc, n_tiles_e, tm_e):
    m, d = h2_p.shape
    n_p = m // ts
    dst, nch, _, n_used, gap_start, gap_chunks = plan
    prompt = lambda w: pl.BlockSpec((ts, w), lambda i, *_: (jnp.minimum(i, n_p - 1), 0))
    sample = lambda w: pl.BlockSpec((ts, w), lambda i, *_: (0, 0))
    return pl.pallas_call(
        functools.partial(_compact_kernel, nc=nc),
        grid_spec=pltpu.PrefetchScalarGridSpec(
            num_scalar_prefetch=5,
            grid=(n_p + 1,),
            in_specs=[prompt(d), prompt(LANES), sample(d), sample(LANES)],
            out_specs=pl.BlockSpec(memory_space=pl.ANY),
            scratch_shapes=[pltpu.VMEM((2, r_pad, d), F32), pltpu.VMEM((tm_e, d), F32),
                            pltpu.SemaphoreType.DMA((2,))]),
        out_shape=jax.ShapeDtypeStruct((n_tiles_e * tm_e, d), F32),
        compiler_params=_cparams("arbitrary"),
        name="moe_compact",
    )(dst, nch, gap_start, gap_chunks, n_used, h2_p, route_p, h2_s, route_s)


def _expert_kernel(te_ref, nu_ref, x_ref, wgu_ref, bgu_ref, wd_ref, bd_ref, *rest, d_ff, sub, n_caches):
    old_refs, new_refs = rest[:n_caches], rest[n_caches:2 * n_caches]
    o_ref = rest[2 * n_caches]
    out_refs = rest[2 * n_caches + 1:3 * n_caches + 1]
    wgu_bf, wd_bf, sem = rest[3 * n_caches + 1:]
    t = pl.program_id(0)
    used = t < nu_ref[0]

    def shift_copies():
        copies = []
        for g in range(n_caches):
            keep = old_refs[g].shape[1] - 1
            copies.append(pltpu.make_async_copy(old_refs[g].at[:, pl.ds(1, keep)],
                                                out_refs[g].at[:, pl.ds(0, keep)], sem.at[2 * g]))
            copies.append(pltpu.make_async_copy(new_refs[g], out_refs[g].at[:, pl.ds(keep, 1)], sem.at[2 * g + 1]))
        return copies

    @pl.when(t == 0)
    def _():
        for cp in shift_copies():
            cp.start()

    @pl.when(t == pl.num_programs(0) - 1)
    def _():
        for cp in shift_copies():
            cp.wait()

    @pl.when(used)
    def _():
        @pl.when((t == 0) | (te_ref[t] != te_ref[jnp.maximum(t - 1, 0)]))
        def _():
            step = 128
            for i in range(wgu_bf.shape[0] // step):
                rows = slice(i * step, (i + 1) * step)
                wgu_bf[rows, :] = wgu_ref[0, rows, :].astype(BF16)
            for i in range(wd_bf.shape[0] // step):
                rows = slice(i * step, (i + 1) * step)
                wd_bf[rows, :] = wd_ref[0, rows, :].astype(BF16)

        for s in range(x_ref.shape[0] // sub):
            rows = slice(s * sub, (s + 1) * sub)
            gu = _dot(x_ref[rows, :].astype(BF16), wgu_bf[...]) + bgu_ref[0]
            gate = jnp.minimum(gu[:, :d_ff], SWIGLU_LIMIT)
            up = jnp.clip(gu[:, d_ff:], -SWIGLU_LIMIT, SWIGLU_LIMIT)
            act = (up + 1.0) * gate * jax.nn.sigmoid(SWIGLU_ALPHA * gate)
            o_ref[rows, :] = _dot(act.astype(BF16), wd_bf[...]) + bd_ref[0]

    @pl.when(jnp.logical_not(used))
    def _():
        o_ref[...] = jnp.zeros_like(o_ref)


def _experts(xs, tile_expert, n_used, wgu, bgu, wd, bd, tm_e, caches, new_rows):
    n_rows, d = xs.shape
    _, _, two_ff = wgu.shape
    d_ff = two_ff // 2
    n_caches = len(caches)
    kern = functools.partial(_expert_kernel, d_ff=d_ff, sub=tm_e, n_caches=n_caches)
    by_expert = lambda shape: pl.BlockSpec(shape, lambda t, te, nu: (te[t], 0, 0))
    hbm = pl.BlockSpec(memory_space=pl.ANY)
    res = pl.pallas_call(
        kern,
        grid_spec=pltpu.PrefetchScalarGridSpec(
            num_scalar_prefetch=2,
            grid=(n_rows // tm_e,),
            in_specs=[pl.BlockSpec((tm_e, d), lambda t, te, nu: (jnp.minimum(t, nu[0] - 1), 0)),
                      by_expert((1, d, two_ff)), by_expert((1, 1, two_ff)),
                      by_expert((1, d_ff, d)), by_expert((1, 1, d))] + [hbm] * (2 * n_caches),
            out_specs=[pl.BlockSpec((tm_e, d), lambda t, te, nu: (t, 0))] + [hbm] * n_caches,
            scratch_shapes=[pltpu.VMEM((d, two_ff), BF16), pltpu.VMEM((d_ff, d), BF16),
                            pltpu.SemaphoreType.DMA((2 * n_caches,))]),
        out_shape=[jax.ShapeDtypeStruct((n_rows, d), F32)]
                  + [jax.ShapeDtypeStruct(c.shape, c.dtype) for c in caches],
        compiler_params=pltpu.CompilerParams(dimension_semantics=("arbitrary",), vmem_limit_bytes=56 * 1024 * 1024),
        name="moe_experts",
    )(tile_expert, n_used, xs, wgu, bgu, wd, bd, *caches, *new_rows)
    return res[0], res[1:]


def _combine_kernel(dst_ref, nch_ref, route_ref, x1_ref, gfin_ref, ys_ref, y_ref, yc_scr, sem, *, nc, tile0):
    i = pl.program_id(0)

    def chunk(step, c):
        slot = step % 2
        tile = step + tile0
        src = ys_ref.at[pl.ds(pl.multiple_of(dst_ref[tile * nc + c], ROW_ALIGN), ROW_ALIGN)]
        dst = yc_scr.at[slot, pl.ds(pl.multiple_of(c * ROW_ALIGN, ROW_ALIGN), ROW_ALIGN)]
        return pltpu.make_async_copy(src, dst, sem.at[slot])

    def fetch(step):
        _loop(nch_ref[step + tile0], lambda c: chunk(step, c).start())

    @pl.when(i == 0)
    def _():
        yc_scr[...] = jnp.zeros_like(yc_scr)
        fetch(i)

    @pl.when(i + 1 < pl.num_programs(0))
    def _():
        fetch(i + 1)

    _loop(nch_ref[i + tile0], lambda c: chunk(i, c).wait())

    ts, r_pad = x1_ref.shape[0], yc_scr.shape[1]
    route = route_ref[...]
    col_id = lax.broadcasted_iota(jnp.int32, (ts, r_pad), 1).astype(F32)
    weights = jnp.zeros((ts, r_pad), F32)
    for k in range(TOP_K):
        weights = jnp.where(col_id == route[:, k:k + 1], route[:, TOP_K + k:TOP_K + k + 1], weights)
    moe = _dot(weights.astype(BF16), yc_scr[i % 2].astype(BF16))
    y_ref[...] = _rmsnorm(x1_ref[...] + moe, gfin_ref[...])


def _combine(route, x1, g_final, ys, plan, tile0, ts, r_pad, nc):
    m, d = x1.shape
    dst, nch = plan[0], plan[1]
    return pl.pallas_call(
        functools.partial(_combine_kernel, nc=nc, tile0=tile0),
        grid_spec=pltpu.PrefetchScalarGridSpec(
            num_scalar_prefetch=2,
            grid=(m // ts,),
            in_specs=[pl.BlockSpec((ts, LANES), lambda i, *_: (i, 0)),
                      pl.BlockSpec((ts, d), lambda i, *_: (i, 0)),
                      pl.BlockSpec((1, d), lambda i, *_: (0, 0)),
                      pl.BlockSpec(memory_space=pl.ANY)],
            out_specs=pl.BlockSpec((ts, d), lambda i, *_: (i, 0)),
            scratch_shapes=[pltpu.VMEM((2, r_pad, d), F32), pltpu.SemaphoreType.DMA((2,))]),
        out_shape=jax.ShapeDtypeStruct((m, d), F32),
        compiler_params=_cparams("arbitrary"),
        name="moe_combine",
    )(dst, nch, route, x1, g_final, ys)


def _column(x_row):
    return jnp.transpose(jnp.broadcast_to(x_row, (LANES, LANES)))


def _gla_step_kernel(p_ref, s_ref, wup_ref, bup_ref, gn_ref, u_ref, so_ref):
    row = p_ref[0]
    gk8 = jnp.broadcast_to(row[:, OFF_GK:OFF_GK + LANES], (8, LANES))
    gk = _dot_hi(gk8, wup_ref[...]) + bup_ref[...]
    decay = jnp.exp(_log_sigmoid(gk[0:1, :]) * (1.0 / GLA_GATE_NORMALIZER))
    kd, vd = GLA_HEAD_K, GLA_HEAD_V
    wide = lambda c: jnp.concatenate([c, c], axis=1)
    for h in range(GLA_HEADS):
        q = row[:, OFF_QG + h * kd:OFF_QG + (h + 1) * kd] * (kd ** -0.5)
        k = row[:, OFF_KG + h * kd:OFF_KG + (h + 1) * kd]
        v = row[:, OFF_VG + h * vd:OFF_VG + (h + 1) * vd]
        r = row[:, OFF_RG + h * vd:OFF_RG + (h + 1) * vd]
        s_new = s_ref[0, h] * wide(_column(decay[:, h * kd:(h + 1) * kd])) + wide(_column(k)) * v
        so_ref[0, h] = s_new
        o = jnp.sum(wide(_column(q)) * s_new, axis=0, keepdims=True)
        u_ref[0, :, h * vd:(h + 1) * vd] = _rmsnorm(o, gn_ref[...]) * (r * jax.nn.sigmoid(r))


def _gla_sample(proj_s3, state, w_up, b_up, g_norm):
    bd = proj_s3.shape[0]
    full = lambda a: pl.BlockSpec(a.shape, lambda b: (0,) * a.ndim)
    st_spec = pl.BlockSpec((1, GLA_HEADS, GLA_HEAD_K, GLA_HEAD_V), lambda b: (b, 0, 0, 0))
    return pl.pallas_call(
        _gla_step_kernel,
        grid=(bd,),
        in_specs=[pl.BlockSpec((1, 1, PROJ_W), lambda b: (b, 0, 0)), st_spec,
                  full(w_up), full(b_up), full(g_norm)],
        out_specs=[pl.BlockSpec((1, 1, GLA_HEADS * GLA_HEAD_V), lambda b: (b, 0, 0)), st_spec],
        out_shape=[jax.ShapeDtypeStruct((bd, 1, GLA_HEADS * GLA_HEAD_V), F32),
                   jax.ShapeDtypeStruct(state.shape, state.dtype)],
        compiler_params=_cparams("parallel"),
        name="gla_sample",
    )(proj_s3, state, w_up, b_up, g_norm)


def _attn_step_kernel(p_ref, c1_ref, c2_ref, c3_ref, o_ref, l_ref):
    row = p_ref[0]
    gw, e = ATTN_GROUP_DIM, ATTN_HEAD_DIM
    for g, c_ref in enumerate((c1_ref, c2_ref, c3_ref)):
        window, dilation = ATTN_GROUPS[g]
        blk = window // dilation
        back = (dilation * (blk - lax.broadcasted_iota(jnp.int32, (blk, 1), 0))).astype(F32)
        for h in range(ATTN_HEADS):
            lo = OFF_QKV[g] + h * e
            q = row[:, lo:lo + e]
            k_new = row[:, lo + gw:lo + gw + e]
            v_new = row[:, lo + 2 * gw:lo + 2 * gw + e]
            k_c = c_ref[0, :, 0, h, :]
            v_c = c_ref[0, :, 1, h, :]
            s = jnp.sum(k_c * q, axis=-1, keepdims=True) * (e ** -0.5) - _alibi_slope(g, h) * back
            s0 = jnp.sum(k_new * q, axis=-1, keepdims=True) * (e ** -0.5)
            m = jnp.maximum(jnp.max(s, axis=0, keepdims=True), s0)
            p = jnp.exp(s - m)
            p0 = jnp.exp(s0 - m)
            den = jnp.sum(p, axis=0, keepdims=True) + p0
            o = (jnp.sum(p * v_c, axis=0, keepdims=True) + p0 * v_new) / den
            cols = slice(g * gw + h * e, g * gw + (h + 1) * e)
            o_ref[0, :, cols] = o
            l_ref[0, :, cols] = jnp.broadcast_to(m + jnp.log(den), (1, e))


def _attn_sample(proj_s3, caches):
    bd = proj_s3.shape[0]
    gw = ATTN_GROUP_DIM
    views, specs = [], []
    for g, cache in enumerate(caches):
        window, dilation = ATTN_GROUPS[g]
        n_buf = cache.shape[1]
        assert n_buf == window, "cache must hold exactly one window of rows"
        blk = n_buf // dilation
        views.append(cache.reshape(bd, blk, dilation, 2, ATTN_HEADS, ATTN_HEAD_DIM))
        specs.append(pl.BlockSpec((1, blk, None, 2, ATTN_HEADS, ATTN_HEAD_DIM), lambda b: (b, 0, 0, 0, 0, 0)))
    out_spec = pl.BlockSpec((1, 1, N_GROUPS * gw), lambda b: (b, 0, 0))
    return pl.pallas_call(
        _attn_step_kernel,
        grid=(bd,),
        in_specs=[pl.BlockSpec((1, 1, PROJ_W), lambda b: (b, 0, 0))] + specs,
        out_specs=[out_spec, out_spec],
        out_shape=[jax.ShapeDtypeStruct((bd, 1, N_GROUPS * gw), F32),
                   jax.ShapeDtypeStruct((bd, 1, N_GROUPS * gw), F32)],
        compiler_params=_cparams("parallel"),
        name="attn_sample",
    )(proj_s3, *views)


def _prep_w_in(w_in):
    offs, acc = [], 0
    for s in PROJ_SPLITS[:-1]:
        acc += s
        offs.append(acc)
    q_g, k_g, v_g, r_g, gk, q_a, k_a, v_a, gg, ga = jnp.split(w_in, offs, axis=1)
    gw = ATTN_GROUP_DIM
    qkv = lambda g: [a[:, g * gw:(g + 1) * gw] for a in (q_a, k_a, v_a)]
    pad = jnp.zeros((w_in.shape[0], NAT_W - OFF_GK - GLA_GATE_RANK), w_in.dtype)
    return jnp.concatenate([q_g, k_g, v_g, r_g, gg, ga, *qkv(0), gk, pad, *qkv(1), *qkv(2)], axis=1)


def _kv_rows(src, col_k, n_keep):
    b, dil, n_sub, _ = src.shape
    kv = src[:, :, n_sub - n_keep // dil:, col_k:col_k + 2 * ATTN_GROUP_DIM]
    kv = jnp.swapaxes(kv, 1, 2).astype(F32)
    return kv.reshape(b, n_keep, 2, ATTN_HEADS, ATTN_HEAD_DIM)


def kernel(x_prompt, x_sample, state_gla, cache_kv_w128, cache_kv_w512, cache_kv_w2048, g_norm_mix, w_in, w_gk_up,
           b_gk_up, g_gla_norm, w_branch_gla, w_branch_attn, w_out, g_norm_ffn, w_router, b_router, w_gate_up,
           b_gate_up, w_down, b_down, g_final):
    depth = g_norm_mix.shape[0]
    assert depth == 1, "single-layer trunk"
    batch, seq, d = x_prompt.shape
    bd, dec_seq, _ = x_sample.shape
    assert d == D_MODEL and dec_seq == 1 and seq % ATTN_GROUPS[-1][0] == 0
    n_experts = w_router.shape[-1]
    caches = (cache_kv_w128[0], cache_kv_w512[0], cache_kv_w2048[0])
    gw = ATTN_GROUP_DIM

    w_in_f32 = _prep_w_in(w_in[0])
    w_in_bf = w_in_f32.astype(BF16)
    w_up = jnp.zeros((LANES, GLA_HEADS * GLA_HEAD_K), F32).at[:GLA_GATE_RANK].set(w_gk_up[0])
    w_up_bf = w_up.astype(BF16)
    b_up = b_gk_up[0][None, :]
    g_mix = g_norm_mix[0][None, :]
    g_gla = g_gla_norm[0][None, :]
    wbg, wba, wo = w_branch_gla[0], w_branch_attn[0], w_out[0]
    g_ffn = g_norm_ffn[0][None, :]
    w_r = jnp.zeros((d, LANES), F32).at[:, :n_experts].set(w_router[0])
    wr_hi = w_r.astype(BF16)
    wr_lo = (w_r - wr_hi.astype(F32)).astype(BF16)
    b_r = jnp.zeros((1, LANES), F32).at[0, :n_experts].set(b_router[0])
    bgu, bdn = b_gate_up[0][:, None, :], b_down[0][:, None, :]
    g_fin = g_final[None, :]

    xp = x_prompt.reshape(batch * seq, d)
    nat = _norm_proj(xp, g_mix, w_in_bf[:, :NAT_W], batch, seq, 1, tm=1024, tn=1024)
    srcs = [nat] + [_norm_proj(xp, g_mix, w_in_bf[:, OFF_QKV[g]:OFF_QKV[g] + QKV_W], batch, seq, ATTN_GROUPS[g][1],
                               tm=1024, tn=QKV_W) for g in (1, 2)]
    nat2d = nat.reshape(batch * seq, NAT_W)
    u_p, st_p = _gla_prompt(nat2d, w_up_bf, b_up, g_gla, batch, seq, tc=512)
    outs, lses = [], []
    for g in range(N_GROUPS):
        col0 = OFF_QKV0 if g == 0 else 0
        o, lse = _attn_prompt(srcs[g], col0, g, tq=min(256, seq // ATTN_GROUPS[g][1]))
        outs.append(o)
        lses.append(lse)
    x1_p, h2_p, route_p, cpad_p = _merge(xp, u_p, nat2d, outs, lses, wbg.astype(BF16), wba.astype(BF16), wo.astype(BF16),
                                 g_ffn, wr_hi, wr_lo, b_r, n_experts, seq, tm=512, ts=MOE_TOKEN_TILE, precise=False)

    xs = x_sample.reshape(bd, d)
    proj_s = _norm_proj_precise(xs, g_mix, w_in_f32, tn=1024)
    proj_s3 = proj_s.reshape(bd, 1, PROJ_W)
    u_s, st_s = _gla_sample(proj_s3, state_gla[0], w_up, b_up, g_gla)
    o_s, l_s = _attn_sample(proj_s3, caches)
    outs_s = [o_s[:, 0, g * gw:(g + 1) * gw].reshape(1, 1, bd, gw) for g in range(N_GROUPS)]
    lses_s = [l_s[:, 0, g * gw:(g + 1) * gw].reshape(1, 1, bd, gw) for g in range(N_GROUPS)]
    x1_s, h2_s, route_s, cpad_s = _merge(xs, u_s.reshape(bd, d), proj_s, outs_s, lses_s, wbg, wba, wo, g_ffn,
                                 w_r, w_r, b_r, n_experts, bd, tm=bd, ts=bd, precise=True)

    ts, tm_e = MOE_TOKEN_TILE, MOE_EXPERT_TILE
    assert bd <= ts
    n_tiles_p = batch * seq // ts
    rows = _tile_rows(ts, n_experts)
    r_pad, nc = _round_up(rows, LANES), rows // ROW_ALIGN
    n_tiles_e = -(-(n_tiles_p * rows + _tile_rows(bd, n_experts) + n_experts * (tm_e - 1)) // tm_e)
    cpad = jnp.concatenate([cpad_p[::8, :n_experts], cpad_s[::8, :n_experts]], axis=0).astype(jnp.int32)
    plan = _moe_plan(cpad, nc, n_experts, tm_e, n_tiles_e)
    pad_rows = lambda a, fill: jnp.concatenate([a, jnp.full((ts - bd, a.shape[1]), fill, a.dtype)], axis=0)
    h2_s, x1_s = pad_rows(h2_s, 0), pad_rows(x1_s, 0)
    route_s = pad_rows(route_s, -1.0)
    xs = _compact(h2_p, route_p, h2_s, route_s, plan, ts, r_pad, nc, n_tiles_e, tm_e)
    new_rows = [proj_s[:, OFF_QKV[g] + gw:OFF_QKV[g] + 3 * gw].reshape(bd, 1, 2, ATTN_HEADS, ATTN_HEAD_DIM)
                .astype(caches[g].dtype) for g in range(N_GROUPS)]
    ys, kv_s = _experts(xs, plan[2], plan[3], w_gate_up[0], bgu, w_down[0], bdn, tm_e, caches, new_rows)
    kv_s = [c[None] for c in kv_s]
    y_p = _combine(route_p, x1_p, g_fin, ys, plan, 0, ts, r_pad, nc)
    y_s = _combine(route_s, x1_s, g_fin, ys, plan, n_tiles_p, ts, r_pad, nc)[:bd]

    kv_p = [_kv_rows(srcs[g], (OFF_QKV0 if g == 0 else 0) + gw, min(ATTN_GROUPS[g][0], seq))[None]
            for g in range(N_GROUPS)]
    return (y_p.reshape(batch, seq, d), y_s.reshape(bd, 1, d),
            jnp.swapaxes(st_p, -1, -2)[None], kv_p[0], kv_p[1], kv_p[2],
            st_s[None], kv_s[0], kv_s[1], kv_s[2])
```

```python
import functools

import jax
import jax.numpy as jnp
from jax import lax
from jax.experimental import pallas as pl
from jax.experimental.pallas import tpu as pltpu

F32 = jnp.float32
BF16 = jnp.bfloat16
HIGHEST = lax.Precision.HIGHEST

D_MODEL = 1024
NORM_EPS = 1e-5
GLA_HEADS = 4
GLA_HEAD_K = 128
GLA_HEAD_V = 256
GLA_GATE_RANK = 16
GLA_GATE_NORMALIZER = 16.0
GLA_CHUNK = 64
ATTN_GROUPS = ((128, 1), (512, 4), (2048, 16))
N_GROUPS = 3
ATTN_HEADS = 4
ATTN_HEAD_DIM = 128
ATTN_GROUP_DIM = ATTN_HEADS * ATTN_HEAD_DIM
TOP_K = 4
SWIGLU_LIMIT = 7.0
SWIGLU_ALPHA = 1.702
NEG_BIG = -1e30
LANES = 128
ROW_ALIGN = 8
MOE_TOKEN_TILE = 256
MOE_EXPERT_TILE = 512

OFF_QG, OFF_KG, OFF_VG, OFF_RG = 0, 512, 1024, 2048
OFF_GATE_GLA, OFF_GATE_ATTN = 3072, 4096
OFF_QKV0 = 5120
OFF_GK = 6656
NAT_W = 7168
QKV_W = 3 * ATTN_GROUP_DIM
PROJ_W = NAT_W + 2 * QKV_W
OFF_QKV = (OFF_QKV0, NAT_W, NAT_W + QKV_W)
PROJ_SPLITS = (512, 512, 1024, 1024, GLA_GATE_RANK, 1536, 1536, 1536, 1024, 1024)

VMEM_LIMIT = 48 * 1024 * 1024


def _cparams(*sem):
    return pltpu.CompilerParams(dimension_semantics=sem, vmem_limit_bytes=VMEM_LIMIT)


def _alibi_slope(group, head):
    n = N_GROUPS * ATTN_HEADS
    return 2.0 ** (-8.0 * (group * ATTN_HEADS + head + 1) / n)


def _log_sigmoid(x):
    return jnp.minimum(x, 0.0) - jnp.log1p(jnp.exp(-jnp.abs(x)))


def _dot_nt(a, b):
    return lax.dot_general(a, b, (((1,), (1,)), ((), ())), preferred_element_type=F32)


def _dot(a, b):
    return jnp.dot(a, b, preferred_element_type=F32)


def _dot_hi(a, b):
    return jnp.dot(a, b, preferred_element_type=F32, precision=HIGHEST)


def _split3(x):
    hi = x.astype(BF16)
    r1 = x - hi.astype(F32)
    mid = r1.astype(BF16)
    lo = (r1 - mid.astype(F32)).astype(BF16)
    return hi, mid, lo


def _rmsnorm(x, g):
    return x * lax.rsqrt(jnp.mean(x * x, axis=-1, keepdims=True) + NORM_EPS) * g


def _proj_kernel(x_ref, g_ref, w_ref, o_ref, h_ref, *scr, dilation):
    @pl.when(pl.program_id(1) == 0)
    def _():
        h_ref[...] = _rmsnorm(x_ref[...], g_ref[...]).astype(BF16)

    acc = _dot(h_ref[...], w_ref[...])
    if dilation == 1:
        o_ref[0, 0] = acc.astype(o_ref.dtype)
    else:
        n = acc.shape[0] // dilation
        for c in range(acc.shape[1] // LANES):
            cols = slice(c * LANES, (c + 1) * LANES)
            scr[0][c] = acc[:, cols]
            for r in range(dilation):
                o_ref[0, r, :, cols] = scr[0][c, pl.ds(r, n, stride=dilation), :].astype(o_ref.dtype)


def _norm_proj(x2d, g_row, w_bf, batch, seq, dilation, tm, tn):
    m, d = x2d.shape
    n = w_bf.shape[1]
    tps = seq // tm
    scratch = [pltpu.VMEM((tm, d), BF16)]
    if dilation > 1:
        scratch.append(pltpu.VMEM((tn // LANES, tm, LANES), F32))
    return pl.pallas_call(
        functools.partial(_proj_kernel, dilation=dilation),
        grid=(m // tm, n // tn),
        in_specs=[pl.BlockSpec((tm, d), lambda i, j: (i, 0)),
                  pl.BlockSpec((1, d), lambda i, j: (0, 0)),
                  pl.BlockSpec((d, tn), lambda i, j: (0, j))],
        out_specs=pl.BlockSpec((1, dilation, tm // dilation, tn), lambda i, j: (i // tps, 0, i % tps, j)),
        out_shape=jax.ShapeDtypeStruct((batch, dilation, seq // dilation, n), BF16),
        scratch_shapes=scratch,
        compiler_params=_cparams("parallel", "arbitrary"),
        name=f"norm_proj_d{dilation}",
    )(x2d, g_row, w_bf)


def _proj_precise_kernel(x_ref, g_ref, w_ref, o_ref):
    o_ref[...] = _dot_hi(_rmsnorm(x_ref[...], g_ref[...]), w_ref[...])


def _norm_proj_precise(x2d, g_row, w_f32, tn):
    m, d = x2d.shape
    n = w_f32.shape[1]
    return pl.pallas_call(
        _proj_precise_kernel,
        grid=(n // tn,),
        in_specs=[pl.BlockSpec((m, d), lambda j: (0, 0)),
                  pl.BlockSpec((1, d), lambda j: (0, 0)),
                  pl.BlockSpec((d, tn), lambda j: (0, j))],
        out_specs=pl.BlockSpec((m, tn), lambda j: (0, j)),
        out_shape=jax.ShapeDtypeStruct((m, n), F32),
        compiler_params=_cparams("parallel"),
        name="norm_proj_sample",
    )(x2d, g_row, w_f32)


def _gla_kernel(q_ref, k_ref, v_ref, r_ref, gk_ref, wup_ref, bup_ref, gn_ref,
                u_ref, st_ref, s_scr, o_scr, *, n_chunks):
    t = pl.program_id(1)
    kd, vd = GLA_HEAD_K, GLA_HEAD_V

    @pl.when(t == 0)
    def _():
        s_scr[...] = jnp.zeros_like(s_scr)

    gk = _dot(gk_ref[...], wup_ref[...]) + bup_ref[...]
    log_a = _log_sigmoid(gk) * (1.0 / GLA_GATE_NORMALIZER)
    c_sz = GLA_CHUNK
    row = lax.broadcasted_iota(jnp.int32, (c_sz, c_sz), 0)
    col = lax.broadcasted_iota(jnp.int32, (c_sz, c_sz), 1)
    tri = row >= col
    tri_bf = jnp.where(tri, 1.0, 0.0).astype(BF16)
    scale = kd ** -0.5

    for c in range(n_chunks):
        sl = slice(c * c_sz, (c + 1) * c_sz)
        g_hi, g_mid, g_lo = _split3(log_a[sl])
        b_all = _dot(tri_bf, g_hi) + _dot(tri_bf, g_mid) + _dot(tri_bf, g_lo)
        for h in range(GLA_HEADS):
            kc, vc = slice(h * kd, (h + 1) * kd), slice(h * vd, (h + 1) * vd)
            b = b_all[:, kc]
            b_last = b[c_sz - 1:c_sz, :]
            q = q_ref[sl, kc].astype(F32) * scale
            k = k_ref[sl, kc].astype(F32)
            v_bf = v_ref[sl, vc]
            q_in = (q * jnp.exp(b)).astype(BF16)
            k_in = (k * jnp.exp(-b)).astype(BF16)
            k_out = (k * jnp.exp(b_last - b)).astype(BF16)
            a = jnp.where(tri, _dot_nt(q_in, k_in), 0.0)
            st = s_scr[h]
            o_scr[sl, vc] = _dot_nt(q_in, st.astype(BF16)) + _dot(a.astype(BF16), v_bf)
            s_scr[h] = st * jnp.exp(b_last) + pl.dot(v_bf, k_out, trans_a=True)

    for h in range(GLA_HEADS):
        vc = slice(h * vd, (h + 1) * vd)
        r = r_ref[:, vc].astype(F32)
        u_ref[:, vc] = (_rmsnorm(o_scr[:, vc], gn_ref[...]) * (r * jax.nn.sigmoid(r))).astype(u_ref.dtype)

    @pl.when(t == pl.num_programs(1) - 1)
    def _():
        st_ref[0] = s_scr[...]


def _gla_prompt(proj, w_up_bf, b_up, g_norm, batch, seq, tc):
    nt = seq // tc
    kw, vw = GLA_HEADS * GLA_HEAD_K, GLA_HEADS * GLA_HEAD_V
    rows = lambda b, t: b * nt + t
    kern = functools.partial(_gla_kernel, n_chunks=tc // GLA_CHUNK)
    full = lambda a: pl.BlockSpec(a.shape, lambda b, t: (0, 0))
    return pl.pallas_call(
        kern,
        grid=(batch, nt),
        in_specs=[
            pl.BlockSpec((tc, kw), lambda b, t: (rows(b, t), OFF_QG // kw)),
            pl.BlockSpec((tc, kw), lambda b, t: (rows(b, t), OFF_KG // kw)),
            pl.BlockSpec((tc, vw), lambda b, t: (rows(b, t), OFF_VG // vw)),
            pl.BlockSpec((tc, vw), lambda b, t: (rows(b, t), OFF_RG // vw)),
            pl.BlockSpec((tc, LANES), lambda b, t: (rows(b, t), OFF_GK // LANES)),
            full(w_up_bf), full(b_up), full(g_norm),
        ],
        out_specs=[
            pl.BlockSpec((tc, vw), lambda b, t: (rows(b, t), 0)),
            pl.BlockSpec((1, GLA_HEADS, GLA_HEAD_V, GLA_HEAD_K), lambda b, t: (b, 0, 0, 0)),
        ],
        out_shape=[jax.ShapeDtypeStruct((batch * seq, vw), BF16),
                   jax.ShapeDtypeStruct((batch, GLA_HEADS, GLA_HEAD_V, GLA_HEAD_K), F32)],
        scratch_shapes=[pltpu.VMEM((GLA_HEADS, GLA_HEAD_V, GLA_HEAD_K), F32), pltpu.VMEM((tc, vw), F32)],
        compiler_params=_cparams("parallel", "arbitrary"),
        name="gla_prompt",
    )(proj, proj, proj, proj, proj, w_up_bf, b_up, g_norm)


def _attn_kernel(q_ref, kc_ref, kp_ref, vc_ref, vp_ref, o_ref, l_ref, *, group, dilation, blk, n_qb):
    n = pl.program_id(2)
    e = ATTN_HEAD_DIM
    i_idx = lax.broadcasted_iota(jnp.int32, (blk, 2 * blk), 0)
    j_idx = lax.broadcasted_iota(jnp.int32, (blk, 2 * blk), 1)
    rel = i_idx + blk - j_idx
    band = (rel >= 0) & (rel <= blk)
    dist = (dilation * rel).astype(F32)
    for qb in range(n_qb):
        rows = slice(qb * blk, (qb + 1) * blk)
        q = q_ref[0, 0, rows, :]
        if qb == 0:
            k_prev, v_prev = kp_ref[0, 0], vp_ref[0, 0]
            valid = band & ((j_idx >= blk) | (n > 0))
        else:
            prev = slice((qb - 1) * blk, qb * blk)
            k_prev, v_prev = kc_ref[0, 0, prev, :], vc_ref[0, 0, prev, :]
            valid = band
        kk = jnp.concatenate([k_prev, kc_ref[0, 0, rows, :]], axis=0)
        vv = jnp.concatenate([v_prev, vc_ref[0, 0, rows, :]], axis=0)
        for h in range(ATTN_HEADS):
            cols = slice(h * e, (h + 1) * e)
            s = _dot_nt(q[:, cols], kk[:, cols]) * (e ** -0.5) - _alibi_slope(group, h) * dist
            s = jnp.where(valid, s, NEG_BIG)
            m = jnp.max(s, axis=-1, keepdims=True)
            p = jnp.exp(s - m)
            den = jnp.sum(p, axis=-1, keepdims=True)
            o = _dot(p.astype(BF16), vv[:, cols]) / den
            o_ref[0, 0, rows, cols] = o.astype(o_ref.dtype)
            l_ref[0, 0, rows, cols] = jnp.broadcast_to(m + jnp.log(den), (blk, e))


def _attn_prompt(src, col0, group, tq):
    batch, dilation, n_sub, _ = src.shape
    window, dil = ATTN_GROUPS[group]
    assert dil == dilation
    blk = window // dilation
    n_qb = tq // blk
    gw = ATTN_GROUP_DIM
    qc, kc, vc = col0 // gw, col0 // gw + 1, col0 // gw + 2
    cur = lambda c: pl.BlockSpec((1, 1, tq, gw), lambda b, r, n: (b, r, n, c))
    prev = lambda c: pl.BlockSpec((1, 1, blk, gw), lambda b, r, n: (b, r, jnp.maximum(n * n_qb - 1, 0), c))
    kern = functools.partial(_attn_kernel, group=group, dilation=dilation, blk=blk, n_qb=n_qb)
    out_spec = pl.BlockSpec((1, 1, tq, gw), lambda b, r, n: (b, r, n, 0))
    return pl.pallas_call(
        kern,
        grid=(batch, dilation, n_sub // tq),
        in_specs=[cur(qc), cur(kc), prev(kc), cur(vc), prev(vc)],
        out_specs=[out_spec, out_spec],
        out_shape=[jax.ShapeDtypeStruct((batch, dilation, n_sub, gw), BF16),
                   jax.ShapeDtypeStruct((batch, dilation, n_sub, gw), F32)],
        compiler_params=_cparams("parallel", "parallel", "arbitrary"),
        name=f"attn_prompt_g{group}",
    )(src, src, src, src, src)


def _merge_kernel(x_ref, u_ref, gg_ref, ga_ref, o1_ref, o2_ref, o3_ref, l1_ref, l2_ref, l3_ref,
                  wbg_ref, wba_ref, wo_ref, gf_ref, wr_hi_ref, wr_lo_ref, br_ref,
                  x1_ref, h2_ref, route_ref, cpad_ref, *scr, n_experts, dilations, precise, ts):
    tm = x_ref.shape[0]
    scr = list(scr)

    def natural(ref, dilation):
        if dilation == 1:
            return ref[0, 0].astype(F32)
        buf = scr.pop(0)
        n = tm // dilation
        for c in range(buf.shape[0]):
            for r in range(dilation):
                buf[c, pl.ds(r, n, stride=dilation), :] = ref[0, r, :, c * LANES:(c + 1) * LANES].astype(F32)
        return jnp.concatenate([buf[c] for c in range(buf.shape[0])], axis=1)

    o1, o2, o3 = (natural(r, d) for r, d in zip((o1_ref, o2_ref, o3_ref), dilations))
    l1, l2, l3 = (natural(r, d) for r, d in zip((l1_ref, l2_ref, l3_ref), dilations))
    mm = _dot_hi if precise else (lambda a, b: _dot(a.astype(BF16), b))
    lm = jnp.maximum(jnp.maximum(l1, l2), l3)
    e1, e2, e3 = jnp.exp(l1 - lm), jnp.exp(l2 - lm), jnp.exp(l3 - lm)
    o_att = (e1 * o1 + e2 * o2 + e3 * o3) / (e1 + e2 + e3)
    merged = (jax.nn.sigmoid(gg_ref[...].astype(F32)) * mm(u_ref[...], wbg_ref[...])
              + jax.nn.sigmoid(ga_ref[...].astype(F32)) * mm(o_att, wba_ref[...]))
    x1 = x_ref[...] + mm(merged, wo_ref[...])
    x1_ref[...] = x1
    h2 = _rmsnorm(x1, gf_ref[...])
    h2_ref[...] = h2.astype(BF16)

    if precise:
        logits = _dot_hi(h2, wr_hi_ref[...]) + br_ref[...]
    else:
        h_hi, h_mid, _ = _split3(h2)
        logits = (_dot(h_hi, wr_hi_ref[...]) + _dot(h_mid, wr_hi_ref[...]) + _dot(h_hi, wr_lo_ref[...])) + br_ref[...]
    lane = lax.broadcasted_iota(jnp.int32, (ts, LANES), 1).astype(F32)
    ri = lax.broadcasted_iota(jnp.int32, (ts, ts), 0)
    ci = lax.broadcasted_iota(jnp.int32, (ts, ts), 1)
    earlier = jnp.where(ci < ri, 1.0, 0.0).astype(BF16)
    li = lax.broadcasted_iota(jnp.int32, (LANES, LANES), 0)
    lj = lax.broadcasted_iota(jnp.int32, (LANES, LANES), 1)
    before = jnp.where(li < lj, 1.0, 0.0).astype(BF16)
    for s in range(tm // ts):
        rows = slice(s * ts, (s + 1) * ts)
        cur = jnp.where(lane < n_experts, logits[rows, :], -jnp.inf)
        tops, sels = [], []
        for _ in range(TOP_K):
            m = jnp.max(cur, axis=-1, keepdims=True)
            idx = jnp.min(jnp.where(cur == m, lane, float(LANES)), axis=-1, keepdims=True)
            sel = lane == idx
            tops.append(m)
            sels.append(sel)
            cur = jnp.where(sel, -jnp.inf, cur)
        ex = [jnp.exp(m - tops[0]) for m in tops]
        den = ex[0] + ex[1] + ex[2] + ex[3]
        mem = jnp.zeros((ts, LANES), F32)
        for sel in sels:
            mem = jnp.where(sel, 1.0, mem)
        rank = _dot(earlier, mem.astype(BF16))
        count = jnp.sum(mem, axis=0, keepdims=True)
        cpad = jnp.ceil(count * (1.0 / ROW_ALIGN)) * ROW_ALIGN
        cpad8 = jnp.broadcast_to(cpad, (8, LANES))
        seg_start = _dot(cpad8.astype(BF16), before)[0:1]
        pos = seg_start + rank
        route = jnp.zeros((ts, LANES), F32)
        for k in range(TOP_K):
            dest = jnp.sum(jnp.where(sels[k], pos, 0.0), axis=-1, keepdims=True)
            route = jnp.where(lane == float(k), dest, route)
            route = jnp.where(lane == float(TOP_K + k), ex[k] / den, route)
        route_ref[rows, :] = route
        cpad_ref[s * 8:(s + 1) * 8, :] = cpad8


def _merge(x2d, u, gate_src, outs, lses, wbg, wba, wo, g_ffn, wr_hi, wr_lo, b_r, n_experts, seq, tm, ts, precise):
    m, d = x2d.shape
    gw = ATTN_GROUP_DIM
    tps = seq // tm
    dilations = tuple(o.shape[1] for o in outs)
    row = lambda w: pl.BlockSpec((tm, w), lambda i: (i, 0))
    full = lambda a: pl.BlockSpec(a.shape, lambda i: (0, 0))
    grp = lambda dil: pl.BlockSpec((1, dil, tm // dil, gw), lambda i: (i // tps, 0, i % tps, 0))
    kern = functools.partial(_merge_kernel, n_experts=n_experts, dilations=dilations, precise=precise, ts=ts)
    n_scr = 2 * sum(1 for dil in dilations if dil > 1)
    return pl.pallas_call(
        kern,
        grid=(m // tm,),
        in_specs=[row(d), row(d),
                  pl.BlockSpec((tm, d), lambda i: (i, OFF_GATE_GLA // d)),
                  pl.BlockSpec((tm, d), lambda i: (i, OFF_GATE_ATTN // d)),
                  *[grp(dil) for dil in dilations], *[grp(dil) for dil in dilations],
                  full(wbg), full(wba), full(wo), full(g_ffn), full(wr_hi), full(wr_lo), full(b_r)],
        out_specs=[row(d), row(d), row(LANES), pl.BlockSpec((tm // ts * 8, LANES), lambda i: (i, 0))],
        out_shape=[jax.ShapeDtypeStruct((m, d), F32), jax.ShapeDtypeStruct((m, d), BF16),
                   jax.ShapeDtypeStruct((m, LANES), F32), jax.ShapeDtypeStruct((m // ts * 8, LANES), F32)],
        scratch_shapes=[pltpu.VMEM((gw // LANES, tm, LANES), F32)] * n_scr,
        compiler_params=_cparams("parallel"),
        name="merge_router_sample" if precise else "merge_router",
    )(x2d, u, gate_src, gate_src, *outs, *lses, wbg, wba, wo, g_ffn, wr_hi, wr_lo, b_r)


def _round_up(x, mult):
    return (x + mult - 1) // mult * mult


def _tile_rows(n_tokens, n_experts):
    return _round_up(n_tokens * TOP_K + n_experts * (ROW_ALIGN - 1), ROW_ALIGN)


def _moe_plan(cpad, nc, n_experts, tm_e, n_tiles_max):
    tot = jnp.sum(cpad, axis=0)
    gsize = (tot + tm_e - 1) // tm_e * tm_e
    g_end = jnp.cumsum(gsize)
    g_start = g_end - gsize
    n_used = (g_end[-1] // tm_e).astype(jnp.int32)
    o_start = jnp.cumsum(cpad, axis=0) - cpad
    seg_end = jnp.cumsum(cpad, axis=1)
    seg_start = seg_end - cpad

    row0 = jnp.arange(nc, dtype=jnp.int32) * ROW_ALIGN
    e_of = jnp.sum((seg_end[:, None, :] <= row0[None, :, None]).astype(jnp.int32), axis=-1)
    e_of = jnp.minimum(e_of, n_experts - 1)
    pick = (e_of[:, :, None] == jnp.arange(n_experts)[None, None, :]).astype(jnp.int32)
    base = g_start[None, :] + o_start - seg_start
    dst = jnp.sum(pick * base[:, None, :], axis=-1) + row0[None, :]
    nch = (seg_end[:, -1] // ROW_ALIGN).astype(jnp.int32)
    dst = jnp.where(jnp.arange(nc)[None, :] < nch[:, None], dst, 0).astype(jnp.int32)

    t0 = jnp.arange(n_tiles_max, dtype=jnp.int32) * tm_e
    tile_expert = jnp.minimum(jnp.sum((g_end[None, :] <= t0[:, None]).astype(jnp.int32), axis=-1), n_experts - 1)
    tile_id = jnp.arange(n_tiles_max)
    last = jnp.sum(jnp.where(tile_id == n_used - 1, tile_expert, 0))
    tile_expert = jnp.where(tile_id < n_used, tile_expert, last).astype(jnp.int32)
    gap_start = (g_start + tot).astype(jnp.int32)
    gap_chunks = ((gsize - tot) // ROW_ALIGN).astype(jnp.int32)
    return dst.reshape(-1), nch, tile_expert, n_used.reshape(1), gap_start, gap_chunks


def _loop(n, fn):
    lax.fori_loop(0, n, lambda i, carry: (fn(i), carry)[1], 0)


def _compact_kernel(dst_ref, nch_ref, gap_start_ref, gap_chunks_ref, nu_ref, hp_ref, rp_ref, hs_ref, rs_ref,
                    xs_ref, xc_scr, zero_scr, sem, *, nc):
    s = pl.program_id(0)
    last = pl.num_programs(0) - 1
    is_sample = s == last
    ts, r_pad = hp_ref.shape[0], xc_scr.shape[1]

    def chunk(tile, c):
        slot = tile % 2
        src = xc_scr.at[slot, pl.ds(pl.multiple_of(c * ROW_ALIGN, ROW_ALIGN), ROW_ALIGN)]
        dst = xs_ref.at[pl.ds(pl.multiple_of(dst_ref[tile * nc + c], ROW_ALIGN), ROW_ALIGN)]
        return pltpu.make_async_copy(src, dst, sem.at[slot])

    def drain(tile):
        slot = tile % 2
        one = pltpu.make_async_copy(xc_scr.at[slot, pl.ds(0, ROW_ALIGN)], xs_ref.at[pl.ds(0, ROW_ALIGN)], sem.at[slot])
        _loop(nch_ref[tile], lambda c: one.wait())

    @pl.when(s >= 2)
    def _():
        drain(s - 2)

    h = jnp.where(is_sample, hs_ref[...], hp_ref[...])
    route = jnp.where(is_sample, rs_ref[...], rp_ref[...])
    dest_t = jnp.transpose(route)
    row_id = lax.broadcasted_iota(jnp.int32, (r_pad, ts), 0).astype(F32)
    perm = jnp.zeros((r_pad, ts), F32)
    for k in range(TOP_K):
        perm = jnp.where(row_id == dest_t[k:k + 1, :], 1.0, perm)
    xc_scr[s % 2] = _dot(perm.astype(BF16), h)
    _loop(nch_ref[s], lambda c: chunk(s, c).start())

    @pl.when(is_sample)
    def _():
        @pl.when(s >= 1)
        def _():
            drain(s - 1)
        drain(s)

        zero_scr[...] = jnp.zeros_like(zero_scr)
        tile = zero_scr.shape[0]

        def gap(e, j):
            row = pl.multiple_of(gap_start_ref[e] + j * ROW_ALIGN, ROW_ALIGN)
            return pltpu.make_async_copy(zero_scr.at[pl.ds(0, ROW_ALIGN)], xs_ref.at[pl.ds(row, ROW_ALIGN)], sem.at[0])

        def tail(t):
            row = pl.multiple_of(t * tile, tile)
            return pltpu.make_async_copy(zero_scr, xs_ref.at[pl.ds(row, tile)], sem.at[0])

        n_exp = gap_start_ref.shape[0]
        n_tail = xs_ref.shape[0] // tile - nu_ref[0]
        _loop(n_exp, lambda e: _loop(gap_chunks_ref[e], lambda j: gap(e, j).start()))
        _loop(n_tail, lambda t: tail(nu_ref[0] + t).start())
        _loop(n_exp, lambda e: _loop(gap_chunks_ref[e], lambda j: gap(e, j).wait()))
        _loop(n_tail, lambda t: tail(nu_ref[0] + t).wait())


def _compact(h2_p, route_p, h2_s, route_s, plan, ts, r_pad, nc, n_tiles_e, tm_e):
    m, d = h2_p.shape
    n_p = m // ts
    dst, nch, _, n_used, gap_start, gap_chunks = plan
    prompt = lambda w: pl.BlockSpec((ts, w), lambda i, *_: (jnp.minimum(i, n_p - 1), 0))
    sample = lambda w: pl.BlockSpec((ts, w), lambda i, *_: (0, 0))
    return pl.pallas_call(
        functools.partial(_compact_kernel, nc=nc),
        grid_spec=pltpu.PrefetchScalarGridSpec(
            num_scalar_prefetch=5,
            grid=(n_p + 1,),
            in_specs=[prompt(d), prompt(LANES), sample(d), sample(LANES)],
            out_specs=pl.BlockSpec(memory_space=pl.ANY),
            scratch_shapes=[pltpu.VMEM((2, r_pad, d), F32), pltpu.VMEM((tm_e, d), F32),
                            pltpu.SemaphoreType.DMA((2,))]),
        out_shape=jax.ShapeDtypeStruct((n_tiles_e * tm_e, d), F32),
        compiler_params=_cparams("arbitrary"),
        name="moe_compact",
    )(dst, nch, gap_start, gap_chunks, n_used, h2_p, route_p, h2_s, route_s)


def _expert_kernel(te_ref, nu_ref, x_ref, wgu_ref, bgu_ref, wd_ref, bd_ref, o_ref, wgu_bf, wd_bf, *, d_ff, sub):
    t = pl.program_id(0)
    used = t < nu_ref[0]

    @pl.when(used)
    def _():
        @pl.when((t == 0) | (te_ref[t] != te_ref[jnp.maximum(t - 1, 0)]))
        def _():
            step = 128
            for i in range(wgu_bf.shape[0] // step):
                rows = slice(i * step, (i + 1) * step)
                wgu_bf[rows, :] = wgu_ref[0, rows, :].astype(BF16)
            for i in range(wd_bf.shape[0] // step):
                rows = slice(i * step, (i + 1) * step)
                wd_bf[rows, :] = wd_ref[0, rows, :].astype(BF16)

        for s in range(x_ref.shape[0] // sub):
            rows = slice(s * sub, (s + 1) * sub)
            gu = _dot(x_ref[rows, :].astype(BF16), wgu_bf[...]) + bgu_ref[0]
            gate = jnp.minimum(gu[:, :d_ff], SWIGLU_LIMIT)
            up = jnp.clip(gu[:, d_ff:], -SWIGLU_LIMIT, SWIGLU_LIMIT)
            act = (up + 1.0) * gate * jax.nn.sigmoid(SWIGLU_ALPHA * gate)
            o_ref[rows, :] = _dot(act.astype(BF16), wd_bf[...]) + bd_ref[0]

    @pl.when(jnp.logical_not(used))
    def _():
        o_ref[...] = jnp.zeros_like(o_ref)


def _experts(xs, tile_expert, n_used, wgu, bgu, wd, bd, tm_e):
    n_rows, d = xs.shape
    _, _, two_ff = wgu.shape
    d_ff = two_ff // 2
    kern = functools.partial(_expert_kernel, d_ff=d_ff, sub=tm_e)
    by_expert = lambda shape: pl.BlockSpec(shape, lambda t, te, nu: (te[t], 0, 0))
    return pl.pallas_call(
        kern,
        grid_spec=pltpu.PrefetchScalarGridSpec(
            num_scalar_prefetch=2,
            grid=(n_rows // tm_e,),
            in_specs=[pl.BlockSpec((tm_e, d), lambda t, te, nu: (jnp.minimum(t, nu[0] - 1), 0)),
                      by_expert((1, d, two_ff)), by_expert((1, 1, two_ff)),
                      by_expert((1, d_ff, d)), by_expert((1, 1, d))],
            out_specs=pl.BlockSpec((tm_e, d), lambda t, te, nu: (t, 0)),
            scratch_shapes=[pltpu.VMEM((d, two_ff), BF16), pltpu.VMEM((d_ff, d), BF16)]),
        out_shape=jax.ShapeDtypeStruct((n_rows, d), F32),
        compiler_params=pltpu.CompilerParams(dimension_semantics=("arbitrary",), vmem_limit_bytes=56 * 1024 * 1024),
        name="moe_experts",
    )(tile_expert, n_used, xs, wgu, bgu, wd, bd)


def _shift_kernel(c_ref, new_ref, o_ref):
    j = pl.program_id(1)
    last = pl.num_programs(1) - 1
    bn = o_ref.shape[1]

    @pl.when(j != last)
    def _():
        o_ref[...] = c_ref[...]

    @pl.when(j == last)
    def _():
        o_ref[0, 0:bn - 1] = c_ref[0, 1:bn]
        o_ref[0, bn - 1:bn] = new_ref[0]


def _shift_cache(cache, new_row, bn):
    bd, n = cache.shape[:2]
    tail = cache.shape[2:]
    zeros = (0,) * len(tail)
    return pl.pallas_call(
        _shift_kernel,
        grid=(bd, n // bn),
        in_specs=[pl.BlockSpec(tuple(pl.Element(s) for s in (1, bn) + tail),
                               lambda b, j: (b, jnp.minimum(j * bn + 1, n - bn)) + zeros),
                  pl.BlockSpec((1, 1) + tail, lambda b, j: (b, 0) + zeros)],
        out_specs=pl.BlockSpec((1, bn) + tail, lambda b, j: (b, j) + zeros),
        out_shape=jax.ShapeDtypeStruct(cache.shape, cache.dtype),
        compiler_params=_cparams("parallel", "arbitrary"),
        name="cache_shift",
    )(cache, new_row)


def _combine_kernel(dst_ref, nch_ref, route_ref, x1_ref, gfin_ref, ys_ref, y_ref, yc_scr, sem, *, nc, tile0):
    i = pl.program_id(0)

    def chunk(step, c):
        slot = step % 2
        tile = step + tile0
        src = ys_ref.at[pl.ds(pl.multiple_of(dst_ref[tile * nc + c], ROW_ALIGN), ROW_ALIGN)]
        dst = yc_scr.at[slot, pl.ds(pl.multiple_of(c * ROW_ALIGN, ROW_ALIGN), ROW_ALIGN)]
        return pltpu.make_async_copy(src, dst, sem.at[slot])

    def fetch(step):
        _loop(nch_ref[step + tile0], lambda c: chunk(step, c).start())

    @pl.when(i == 0)
    def _():
        yc_scr[...] = jnp.zeros_like(yc_scr)
        fetch(i)

    @pl.when(i + 1 < pl.num_programs(0))
    def _():
        fetch(i + 1)

    one = pltpu.make_async_copy(ys_ref.at[pl.ds(0, ROW_ALIGN)], yc_scr.at[i % 2, pl.ds(0, ROW_ALIGN)], sem.at[i % 2])
    _loop(nch_ref[i + tile0], lambda c: one.wait())

    ts, r_pad = x1_ref.shape[0], yc_scr.shape[1]
    route = route_ref[...]
    col_id = lax.broadcasted_iota(jnp.int32, (ts, r_pad), 1).astype(F32)
    weights = jnp.zeros((ts, r_pad), F32)
    for k in range(TOP_K):
        weights = jnp.where(col_id == route[:, k:k + 1], route[:, TOP_K + k:TOP_K + k + 1], weights)
    moe = _dot(weights.astype(BF16), yc_scr[i % 2].astype(BF16))
    y_ref[...] = _rmsnorm(x1_ref[...] + moe, gfin_ref[...])


def _combine(route, x1, g_final, ys, plan, tile0, ts, r_pad, nc):
    m, d = x1.shape
    dst, nch = plan[0], plan[1]
    return pl.pallas_call(
        functools.partial(_combine_kernel, nc=nc, tile0=tile0),
        grid_spec=pltpu.PrefetchScalarGridSpec(
            num_scalar_prefetch=2,
            grid=(m // ts,),
            in_specs=[pl.BlockSpec((ts, LANES), lambda i, *_: (i, 0)),
                      pl.BlockSpec((ts, d), lambda i, *_: (i, 0)),
                      pl.BlockSpec((1, d), lambda i, *_: (0, 0)),
                      pl.BlockSpec(memory_space=pl.ANY)],
            out_specs=pl.BlockSpec((ts, d), lambda i, *_: (i, 0)),
            scratch_shapes=[pltpu.VMEM((2, r_pad, d), F32), pltpu.SemaphoreType.DMA((2,))]),
        out_shape=jax.ShapeDtypeStruct((m, d), F32),
        compiler_params=_cparams("arbitrary"),
        name="moe_combine",
    )(dst, nch, route, x1, g_final, ys)


def _column(x_row):
    return jnp.transpose(jnp.broadcast_to(x_row, (LANES, LANES)))


def _gla_step_kernel(p_ref, s_ref, wup_ref, bup_ref, gn_ref, u_ref, so_ref):
    row = p_ref[0]
    gk8 = jnp.broadcast_to(row[:, OFF_GK:OFF_GK + LANES], (8, LANES))
    gk = _dot_hi(gk8, wup_ref[...]) + bup_ref[...]
    decay = jnp.exp(_log_sigmoid(gk[0:1, :]) * (1.0 / GLA_GATE_NORMALIZER))
    kd, vd = GLA_HEAD_K, GLA_HEAD_V
    wide = lambda c: jnp.concatenate([c, c], axis=1)
    for h in range(GLA_HEADS):
        q = row[:, OFF_QG + h * kd:OFF_QG + (h + 1) * kd] * (kd ** -0.5)
        k = row[:, OFF_KG + h * kd:OFF_KG + (h + 1) * kd]
        v = row[:, OFF_VG + h * vd:OFF_VG + (h + 1) * vd]
        r = row[:, OFF_RG + h * vd:OFF_RG + (h + 1) * vd]
        s_new = s_ref[0, h] * wide(_column(decay[:, h * kd:(h + 1) * kd])) + wide(_column(k)) * v
        so_ref[0, h] = s_new
        o = jnp.sum(wide(_column(q)) * s_new, axis=0, keepdims=True)
        u_ref[0, :, h * vd:(h + 1) * vd] = _rmsnorm(o, gn_ref[...]) * (r * jax.nn.sigmoid(r))


def _gla_sample(proj_s3, state, w_up, b_up, g_norm):
    bd = proj_s3.shape[0]
    full = lambda a: pl.BlockSpec(a.shape, lambda b: (0,) * a.ndim)
    st_spec = pl.BlockSpec((1, GLA_HEADS, GLA_HEAD_K, GLA_HEAD_V), lambda b: (b, 0, 0, 0))
    return pl.pallas_call(
        _gla_step_kernel,
        grid=(bd,),
        in_specs=[pl.BlockSpec((1, 1, PROJ_W), lambda b: (b, 0, 0)), st_spec,
                  full(w_up), full(b_up), full(g_norm)],
        out_specs=[pl.BlockSpec((1, 1, GLA_HEADS * GLA_HEAD_V), lambda b: (b, 0, 0)), st_spec],
        out_shape=[jax.ShapeDtypeStruct((bd, 1, GLA_HEADS * GLA_HEAD_V), F32),
                   jax.ShapeDtypeStruct(state.shape, state.dtype)],
        compiler_params=_cparams("parallel"),
        name="gla_sample",
    )(proj_s3, state, w_up, b_up, g_norm)


def _attn_step_kernel(p_ref, c1_ref, c2_ref, c3_ref, o_ref, l_ref):
    row = p_ref[0]
    gw, e = ATTN_GROUP_DIM, ATTN_HEAD_DIM
    for g, c_ref in enumerate((c1_ref, c2_ref, c3_ref)):
        window, dilation = ATTN_GROUPS[g]
        blk = window // dilation
        back = (dilation * (blk - lax.broadcasted_iota(jnp.int32, (blk, 1), 0))).astype(F32)
        for h in range(ATTN_HEADS):
            lo = OFF_QKV[g] + h * e
            q = row[:, lo:lo + e]
            k_new = row[:, lo + gw:lo + gw + e]
            v_new = row[:, lo + 2 * gw:lo + 2 * gw + e]
            k_c = c_ref[0, :, 0, h, :]
            v_c = c_ref[0, :, 1, h, :]
            s = jnp.sum(k_c * q, axis=-1, keepdims=True) * (e ** -0.5) - _alibi_slope(g, h) * back
            s0 = jnp.sum(k_new * q, axis=-1, keepdims=True) * (e ** -0.5)
            m = jnp.maximum(jnp.max(s, axis=0, keepdims=True), s0)
            p = jnp.exp(s - m)
            p0 = jnp.exp(s0 - m)
            den = jnp.sum(p, axis=0, keepdims=True) + p0
            o = (jnp.sum(p * v_c, axis=0, keepdims=True) + p0 * v_new) / den
            cols = slice(g * gw + h * e, g * gw + (h + 1) * e)
            o_ref[0, :, cols] = o
            l_ref[0, :, cols] = jnp.broadcast_to(m + jnp.log(den), (1, e))


def _attn_sample(proj_s3, caches):
    bd = proj_s3.shape[0]
    gw = ATTN_GROUP_DIM
    views, specs = [], []
    for g, cache in enumerate(caches):
        window, dilation = ATTN_GROUPS[g]
        n_buf = cache.shape[1]
        assert n_buf == window, "cache must hold exactly one window of rows"
        blk = n_buf // dilation
        views.append(cache.reshape(bd, blk, dilation, 2, ATTN_HEADS, ATTN_HEAD_DIM))
        specs.append(pl.BlockSpec((1, blk, None, 2, ATTN_HEADS, ATTN_HEAD_DIM), lambda b: (b, 0, 0, 0, 0, 0)))
    out_spec = pl.BlockSpec((1, 1, N_GROUPS * gw), lambda b: (b, 0, 0))
    return pl.pallas_call(
        _attn_step_kernel,
        grid=(bd,),
        in_specs=[pl.BlockSpec((1, 1, PROJ_W), lambda b: (b, 0, 0))] + specs,
        out_specs=[out_spec, out_spec],
        out_shape=[jax.ShapeDtypeStruct((bd, 1, N_GROUPS * gw), F32),
                   jax.ShapeDtypeStruct((bd, 1, N_GROUPS * gw), F32)],
        compiler_params=_cparams("parallel"),
        name="attn_sample",
    )(proj_s3, *views)


def _prep_w_in(w_in):
    offs, acc = [], 0
    for s in PROJ_SPLITS[:-1]:
        acc += s
        offs.append(acc)
    q_g, k_g, v_g, r_g, gk, q_a, k_a, v_a, gg, ga = jnp.split(w_in, offs, axis=1)
    gw = ATTN_GROUP_DIM
    qkv = lambda g: [a[:, g * gw:(g + 1) * gw] for a in (q_a, k_a, v_a)]
    pad = jnp.zeros((w_in.shape[0], NAT_W - OFF_GK - GLA_GATE_RANK), w_in.dtype)
    return jnp.concatenate([q_g, k_g, v_g, r_g, gg, ga, *qkv(0), gk, pad, *qkv(1), *qkv(2)], axis=1)


def _kv_rows(src, col_k, n_keep):
    b, dil, n_sub, _ = src.shape
    kv = src[:, :, n_sub - n_keep // dil:, col_k:col_k + 2 * ATTN_GROUP_DIM]
    kv = jnp.swapaxes(kv, 1, 2).astype(F32)
    return kv.reshape(b, n_keep, 2, ATTN_HEADS, ATTN_HEAD_DIM)


def kernel(x_prompt, x_sample, state_gla, cache_kv_w128, cache_kv_w512, cache_kv_w2048, g_norm_mix, w_in, w_gk_up,
           b_gk_up, g_gla_norm, w_branch_gla, w_branch_attn, w_out, g_norm_ffn, w_router, b_router, w_gate_up,
           b_gate_up, w_down, b_down, g_final):
    depth = g_norm_mix.shape[0]
    assert depth == 1, "single-layer trunk"
    batch, seq, d = x_prompt.shape
    bd, dec_seq, _ = x_sample.shape
    assert d == D_MODEL and dec_seq == 1 and seq % ATTN_GROUPS[-1][0] == 0
    n_experts = w_router.shape[-1]
    caches = (cache_kv_w128[0], cache_kv_w512[0], cache_kv_w2048[0])
    gw = ATTN_GROUP_DIM

    w_in_f32 = _prep_w_in(w_in[0])
    w_in_bf = w_in_f32.astype(BF16)
    w_up = jnp.zeros((LANES, GLA_HEADS * GLA_HEAD_K), F32).at[:GLA_GATE_RANK].set(w_gk_up[0])
    w_up_bf = w_up.astype(BF16)
    b_up = b_gk_up[0][None, :]
    g_mix = g_norm_mix[0][None, :]
    g_gla = g_gla_norm[0][None, :]
    wbg, wba, wo = w_branch_gla[0], w_branch_attn[0], w_out[0]
    g_ffn = g_norm_ffn[0][None, :]
    w_r = jnp.zeros((d, LANES), F32).at[:, :n_experts].set(w_router[0])
    wr_hi = w_r.astype(BF16)
    wr_lo = (w_r - wr_hi.astype(F32)).astype(BF16)
    b_r = jnp.zeros((1, LANES), F32).at[0, :n_experts].set(b_router[0])
    bgu, bdn = b_gate_up[0][:, None, :], b_down[0][:, None, :]
    g_fin = g_final[None, :]

    xp = x_prompt.reshape(batch * seq, d)
    nat = _norm_proj(xp, g_mix, w_in_bf[:, :NAT_W], batch, seq, 1, tm=1024, tn=1024)
    srcs = [nat] + [_norm_proj(xp, g_mix, w_in_bf[:, OFF_QKV[g]:OFF_QKV[g] + QKV_W], batch, seq, ATTN_GROUPS[g][1],
                               tm=1024, tn=QKV_W) for g in (1, 2)]
    nat2d = nat.reshape(batch * seq, NAT_W)
    u_p, st_p = _gla_prompt(nat2d, w_up_bf, b_up, g_gla, batch, seq, tc=512)
    outs, lses = [], []
    for g in range(N_GROUPS):
        col0 = OFF_QKV0 if g == 0 else 0
        o, lse = _attn_prompt(srcs[g], col0, g, tq=min(256, seq // ATTN_GROUPS[g][1]))
        outs.append(o)
        lses.append(lse)
    x1_p, h2_p, route_p, cpad_p = _merge(xp, u_p, nat2d, outs, lses, wbg.astype(BF16), wba.astype(BF16), wo.astype(BF16),
                                 g_ffn, wr_hi, wr_lo, b_r, n_experts, seq, tm=512, ts=MOE_TOKEN_TILE, precise=False)

    xs = x_sample.reshape(bd, d)
    proj_s = _norm_proj_precise(xs, g_mix, w_in_f32, tn=1024)
    proj_s3 = proj_s.reshape(bd, 1, PROJ_W)
    u_s, st_s = _gla_sample(proj_s3, state_gla[0], w_up, b_up, g_gla)
    o_s, l_s = _attn_sample(proj_s3, caches)
    outs_s = [o_s[:, 0, g * gw:(g + 1) * gw].reshape(1, 1, bd, gw) for g in range(N_GROUPS)]
    lses_s = [l_s[:, 0, g * gw:(g + 1) * gw].reshape(1, 1, bd, gw) for g in range(N_GROUPS)]
    x1_s, h2_s, route_s, cpad_s = _merge(xs, u_s.reshape(bd, d), proj_s, outs_s, lses_s, wbg, wba, wo, g_ffn,
                                 w_r, w_r, b_r, n_experts, bd, tm=bd, ts=bd, precise=True)

    ts, tm_e = MOE_TOKEN_TILE, MOE_EXPERT_TILE
    assert bd <= ts
    n_tiles_p = batch * seq // ts
    rows = _tile_rows(ts, n_experts)
    r_pad, nc = _round_up(rows, LANES), rows // ROW_ALIGN
    n_tiles_e = -(-(n_tiles_p * rows + _tile_rows(bd, n_experts) + n_experts * (tm_e - 1)) // tm_e)
    cpad = jnp.concatenate([cpad_p[::8, :n_experts], cpad_s[::8, :n_experts]], axis=0).astype(jnp.int32)
    plan = _moe_plan(cpad, nc, n_experts, tm_e, n_tiles_e)
    pad_rows = lambda a, fill: jnp.concatenate([a, jnp.full((ts - bd, a.shape[1]), fill, a.dtype)], axis=0)
    h2_s, x1_s = pad_rows(h2_s, 0), pad_rows(x1_s, 0)
    route_s = pad_rows(route_s, -1.0)
    xs = _compact(h2_p, route_p, h2_s, route_s, plan, ts, r_pad, nc, n_tiles_e, tm_e)
    new_rows = [proj_s[:, OFF_QKV[g] + gw:OFF_QKV[g] + 3 * gw].reshape(bd, 1, 2, ATTN_HEADS, ATTN_HEAD_DIM)
                .astype(caches[g].dtype) for g in range(N_GROUPS)]
    ys = _experts(xs, plan[2], plan[3], w_gate_up[0], bgu, w_down[0], bdn, tm_e)
    kv_s = [_shift_cache(caches[g], new_rows[g], min(512, caches[g].shape[1]))[None] for g in range(N_GROUPS)]
    y_p = _combine(route_p, x1_p, g_fin, ys, plan, 0, ts, r_pad, nc)
    y_s = _combine(route_s, x1_s, g_fin, ys, plan, n_tiles_p, ts, r_pad, nc)[:bd]

    kv_p = [_kv_rows(srcs[g], (OFF_QKV0 if g == 0 else 0) + gw, min(ATTN_GROUPS[g][0], seq))[None]
            for g in range(N_GROUPS)]
    return (y_p.reshape(batch, seq, d), y_s.reshape(bd, 1, d),
            jnp.swapaxes(st_p, -1, -2)[None], kv_p[0], kv_p[1], kv_p[2],
            st_s[None], kv_s[0], kv_s[1], kv_s[2])
```

```python
import functools

import jax
import jax.numpy as jnp
from jax import lax
from jax.experimental import pallas as pl
from jax.experimental.pallas import tpu as pltpu

F32 = jnp.float32
BF16 = jnp.bfloat16
HIGHEST = lax.Precision.HIGHEST

D_MODEL = 1024
NORM_EPS = 1e-5
GLA_HEADS = 4
GLA_HEAD_K = 128
GLA_HEAD_V = 256
GLA_GATE_RANK = 16
GLA_GATE_NORMALIZER = 16.0
GLA_CHUNK = 64
ATTN_GROUPS = ((128, 1), (512, 4), (2048, 16))
N_GROUPS = 3
ATTN_HEADS = 4
ATTN_HEAD_DIM = 128
ATTN_GROUP_DIM = ATTN_HEADS * ATTN_HEAD_DIM
TOP_K = 4
SWIGLU_LIMIT = 7.0
SWIGLU_ALPHA = 1.702
NEG_BIG = -1e30
LANES = 128
ROW_ALIGN = 8
MOE_TOKEN_TILE = 256
MOE_EXPERT_TILE = 512

OFF_QG, OFF_KG, OFF_VG, OFF_RG = 0, 512, 1024, 2048
OFF_GATE_GLA, OFF_GATE_ATTN = 3072, 4096
OFF_QKV0 = 5120
OFF_GK = 6656
NAT_W = 7168
QKV_W = 3 * ATTN_GROUP_DIM
PROJ_W = NAT_W + 2 * QKV_W
OFF_QKV = (OFF_QKV0, NAT_W, NAT_W + QKV_W)
PROJ_SPLITS = (512, 512, 1024, 1024, GLA_GATE_RANK, 1536, 1536, 1536, 1024, 1024)

VMEM_LIMIT = 48 * 1024 * 1024


def _cparams(*sem):
    return pltpu.CompilerParams(dimension_semantics=sem, vmem_limit_bytes=VMEM_LIMIT)


def _alibi_slope(group, head):
    n = N_GROUPS * ATTN_HEADS
    return 2.0 ** (-8.0 * (group * ATTN_HEADS + head + 1) / n)


def _log_sigmoid(x):
    return jnp.minimum(x, 0.0) - jnp.log1p(jnp.exp(-jnp.abs(x)))


def _dot_nt(a, b):
    return lax.dot_general(a, b, (((1,), (1,)), ((), ())), preferred_element_type=F32)


def _dot(a, b):
    return jnp.dot(a, b, preferred_element_type=F32)


def _dot_hi(a, b):
    return jnp.dot(a, b, preferred_element_type=F32, precision=HIGHEST)


def _split3(x):
    hi = x.astype(BF16)
    r1 = x - hi.astype(F32)
    mid = r1.astype(BF16)
    lo = (r1 - mid.astype(F32)).astype(BF16)
    return hi, mid, lo


def _rmsnorm(x, g):
    return x * lax.rsqrt(jnp.mean(x * x, axis=-1, keepdims=True) + NORM_EPS) * g


def _proj_kernel(x_ref, g_ref, w_ref, o_ref, *rest, dilation, kv_first, kv_row0):
    kv_ref = rest[0] if kv_first is not None else None
    h_ref = rest[-2] if dilation > 1 else rest[-1]

    @pl.when(pl.program_id(1) == 0)
    def _():
        h_ref[...] = _rmsnorm(x_ref[...], g_ref[...]).astype(BF16)

    acc = _dot(h_ref[...], w_ref[...])
    if dilation == 1:
        o_ref[0, 0] = acc.astype(o_ref.dtype)
    else:
        scr = rest[-1]
        n = acc.shape[0] // dilation
        for c in range(acc.shape[1] // LANES):
            cols = slice(c * LANES, (c + 1) * LANES)
            scr[c] = acc[:, cols]
            for r in range(dilation):
                o_ref[0, r, :, cols] = scr[c, pl.ds(r, n, stride=dilation), :].astype(o_ref.dtype)

    if kv_ref is not None:
        @pl.when(pl.program_id(0) % kv_first[1] >= kv_first[0])
        def _():
            rows = kv_ref.shape[1]
            for j in range(2):
                for h in range(ATTN_HEADS):
                    c0 = ATTN_GROUP_DIM + (j * ATTN_HEADS + h) * ATTN_HEAD_DIM
                    kv_ref[0, :, j, h, :] = acc[kv_row0:kv_row0 + rows, c0:c0 + ATTN_HEAD_DIM]


def _norm_proj(x2d, g_row, w_bf, batch, seq, dilation, tm, tn, kv_keep=None):
    m, d = x2d.shape
    n = w_bf.shape[1]
    tps = seq // tm
    out_specs = [pl.BlockSpec((1, dilation, tm // dilation, tn), lambda i, j: (i // tps, 0, i % tps, j))]
    out_shape = [jax.ShapeDtypeStruct((batch, dilation, seq // dilation, n), BF16)]
    kv_first, kv_row0 = None, 0
    if kv_keep is not None:
        assert n == tn == QKV_W
        kv_tiles = max(kv_keep // tm, 1)
        rows = min(kv_keep, tm)
        first = tps - kv_tiles
        kv_first, kv_row0 = (first, tps), tm - rows
        out_specs.append(pl.BlockSpec((1, rows, 2, ATTN_HEADS, ATTN_HEAD_DIM),
                                      lambda i, j: (i // tps, jnp.maximum(i % tps - first, 0), 0, 0, 0)))
        out_shape.append(jax.ShapeDtypeStruct((batch, kv_keep, 2, ATTN_HEADS, ATTN_HEAD_DIM), F32))
    scratch = [pltpu.VMEM((tm, d), BF16)]
    if dilation > 1:
        scratch.append(pltpu.VMEM((tn // LANES, tm, LANES), F32))
    res = pl.pallas_call(
        functools.partial(_proj_kernel, dilation=dilation, kv_first=kv_first, kv_row0=kv_row0),
        grid=(m // tm, n // tn),
        in_specs=[pl.BlockSpec((tm, d), lambda i, j: (i, 0)),
                  pl.BlockSpec((1, d), lambda i, j: (0, 0)),
                  pl.BlockSpec((d, tn), lambda i, j: (0, j))],
        out_specs=out_specs,
        out_shape=out_shape,
        scratch_shapes=scratch,
        compiler_params=_cparams("arbitrary", "arbitrary"),
        name=f"norm_proj_d{dilation}",
    )(x2d, g_row, w_bf)
    return res if kv_keep is not None else res[0]


def _proj_precise_kernel(x_ref, g_ref, w_ref, o_ref):
    o_ref[...] = _dot_hi(_rmsnorm(x_ref[...], g_ref[...]), w_ref[...])


def _norm_proj_precise(x2d, g_row, w_f32, tn):
    m, d = x2d.shape
    n = w_f32.shape[1]
    return pl.pallas_call(
        _proj_precise_kernel,
        grid=(pl.cdiv(n, tn),),
        in_specs=[pl.BlockSpec((m, d), lambda j: (0, 0)),
                  pl.BlockSpec((1, d), lambda j: (0, 0)),
                  pl.BlockSpec((d, tn), lambda j: (0, j))],
        out_specs=pl.BlockSpec((m, tn), lambda j: (0, j)),
        out_shape=jax.ShapeDtypeStruct((m, n), F32),
        compiler_params=_cparams("parallel"),
        name="norm_proj_sample",
    )(x2d, g_row, w_f32)


def _gla_kernel(q_ref, k_ref, v_ref, r_ref, gk_ref, wup_ref, bup_ref, gn_ref,
                u_ref, st_ref, s_scr, o_scr, *, n_chunks):
    t = pl.program_id(1)
    kd, vd = GLA_HEAD_K, GLA_HEAD_V

    @pl.when(t == 0)
    def _():
        s_scr[...] = jnp.zeros_like(s_scr)

    gk = _dot(gk_ref[...], wup_ref[...]) + bup_ref[...]
    log_a = _log_sigmoid(gk) * (1.0 / GLA_GATE_NORMALIZER)
    c_sz = GLA_CHUNK
    row = lax.broadcasted_iota(jnp.int32, (c_sz, c_sz), 0)
    col = lax.broadcasted_iota(jnp.int32, (c_sz, c_sz), 1)
    tri = row >= col
    tri_bf = jnp.where(tri, 1.0, 0.0).astype(BF16)
    scale = kd ** -0.5

    for c in range(n_chunks):
        sl = slice(c * c_sz, (c + 1) * c_sz)
        g_hi, g_mid, g_lo = _split3(log_a[sl])
        b_all = _dot(tri_bf, g_hi) + _dot(tri_bf, g_mid) + _dot(tri_bf, g_lo)
        for h in range(GLA_HEADS):
            kc, vc = slice(h * kd, (h + 1) * kd), slice(h * vd, (h + 1) * vd)
            b = b_all[:, kc]
            b_last = b[c_sz - 1:c_sz, :]
            q = q_ref[sl, kc].astype(F32) * scale
            k = k_ref[sl, kc].astype(F32)
            v_bf = v_ref[sl, vc]
            q_in = (q * jnp.exp(b)).astype(BF16)
            k_in = (k * jnp.exp(-b)).astype(BF16)
            k_out = (k * jnp.exp(b_last - b)).astype(BF16)
            a = jnp.where(tri, _dot_nt(q_in, k_in), 0.0)
            st = s_scr[h]
            o_scr[sl, vc] = _dot_nt(q_in, st.astype(BF16)) + _dot(a.astype(BF16), v_bf)
            s_scr[h] = st * jnp.exp(b_last) + pl.dot(v_bf, k_out, trans_a=True)

    for h in range(GLA_HEADS):
        vc = slice(h * vd, (h + 1) * vd)
        r = r_ref[:, vc].astype(F32)
        u_ref[:, vc] = (_rmsnorm(o_scr[:, vc], gn_ref[...]) * (r * jax.nn.sigmoid(r))).astype(u_ref.dtype)

    @pl.when(t == pl.num_programs(1) - 1)
    def _():
        st_ref[0] = s_scr[...]


def _gla_prompt(proj, w_up_bf, b_up, g_norm, batch, seq, tc):
    nt = seq // tc
    kw, vw = GLA_HEADS * GLA_HEAD_K, GLA_HEADS * GLA_HEAD_V
    rows = lambda b, t: b * nt + t
    kern = functools.partial(_gla_kernel, n_chunks=tc // GLA_CHUNK)
    full = lambda a: pl.BlockSpec(a.shape, lambda b, t: (0, 0))
    return pl.pallas_call(
        kern,
        grid=(batch, nt),
        in_specs=[
            pl.BlockSpec((tc, kw), lambda b, t: (rows(b, t), OFF_QG // kw)),
            pl.BlockSpec((tc, kw), lambda b, t: (rows(b, t), OFF_KG // kw)),
            pl.BlockSpec((tc, vw), lambda b, t: (rows(b, t), OFF_VG // vw)),
            pl.BlockSpec((tc, vw), lambda b, t: (rows(b, t), OFF_RG // vw)),
            pl.BlockSpec((tc, LANES), lambda b, t: (rows(b, t), OFF_GK // LANES)),
            full(w_up_bf), full(b_up), full(g_norm),
        ],
        out_specs=[
            pl.BlockSpec((tc, vw), lambda b, t: (rows(b, t), 0)),
            pl.BlockSpec((1, GLA_HEADS, GLA_HEAD_V, GLA_HEAD_K), lambda b, t: (b, 0, 0, 0)),
        ],
        out_shape=[jax.ShapeDtypeStruct((batch * seq, vw), BF16),
                   jax.ShapeDtypeStruct((batch, GLA_HEADS, GLA_HEAD_V, GLA_HEAD_K), F32)],
        scratch_shapes=[pltpu.VMEM((GLA_HEADS, GLA_HEAD_V, GLA_HEAD_K), F32), pltpu.VMEM((tc, vw), F32)],
        compiler_params=_cparams("parallel", "arbitrary"),
        name="gla_prompt",
    )(proj, proj, proj, proj, proj, w_up_bf, b_up, g_norm)


def _attn_kernel(q_ref, kc_ref, kp_ref, vc_ref, vp_ref, o_ref, l_ref, *, group, dilation, blk, n_qb):
    n = pl.program_id(2)
    e = ATTN_HEAD_DIM
    i_idx = lax.broadcasted_iota(jnp.int32, (blk, 2 * blk), 0)
    j_idx = lax.broadcasted_iota(jnp.int32, (blk, 2 * blk), 1)
    rel = i_idx + blk - j_idx
    band = (rel >= 0) & (rel <= blk)
    dist = (dilation * rel).astype(F32)
    bias = [jnp.where(band, -_alibi_slope(group, h) * dist, NEG_BIG) for h in range(ATTN_HEADS)]
    has_prev = (j_idx >= blk) | (n > 0)
    lane = lax.broadcasted_iota(jnp.int32, (blk, LANES), 1)
    for qb in range(n_qb):
        rows = slice(qb * blk, (qb + 1) * blk)
        q = q_ref[0, 0, rows, :]
        if qb == 0:
            k_prev, v_prev = kp_ref[0, 0], vp_ref[0, 0]
        else:
            prev = slice((qb - 1) * blk, qb * blk)
            k_prev, v_prev = kc_ref[0, 0, prev, :], vc_ref[0, 0, prev, :]
        kk = jnp.concatenate([k_prev, kc_ref[0, 0, rows, :]], axis=0)
        vv = jnp.concatenate([v_prev, vc_ref[0, 0, rows, :]], axis=0)
        lse_tile = jnp.zeros((blk, LANES), F32)
        for h in range(ATTN_HEADS):
            cols = slice(h * e, (h + 1) * e)
            b_h = jnp.where(has_prev, bias[h], NEG_BIG) if qb == 0 else bias[h]
            s = _dot_nt(q[:, cols], kk[:, cols]) * (e ** -0.5) + b_h
            m = jnp.max(s, axis=-1, keepdims=True)
            p = jnp.exp(s - m)
            den = jnp.sum(p, axis=-1, keepdims=True)
            o = _dot(p.astype(BF16), vv[:, cols]) / den
            o_ref[0, 0, rows, cols] = o.astype(o_ref.dtype)
            lse_tile = jnp.where(lane == h, m + jnp.log(den), lse_tile)
        l_ref[0, 0, rows, :] = lse_tile


def _attn_prompt(src, col0, group, tq):
    batch, dilation, n_sub, _ = src.shape
    window, dil = ATTN_GROUPS[group]
    assert dil == dilation
    blk = window // dilation
    n_qb = tq // blk
    gw = ATTN_GROUP_DIM
    qc, kc, vc = col0 // gw, col0 // gw + 1, col0 // gw + 2
    cur = lambda c: pl.BlockSpec((1, 1, tq, gw), lambda b, r, n: (b, r, n, c))
    prev = lambda c: pl.BlockSpec((1, 1, blk, gw), lambda b, r, n: (b, r, jnp.maximum(n * n_qb - 1, 0), c))
    kern = functools.partial(_attn_kernel, group=group, dilation=dilation, blk=blk, n_qb=n_qb)
    out_spec = pl.BlockSpec((1, 1, tq, gw), lambda b, r, n: (b, r, n, 0))
    return pl.pallas_call(
        kern,
        grid=(batch, dilation, n_sub // tq),
        in_specs=[cur(qc), cur(kc), prev(kc), cur(vc), prev(vc)],
        out_specs=[out_spec, pl.BlockSpec((1, 1, tq, LANES), lambda b, r, n: (b, r, n, 0))],
        out_shape=[jax.ShapeDtypeStruct((batch, dilation, n_sub, gw), BF16),
                   jax.ShapeDtypeStruct((batch, dilation, n_sub, LANES), F32)],
        compiler_params=_cparams("parallel", "parallel", "arbitrary"),
        name=f"attn_prompt_g{group}",
    )(src, src, src, src, src)


def _merge_kernel(x_ref, u_ref, gg_ref, ga_ref, o1_ref, o2_ref, o3_ref, l1_ref, l2_ref, l3_ref,
                  wbg_ref, wba_ref, wo_ref, gf_ref, wr_hi_ref, wr_lo_ref, br_ref,
                  x1_ref, h2_ref, route_ref, cpad_ref, *scr, n_experts, dilations, precise, ts):
    tm = x_ref.shape[0]
    scr = list(scr)

    def natural(ref, dilation):
        if dilation == 1:
            return ref[0, 0].astype(F32)
        buf = scr.pop(0)
        n = tm // dilation
        for c in range(buf.shape[0]):
            for r in range(dilation):
                buf[c, pl.ds(r, n, stride=dilation), :] = ref[0, r, :, c * LANES:(c + 1) * LANES].astype(F32)
        return jnp.concatenate([buf[c] for c in range(buf.shape[0])], axis=1)

    o1, o2, o3 = (natural(r, d) for r, d in zip((o1_ref, o2_ref, o3_ref), dilations))
    l1, l2, l3 = (natural(r, d) for r, d in zip((l1_ref, l2_ref, l3_ref), dilations))
    mm = _dot_hi if precise else (lambda a, b: _dot(a.astype(BF16), b))
    lm = jnp.maximum(jnp.maximum(l1, l2), l3)
    e1, e2, e3 = jnp.exp(l1 - lm), jnp.exp(l2 - lm), jnp.exp(l3 - lm)
    inv = 1.0 / (e1 + e2 + e3)
    head = lax.broadcasted_iota(jnp.int32, (LANES, ATTN_GROUP_DIM), 0)
    owner = lax.broadcasted_iota(jnp.int32, (LANES, ATTN_GROUP_DIM), 1) // ATTN_HEAD_DIM
    spread = jnp.where(head == owner, 1.0, 0.0)

    def per_lane(w):
        if precise:
            return _dot_hi(w, spread)
        return _dot(w.astype(BF16), spread.astype(BF16))

    o_att = per_lane(e1 * inv) * o1 + per_lane(e2 * inv) * o2 + per_lane(e3 * inv) * o3
    merged = (jax.nn.sigmoid(gg_ref[...].astype(F32)) * mm(u_ref[...], wbg_ref[...])
              + jax.nn.sigmoid(ga_ref[...].astype(F32)) * mm(o_att, wba_ref[...]))
    x1 = x_ref[...] + mm(merged, wo_ref[...])
    x1_ref[...] = x1
    h2 = _rmsnorm(x1, gf_ref[...])
    h2_ref[...] = h2.astype(BF16)

    if precise:
        logits = _dot_hi(h2, wr_hi_ref[...]) + br_ref[...]
    else:
        h_hi, h_mid, _ = _split3(h2)
        both = _dot(h_hi, wr_lo_ref[...])
        logits = both[:, :LANES] + both[:, LANES:] + _dot(h_mid, wr_hi_ref[...]) + br_ref[...]
    lane = lax.broadcasted_iota(jnp.int32, (ts, LANES), 1).astype(F32)
    ri = lax.broadcasted_iota(jnp.int32, (ts, ts), 0)
    ci = lax.broadcasted_iota(jnp.int32, (ts, ts), 1)
    earlier = jnp.where(ci < ri, 1.0, 0.0).astype(BF16)
    li = lax.broadcasted_iota(jnp.int32, (LANES, LANES), 0)
    lj = lax.broadcasted_iota(jnp.int32, (LANES, LANES), 1)
    before = jnp.where(li < lj, 1.0, 0.0).astype(BF16)
    for s in range(tm // ts):
        rows = slice(s * ts, (s + 1) * ts)
        cur = jnp.where(lane < n_experts, logits[rows, :], -jnp.inf)
        tops, sels = [], []
        for _ in range(TOP_K):
            m = jnp.max(cur, axis=-1, keepdims=True)
            idx = jnp.min(jnp.where(cur == m, lane, float(LANES)), axis=-1, keepdims=True)
            sel = lane == idx
            tops.append(m)
            sels.append(sel)
            cur = jnp.where(sel, -jnp.inf, cur)
        ex = [jnp.exp(m - tops[0]) for m in tops]
        den = ex[0] + ex[1] + ex[2] + ex[3]
        mem = jnp.zeros((ts, LANES), F32)
        for sel in sels:
            mem = jnp.where(sel, 1.0, mem)
        rank = _dot(earlier, mem.astype(BF16))
        count = jnp.sum(mem, axis=0, keepdims=True)
        cpad = jnp.ceil(count * (1.0 / ROW_ALIGN)) * ROW_ALIGN
        cpad8 = jnp.broadcast_to(cpad, (8, LANES))
        seg_start = _dot(cpad8.astype(BF16), before)[0:1]
        pos = seg_start + rank
        route = jnp.zeros((ts, LANES), F32)
        for k in range(TOP_K):
            dest = jnp.sum(jnp.where(sels[k], pos, 0.0), axis=-1, keepdims=True)
            route = jnp.where(lane == float(k), dest, route)
            route = jnp.where(lane == float(TOP_K + k), ex[k] / den, route)
        route_ref[rows, :] = route
        cpad_ref[s * 8:(s + 1) * 8, :] = cpad8


def _merge(x2d, u, gate_src, outs, lses, wbg, wba, wo, g_ffn, wr_hi, wr_lo, b_r, n_experts, seq, tm, ts, precise):
    m, d = x2d.shape
    gw = ATTN_GROUP_DIM
    tps = seq // tm
    dilations = tuple(o.shape[1] for o in outs)
    row = lambda w: pl.BlockSpec((tm, w), lambda i: (i, 0))
    full = lambda a: pl.BlockSpec(a.shape, lambda i: (0, 0))
    grp = lambda dil, w: pl.BlockSpec((1, dil, tm // dil, w), lambda i: (i // tps, 0, i % tps, 0))
    kern = functools.partial(_merge_kernel, n_experts=n_experts, dilations=dilations, precise=precise, ts=ts)
    n_dilated = sum(1 for dil in dilations if dil > 1)
    interleave = lambda w: [pltpu.VMEM((w // LANES, tm, LANES), F32)] * n_dilated
    return pl.pallas_call(
        kern,
        grid=(m // tm,),
        in_specs=[row(d), row(d),
                  pl.BlockSpec((tm, d), lambda i: (i, OFF_GATE_GLA // d)),
                  pl.BlockSpec((tm, d), lambda i: (i, OFF_GATE_ATTN // d)),
                  *[grp(dil, gw) for dil in dilations], *[grp(dil, LANES) for dil in dilations],
                  full(wbg), full(wba), full(wo), full(g_ffn), full(wr_hi), full(wr_lo), full(b_r)],
        out_specs=[row(d), row(d), row(LANES), pl.BlockSpec((tm // ts * 8, LANES), lambda i: (i, 0))],
        out_shape=[jax.ShapeDtypeStruct((m, d), F32), jax.ShapeDtypeStruct((m, d), BF16),
                   jax.ShapeDtypeStruct((m, LANES), F32), jax.ShapeDtypeStruct((m // ts * 8, LANES), F32)],
        scratch_shapes=interleave(gw) + interleave(LANES),
        compiler_params=_cparams("parallel"),
        name="merge_router_sample" if precise else "merge_router",
    )(x2d, u, gate_src, gate_src, *outs, *lses, wbg, wba, wo, g_ffn, wr_hi, wr_lo, b_r)


def _round_up(x, mult):
    return (x + mult - 1) // mult * mult


def _tile_rows(n_tokens, n_experts):
    return _round_up(n_tokens * TOP_K + n_experts * (ROW_ALIGN - 1), ROW_ALIGN)


def _moe_plan(cpad, nc, n_experts, tm_e, n_tiles_max):
    tot = jnp.sum(cpad, axis=0)
    gsize = (tot + tm_e - 1) // tm_e * tm_e
    g_end = jnp.cumsum(gsize)
    g_start = g_end - gsize
    n_used = (g_end[-1] // tm_e).astype(jnp.int32)
    o_start = jnp.cumsum(cpad, axis=0) - cpad
    seg_end = jnp.cumsum(cpad, axis=1)
    seg_start = seg_end - cpad

    row0 = jnp.arange(nc, dtype=jnp.int32) * ROW_ALIGN
    e_of = jnp.sum((seg_end[:, None, :] <= row0[None, :, None]).astype(jnp.int32), axis=-1)
    e_of = jnp.minimum(e_of, n_experts - 1)
    pick = (e_of[:, :, None] == jnp.arange(n_experts)[None, None, :]).astype(jnp.int32)
    base = g_start[None, :] + o_start - seg_start
    dst = jnp.sum(pick * base[:, None, :], axis=-1) + row0[None, :]
    nch = (seg_end[:, -1] // ROW_ALIGN).astype(jnp.int32)
    dst = jnp.where(jnp.arange(nc)[None, :] < nch[:, None], dst, 0).astype(jnp.int32)

    t0 = jnp.arange(n_tiles_max, dtype=jnp.int32) * tm_e
    tile_expert = jnp.minimum(jnp.sum((g_end[None, :] <= t0[:, None]).astype(jnp.int32), axis=-1), n_experts - 1)
    tile_id = jnp.arange(n_tiles_max)
    last = jnp.sum(jnp.where(tile_id == n_used - 1, tile_expert, 0))
    tile_expert = jnp.where(tile_id < n_used, tile_expert, last).astype(jnp.int32)
    gap_start = (g_start + tot).astype(jnp.int32)
    gap_chunks = ((gsize - tot) // ROW_ALIGN).astype(jnp.int32)
    return dst.reshape(-1), nch, tile_expert, n_used.reshape(1), gap_start, gap_chunks


def _loop(n, fn, unroll=1):
    def body(i, carry):
        for u in range(unroll):
            fn(i * unroll + u)
        return carry

    main = n // unroll if unroll > 1 else n
    lax.fori_loop(0, main, body, 0)
    if unroll > 1:
        lax.fori_loop(main * unroll, n, lambda i, carry: (fn(i), carry)[1], 0)


def _compact_kernel(dst_ref, nch_ref, gap_start_ref, gap_chunks_ref, nu_ref, hp_ref, rp_ref, hs_ref, rs_ref,
                    xs_ref, xc_scr, zero_scr, sem, *, nc):
    s = pl.program_id(0)
    last = pl.num_programs(0) - 1
    is_sample = s == last
    ts, r_pad = hp_ref.shape[0], xc_scr.shape[1]

    def chunk(tile, c):
        slot = tile % 2
        src = xc_scr.at[slot, pl.ds(pl.multiple_of(c * ROW_ALIGN, ROW_ALIGN), ROW_ALIGN)]
        dst = xs_ref.at[pl.ds(pl.multiple_of(dst_ref[tile * nc + c], ROW_ALIGN), ROW_ALIGN)]
        return pltpu.make_async_copy(src, dst, sem.at[slot])

    def drain(tile):
        slot = tile % 2
        one = pltpu.make_async_copy(xc_scr.at[slot, pl.ds(0, ROW_ALIGN)], xs_ref.at[pl.ds(0, ROW_ALIGN)], sem.at[slot])
        _loop(nch_ref[tile], lambda c: one.wait(), unroll=8)

    @pl.when(s >= 2)
    def _():
        drain(s - 2)

    h = jnp.where(is_sample, hs_ref[...], hp_ref[...])
    route = jnp.where(is_sample, rs_ref[...], rp_ref[...])
    dest_t = jnp.transpose(route)
    row_id = lax.broadcasted_iota(jnp.int32, (r_pad, ts), 0).astype(F32)
    perm = jnp.zeros((r_pad, ts), F32)
    for k in range(TOP_K):
        perm = jnp.where(row_id == dest_t[k:k + 1, :], 1.0, perm)
    xc_scr[s % 2] = _dot(perm.astype(BF16), h)
    _loop(nch_ref[s], lambda c: chunk(s, c).start(), unroll=4)

    @pl.when(is_sample)
    def _():
        @pl.when(s >= 1)
        def _():
            drain(s - 1)
        drain(s)

        zero_scr[...] = jnp.zeros_like(zero_scr)
        tile = zero_scr.shape[0]

        def gap(e, j):
            row = pl.multiple_of(gap_start_ref[e] + j * ROW_ALIGN, ROW_ALIGN)
            return pltpu.make_async_copy(zero_scr.at[pl.ds(0, ROW_ALIGN)], xs_ref.at[pl.ds(row, ROW_ALIGN)], sem.at[0])

        def tail(t):
            row = pl.multiple_of(t * tile, tile)
            return pltpu.make_async_copy(zero_scr, xs_ref.at[pl.ds(row, tile)], sem.at[0])

        n_exp = gap_start_ref.shape[0]
        n_tail = xs_ref.shape[0] // tile - nu_ref[0]
        _loop(n_exp, lambda e: _loop(gap_chunks_ref[e], lambda j: gap(e, j).start()))
        _loop(n_tail, lambda t: tail(nu_ref[0] + t).start())
        _loop(n_exp, lambda e: _loop(gap_chunks_ref[e], lambda j: gap(e, j).wait()))
        _loop(n_tail, lambda t: tail(nu_ref[0] + t).wait())


def _compact(h2_p, route_p, h2_s, route_s, plan, ts, r_pad, nc, n_tiles_e, tm_e):
    m, d = h2_p.shape
    n_p = m // ts
    dst, nch, _, n_used, gap_start, gap_chunks = plan
    prompt = lambda w: pl.BlockSpec((ts, w), lambda i, *_: (jnp.minimum(i, n_p - 1), 0))
    sample = lambda w: pl.BlockSpec((ts, w), lambda i, *_: (0, 0))
    return pl.pallas_call(
        functools.partial(_compact_kernel, nc=nc),
        grid_spec=pltpu.PrefetchScalarGridSpec(
            num_scalar_prefetch=5,
            grid=(n_p + 1,),
            in_specs=[prompt(d), prompt(LANES), sample(d), sample(LANES)],
            out_specs=pl.BlockSpec(memory_space=pl.ANY),
            scratch_shapes=[pltpu.VMEM((2, r_pad, d), F32), pltpu.VMEM((tm_e, d), F32),
                            pltpu.SemaphoreType.DMA((2,))]),
        out_shape=jax.ShapeDtypeStruct((n_tiles_e * tm_e, d), F32),
        compiler_params=_cparams("arbitrary"),
        name="moe_compact",
    )(dst, nch, gap_start, gap_chunks, n_used, h2_p, route_p, h2_s, route_s)


def _expert_kernel(te_ref, nu_ref, x_ref, wgu_ref, bgu_ref, wd_ref, bd_ref, o_ref, wgu_bf, wd_bf, *, d_ff, sub):
    t = pl.program_id(0)
    used = t < nu_ref[0]

    @pl.when(used)
    def _():
        @pl.when((t == 0) | (te_ref[t] != te_ref[jnp.maximum(t - 1, 0)]))
        def _():
            step = 128
            for i in range(wgu_bf.shape[0] // step):
                rows = slice(i * step, (i + 1) * step)
                wgu_bf[rows, :] = wgu_ref[0, rows, :].astype(BF16)
            for i in range(wd_bf.shape[0] // step):
                rows = slice(i * step, (i + 1) * step)
                wd_bf[rows, :] = wd_ref[0, rows, :].astype(BF16)

        for s in range(x_ref.shape[0] // sub):
            rows = slice(s * sub, (s + 1) * sub)
            gu = _dot(x_ref[rows, :].astype(BF16), wgu_bf[...]) + bgu_ref[0]
            gate = jnp.minimum(gu[:, :d_ff], SWIGLU_LIMIT)
            up = jnp.clip(gu[:, d_ff:], -SWIGLU_LIMIT, SWIGLU_LIMIT)
            act = (up + 1.0) * gate * jax.nn.sigmoid(SWIGLU_ALPHA * gate)
            o_ref[rows, :] = _dot(act.astype(BF16), wd_bf[...]) + bd_ref[0]

    @pl.when(jnp.logical_not(used))
    def _():
        o_ref[...] = jnp.zeros_like(o_ref)


def _experts(xs, tile_expert, n_used, wgu, bgu, wd, bd, tm_e):
    n_rows, d = xs.shape
    _, _, two_ff = wgu.shape
    d_ff = two_ff // 2
    kern = functools.partial(_expert_kernel, d_ff=d_ff, sub=tm_e)
    by_expert = lambda shape: pl.BlockSpec(shape, lambda t, te, nu: (te[t], 0, 0))
    return pl.pallas_call(
        kern,
        grid_spec=pltpu.PrefetchScalarGridSpec(
            num_scalar_prefetch=2,
            grid=(n_rows // tm_e,),
            in_specs=[pl.BlockSpec((tm_e, d), lambda t, te, nu: (jnp.minimum(t, nu[0] - 1), 0)),
                      by_expert((1, d, two_ff)), by_expert((1, 1, two_ff)),
                      by_expert((1, d_ff, d)), by_expert((1, 1, d))],
            out_specs=pl.BlockSpec((tm_e, d), lambda t, te, nu: (t, 0)),
            scratch_shapes=[pltpu.VMEM((d, two_ff), BF16), pltpu.VMEM((d_ff, d), BF16)]),
        out_shape=jax.ShapeDtypeStruct((n_rows, d), F32),
        compiler_params=pltpu.CompilerParams(dimension_semantics=("arbitrary",), vmem_limit_bytes=56 * 1024 * 1024),
        name="moe_experts",
    )(tile_expert, n_used, xs, wgu, bgu, wd, bd)


def _shift_kernel(c_ref, new_ref, o_ref):
    j = pl.program_id(1)
    last = pl.num_programs(1) - 1
    bn = o_ref.shape[1]

    @pl.when(j != last)
    def _():
        o_ref[...] = c_ref[...]

    @pl.when(j == last)
    def _():
        o_ref[0, 0:bn - 1] = c_ref[0, 1:bn]
        o_ref[0, bn - 1:bn] = new_ref[0]


def _shift_cache(cache, new_row, bn):
    bd, n = cache.shape[:2]
    tail = cache.shape[2:]
    zeros = (0,) * len(tail)
    return pl.pallas_call(
        _shift_kernel,
        grid=(bd, n // bn),
        in_specs=[pl.BlockSpec(tuple(pl.Element(s) for s in (1, bn) + tail),
                               lambda b, j: (b, jnp.minimum(j * bn + 1, n - bn)) + zeros),
                  pl.BlockSpec((1, 1) + tail, lambda b, j: (b, 0) + zeros)],
        out_specs=pl.BlockSpec((1, bn) + tail, lambda b, j: (b, j) + zeros),
        out_shape=jax.ShapeDtypeStruct(cache.shape, cache.dtype),
        compiler_params=_cparams("parallel", "arbitrary"),
        name="cache_shift",
    )(cache, new_row)


def _combine_kernel(dst_ref, nch_ref, route_ref, x1_ref, gfin_ref, ys_ref, y_ref, yc_scr, sem, *, nc, tile0):
    i = pl.program_id(0)

    def chunk(step, c):
        slot = step % 2
        tile = step + tile0
        src = ys_ref.at[pl.ds(pl.multiple_of(dst_ref[tile * nc + c], ROW_ALIGN), ROW_ALIGN)]
        dst = yc_scr.at[slot, pl.ds(pl.multiple_of(c * ROW_ALIGN, ROW_ALIGN), ROW_ALIGN)]
        return pltpu.make_async_copy(src, dst, sem.at[slot])

    def fetch(step):
        _loop(nch_ref[step + tile0], lambda c: chunk(step, c).start(), unroll=4)

    @pl.when(i == 0)
    def _():
        yc_scr[...] = jnp.zeros_like(yc_scr)
        fetch(i)

    @pl.when(i + 1 < pl.num_programs(0))
    def _():
        fetch(i + 1)

    one = pltpu.make_async_copy(ys_ref.at[pl.ds(0, ROW_ALIGN)], yc_scr.at[i % 2, pl.ds(0, ROW_ALIGN)], sem.at[i % 2])
    _loop(nch_ref[i + tile0], lambda c: one.wait(), unroll=8)

    ts, r_pad = x1_ref.shape[0], yc_scr.shape[1]
    route = route_ref[...]
    col_id = lax.broadcasted_iota(jnp.int32, (ts, r_pad), 1).astype(F32)
    weights = jnp.zeros((ts, r_pad), F32)
    for k in range(TOP_K):
        weights = jnp.where(col_id == route[:, k:k + 1], route[:, TOP_K + k:TOP_K + k + 1], weights)
    moe = _dot(weights.astype(BF16), yc_scr[i % 2].astype(BF16))
    y_ref[...] = _rmsnorm(x1_ref[...] + moe, gfin_ref[...])


def _combine(route, x1, g_final, ys, plan, tile0, ts, r_pad, nc):
    m, d = x1.shape
    dst, nch = plan[0], plan[1]
    return pl.pallas_call(
        functools.partial(_combine_kernel, nc=nc, tile0=tile0),
        grid_spec=pltpu.PrefetchScalarGridSpec(
            num_scalar_prefetch=2,
            grid=(m // ts,),
            in_specs=[pl.BlockSpec((ts, LANES), lambda i, *_: (i, 0)),
                      pl.BlockSpec((ts, d), lambda i, *_: (i, 0)),
                      pl.BlockSpec((1, d), lambda i, *_: (0, 0)),
                      pl.BlockSpec(memory_space=pl.ANY)],
            out_specs=pl.BlockSpec((ts, d), lambda i, *_: (i, 0)),
            scratch_shapes=[pltpu.VMEM((2, r_pad, d), F32), pltpu.SemaphoreType.DMA((2,))]),
        out_shape=jax.ShapeDtypeStruct((m, d), F32),
        compiler_params=_cparams("arbitrary"),
        name="moe_combine",
    )(dst, nch, route, x1, g_final, ys)


def _column(x_row):
    return jnp.transpose(jnp.broadcast_to(x_row, (LANES, LANES)))


def _gla_step_kernel(p_ref, s_ref, wup_ref, bup_ref, gn_ref, u_ref, so_ref):
    row = p_ref[0]
    gk8 = jnp.broadcast_to(row[:, OFF_GK:OFF_GK + LANES], (8, LANES))
    gk = _dot_hi(gk8, wup_ref[...]) + bup_ref[...]
    decay = jnp.exp(_log_sigmoid(gk[0:1, :]) * (1.0 / GLA_GATE_NORMALIZER))
    kd, vd = GLA_HEAD_K, GLA_HEAD_V
    wide = lambda c: jnp.concatenate([c, c], axis=1)
    for h in range(GLA_HEADS):
        q = row[:, OFF_QG + h * kd:OFF_QG + (h + 1) * kd] * (kd ** -0.5)
        k = row[:, OFF_KG + h * kd:OFF_KG + (h + 1) * kd]
        v = row[:, OFF_VG + h * vd:OFF_VG + (h + 1) * vd]
        r = row[:, OFF_RG + h * vd:OFF_RG + (h + 1) * vd]
        s_new = s_ref[0, h] * wide(_column(decay[:, h * kd:(h + 1) * kd])) + wide(_column(k)) * v
        so_ref[0, h] = s_new
        o = jnp.sum(wide(_column(q)) * s_new, axis=0, keepdims=True)
        u_ref[0, :, h * vd:(h + 1) * vd] = _rmsnorm(o, gn_ref[...]) * (r * jax.nn.sigmoid(r))


def _gla_sample(proj_s3, state, w_up, b_up, g_norm):
    bd = proj_s3.shape[0]
    full = lambda a: pl.BlockSpec(a.shape, lambda b: (0,) * a.ndim)
    st_spec = pl.BlockSpec((1, GLA_HEADS, GLA_HEAD_K, GLA_HEAD_V), lambda b: (b, 0, 0, 0))
    return pl.pallas_call(
        _gla_step_kernel,
        grid=(bd,),
        in_specs=[pl.BlockSpec((1, 1, PROJ_W), lambda b: (b, 0, 0)), st_spec,
                  full(w_up), full(b_up), full(g_norm)],
        out_specs=[pl.BlockSpec((1, 1, GLA_HEADS * GLA_HEAD_V), lambda b: (b, 0, 0)), st_spec],
        out_shape=[jax.ShapeDtypeStruct((bd, 1, GLA_HEADS * GLA_HEAD_V), F32),
                   jax.ShapeDtypeStruct(state.shape, state.dtype)],
        compiler_params=_cparams("parallel"),
        name="gla_sample",
    )(proj_s3, state, w_up, b_up, g_norm)


def _attn_step_kernel(p_ref, c1_ref, c2_ref, c3_ref, o_ref, l_ref):
    row = p_ref[0]
    gw, e = ATTN_GROUP_DIM, ATTN_HEAD_DIM
    for g, c_ref in enumerate((c1_ref, c2_ref, c3_ref)):
        window, dilation = ATTN_GROUPS[g]
        blk = window // dilation
        back = (dilation * (blk - lax.broadcasted_iota(jnp.int32, (blk, 1), 0))).astype(F32)
        lane = lax.broadcasted_iota(jnp.int32, (1, LANES), 1)
        lse_row = jnp.zeros((1, LANES), F32)
        for h in range(ATTN_HEADS):
            lo = OFF_QKV[g] + h * e
            q = row[:, lo:lo + e]
            k_new = row[:, lo + gw:lo + gw + e]
            v_new = row[:, lo + 2 * gw:lo + 2 * gw + e]
            k_c = c_ref[0, :, 0, h, :]
            v_c = c_ref[0, :, 1, h, :]
            s = jnp.sum(k_c * q, axis=-1, keepdims=True) * (e ** -0.5) - _alibi_slope(g, h) * back
            s0 = jnp.sum(k_new * q, axis=-1, keepdims=True) * (e ** -0.5)
            m = jnp.maximum(jnp.max(s, axis=0, keepdims=True), s0)
            p = jnp.exp(s - m)
            p0 = jnp.exp(s0 - m)
            den = jnp.sum(p, axis=0, keepdims=True) + p0
            o = (jnp.sum(p * v_c, axis=0, keepdims=True) + p0 * v_new) / den
            cols = slice(g * gw + h * e, g * gw + (h + 1) * e)
            o_ref[0, :, cols] = o
            lse_row = jnp.where(lane == h, m + jnp.log(den), lse_row)
        l_ref[0, :, g * LANES:(g + 1) * LANES] = lse_row


def _attn_sample(proj_s3, caches):
    bd = proj_s3.shape[0]
    gw = ATTN_GROUP_DIM
    views, specs = [], []
    for g, cache in enumerate(caches):
        window, dilation = ATTN_GROUPS[g]
        n_buf = cache.shape[1]
        assert n_buf == window, "cache must hold exactly one window of rows"
        blk = n_buf // dilation
        views.append(cache.reshape(bd, blk, dilation, 2, ATTN_HEADS, ATTN_HEAD_DIM))
        specs.append(pl.BlockSpec((1, blk, None, 2, ATTN_HEADS, ATTN_HEAD_DIM), lambda b: (b, 0, 0, 0, 0, 0)))
    out_spec = pl.BlockSpec((1, 1, N_GROUPS * gw), lambda b: (b, 0, 0))
    return pl.pallas_call(
        _attn_step_kernel,
        grid=(bd,),
        in_specs=[pl.BlockSpec((1, 1, PROJ_W), lambda b: (b, 0, 0))] + specs,
        out_specs=[out_spec, pl.BlockSpec((1, 1, N_GROUPS * LANES), lambda b: (b, 0, 0))],
        out_shape=[jax.ShapeDtypeStruct((bd, 1, N_GROUPS * gw), F32),
                   jax.ShapeDtypeStruct((bd, 1, N_GROUPS * LANES), F32)],
        compiler_params=_cparams("parallel"),
        name="attn_sample",
    )(proj_s3, *views)


def _reorder_cols(a, dtype):
    offs, acc = [], 0
    for s in PROJ_SPLITS[:-1]:
        acc += s
        offs.append(acc)
    q_g, k_g, v_g, r_g, gk, q_a, k_a, v_a, gg, ga = [p.astype(dtype) for p in jnp.split(a, offs, axis=1)]
    gw = ATTN_GROUP_DIM
    qkv = lambda g: [p[:, g * gw:(g + 1) * gw] for p in (q_a, k_a, v_a)]
    pad = jnp.zeros((a.shape[0], NAT_W - OFF_GK - GLA_GATE_RANK), dtype)
    nat = jnp.concatenate([q_g, k_g, v_g, r_g, gg, ga, *qkv(0), gk, pad], axis=1)
    return nat, jnp.concatenate(qkv(1), axis=1), jnp.concatenate(qkv(2), axis=1)


def _kv_rows(src, col_k, n_keep):
    b, dil, n_sub, _ = src.shape
    kv = src[:, :, n_sub - n_keep // dil:, col_k:col_k + 2 * ATTN_GROUP_DIM]
    kv = jnp.swapaxes(kv, 1, 2).astype(F32)
    return kv.reshape(b, n_keep, 2, ATTN_HEADS, ATTN_HEAD_DIM)


def kernel(x_prompt, x_sample, state_gla, cache_kv_w128, cache_kv_w512, cache_kv_w2048, g_norm_mix, w_in, w_gk_up,
           b_gk_up, g_gla_norm, w_branch_gla, w_branch_attn, w_out, g_norm_ffn, w_router, b_router, w_gate_up,
           b_gate_up, w_down, b_down, g_final):
    depth = g_norm_mix.shape[0]
    assert depth == 1, "single-layer trunk"
    batch, seq, d = x_prompt.shape
    bd, dec_seq, _ = x_sample.shape
    assert d == D_MODEL and dec_seq == 1 and seq % ATTN_GROUPS[-1][0] == 0
    n_experts = w_router.shape[-1]
    caches = (cache_kv_w128[0], cache_kv_w512[0], cache_kv_w2048[0])
    gw = ATTN_GROUP_DIM

    w_blocks = _reorder_cols(w_in[0], BF16)
    w_up =jnp.zeros((LANES, GLA_HEADS * GLA_HEAD_K), F32).at[:GLA_GATE_RANK].set(w_gk_up[0])
    w_up_bf = w_up.astype(BF16)
    b_up = b_gk_up[0][None, :]
    g_mix = g_norm_mix[0][None, :]
    g_gla = g_gla_norm[0][None, :]
    wbg, wba, wo = w_branch_gla[0], w_branch_attn[0], w_out[0]
    g_ffn = g_norm_ffn[0][None, :]
    w_r = jnp.zeros((d, LANES), F32).at[:, :n_experts].set(w_router[0])
    wr_hi = w_r.astype(BF16)
    wr_lo = jnp.concatenate([wr_hi, (w_r - wr_hi.astype(F32)).astype(BF16)], axis=1)
    b_r = jnp.zeros((1, LANES), F32).at[0, :n_experts].set(b_router[0])
    bgu, bdn = b_gate_up[0][:, None, :], b_down[0][:, None, :]
    g_fin = g_final[None, :]

    xp = x_prompt.reshape(batch * seq, d)
    nat = _norm_proj(xp, g_mix, w_blocks[0], batch, seq, 1, tm=1024, tn=1024)
    srcs, kv_p = [nat], [_kv_rows(nat, OFF_QKV0 + gw, min(ATTN_GROUPS[0][0], seq))]
    for g in (1, 2):
        src, kv = _norm_proj(xp, g_mix, w_blocks[g], batch, seq, ATTN_GROUPS[g][1], tm=1024, tn=QKV_W,
                             kv_keep=min(ATTN_GROUPS[g][0], seq))
        srcs.append(src)
        kv_p.append(kv)
    nat2d = nat.reshape(batch * seq, NAT_W)
    u_p, st_p = _gla_prompt(nat2d, w_up_bf, b_up, g_gla, batch, seq, tc=512)
    outs, lses = [], []
    for g in range(N_GROUPS):
        col0 = OFF_QKV0 if g == 0 else 0
        o, lse = _attn_prompt(srcs[g], col0, g, tq=min(256, seq // ATTN_GROUPS[g][1]))
        outs.append(o)
        lses.append(lse)
    x1_p, h2_p, route_p, cpad_p = _merge(xp, u_p, nat2d, outs, lses, wbg.astype(BF16), wba.astype(BF16), wo.astype(BF16),
                                 g_ffn, wr_hi, wr_lo, b_r, n_experts, seq, tm=512, ts=MOE_TOKEN_TILE, precise=False)

    xs = x_sample.reshape(bd, d)
    proj_s = jnp.concatenate(_reorder_cols(_norm_proj_precise(xs, g_mix, w_in[0], tn=1024), F32), axis=1)
    proj_s3 = proj_s.reshape(bd, 1, PROJ_W)
    u_s, st_s = _gla_sample(proj_s3, state_gla[0], w_up, b_up, g_gla)
    o_s, l_s = _attn_sample(proj_s3, caches)
    outs_s = [o_s[:, 0, g * gw:(g + 1) * gw].reshape(1, 1, bd, gw) for g in range(N_GROUPS)]
    lses_s = [l_s[:, 0, g * LANES:(g + 1) * LANES].reshape(1, 1, bd, LANES) for g in range(N_GROUPS)]
    x1_s, h2_s, route_s, cpad_s = _merge(xs, u_s.reshape(bd, d), proj_s, outs_s, lses_s, wbg, wba, wo, g_ffn,
                                 w_r, w_r, b_r, n_experts, bd, tm=bd, ts=bd, precise=True)

    ts, tm_e = MOE_TOKEN_TILE, MOE_EXPERT_TILE
    assert bd <= ts
    n_tiles_p = batch * seq // ts
    rows = _tile_rows(ts, n_experts)
    r_pad, nc = _round_up(rows, LANES), rows // ROW_ALIGN
    n_tiles_e = -(-(n_tiles_p * rows + _tile_rows(bd, n_experts) + n_experts * (tm_e - 1)) // tm_e)
    cpad = jnp.concatenate([cpad_p[::8, :n_experts], cpad_s[::8, :n_experts]], axis=0).astype(jnp.int32)
    plan = _moe_plan(cpad, nc, n_experts, tm_e, n_tiles_e)
    pad_rows = lambda a, fill: jnp.concatenate([a, jnp.full((ts - bd, a.shape[1]), fill, a.dtype)], axis=0)
    h2_s, x1_s = pad_rows(h2_s, 0), pad_rows(x1_s, 0)
    route_s = pad_rows(route_s, -1.0)
    xs = _compact(h2_p, route_p, h2_s, route_s, plan, ts, r_pad, nc, n_tiles_e, tm_e)
    new_rows = [proj_s[:, OFF_QKV[g] + gw:OFF_QKV[g] + 3 * gw].reshape(bd, 1, 2, ATTN_HEADS, ATTN_HEAD_DIM)
                .astype(caches[g].dtype) for g in range(N_GROUPS)]
    ys = _experts(xs, plan[2], plan[3], w_gate_up[0], bgu, w_down[0], bdn, tm_e)
    kv_s = [_shift_cache(caches[g], new_rows[g], min(512, caches[g].shape[1]))[None] for g in range(N_GROUPS)]
    y_p = _combine(route_p, x1_p, g_fin, ys, plan, 0, ts, r_pad, nc)
    y_s = _combine(route_s, x1_s, g_fin, ys, plan, n_tiles_p, ts, r_pad, nc)[:bd]

    return (y_p.reshape(batch, seq, d), y_s.reshape(bd, 1, d),
            jnp.swapaxes(st_p, -1, -2)[None], kv_p[0][None], kv_p[1][None], kv_p[2][None],
            st_s[None], kv_s[0], kv_s[1], kv_s[2])
```

```python
import functools

import jax
import jax.numpy as jnp
from jax import lax
from jax.experimental import pallas as pl
from jax.experimental.pallas import tpu as pltpu

F32 = jnp.float32
BF16 = jnp.bfloat16
HIGHEST = lax.Precision.HIGHEST

D_MODEL = 1024
NORM_EPS = 1e-5
GLA_HEADS = 4
GLA_HEAD_K = 128
GLA_HEAD_V = 256
GLA_GATE_RANK = 16
GLA_GATE_NORMALIZER = 16.0
GLA_CHUNK = 64
ATTN_GROUPS = ((128, 1), (512, 4), (2048, 16))
N_GROUPS = 3
ATTN_HEADS = 4
ATTN_HEAD_DIM = 128
ATTN_GROUP_DIM = ATTN_HEADS * ATTN_HEAD_DIM
TOP_K = 4
SWIGLU_LIMIT = 7.0
SWIGLU_ALPHA = 1.702
NEG_BIG = -1e30
LANES = 128
ROW_ALIGN = 8
MOE_TOKEN_TILE = 256
MOE_EXPERT_TILE = 512

OFF_QG, OFF_KG, OFF_VG, OFF_RG = 0, 512, 1024, 2048
OFF_GATE_GLA, OFF_GATE_ATTN = 3072, 4096
OFF_QKV0 = 5120
OFF_GK = 6656
NAT_W = 7168
QKV_W = 3 * ATTN_GROUP_DIM
PROJ_W = NAT_W + 2 * QKV_W
OFF_QKV = (OFF_QKV0, NAT_W, NAT_W + QKV_W)
PROJ_SPLITS = (512, 512, 1024, 1024, GLA_GATE_RANK, 1536, 1536, 1536, 1024, 1024)

VMEM_LIMIT = 48 * 1024 * 1024


def _cparams(*sem):
    return pltpu.CompilerParams(dimension_semantics=sem, vmem_limit_bytes=VMEM_LIMIT)


def _alibi_slope(group, head):
    n = N_GROUPS * ATTN_HEADS
    return 2.0 ** (-8.0 * (group * ATTN_HEADS + head + 1) / n)


def _log_sigmoid(x):
    return jnp.minimum(x, 0.0) - jnp.log1p(jnp.exp(-jnp.abs(x)))


def _dot_nt(a, b):
    return lax.dot_general(a, b, (((1,), (1,)), ((), ())), preferred_element_type=F32)


def _dot(a, b):
    return jnp.dot(a, b, preferred_element_type=F32)


def _dot_hi(a, b):
    return jnp.dot(a, b, preferred_element_type=F32, precision=HIGHEST)


def _split3(x):
    hi = x.astype(BF16)
    r1 = x - hi.astype(F32)
    mid = r1.astype(BF16)
    lo = (r1 - mid.astype(F32)).astype(BF16)
    return hi, mid, lo


def _rmsnorm(x, g):
    return x * lax.rsqrt(jnp.mean(x * x, axis=-1, keepdims=True) + NORM_EPS) * g


def _proj_kernel(x_ref, g_ref, w_ref, o_ref, *rest, dilation, kv_first, kv_row0):
    kv_ref = rest[0] if kv_first is not None else None
    h_ref = rest[-2] if dilation > 1 else rest[-1]

    @pl.when(pl.program_id(1) == 0)
    def _():
        h_ref[...] = _rmsnorm(x_ref[...], g_ref[...]).astype(BF16)

    acc = _dot(h_ref[...], w_ref[...])
    if dilation == 1:
        o_ref[0, 0] = acc.astype(o_ref.dtype)
    else:
        scr = rest[-1]
        n = acc.shape[0] // dilation
        for c in range(acc.shape[1] // LANES):
            cols = slice(c * LANES, (c + 1) * LANES)
            scr[c] = acc[:, cols]
            for r in range(dilation):
                o_ref[0, r, :, cols] = scr[c, pl.ds(r, n, stride=dilation), :].astype(o_ref.dtype)

    if kv_ref is not None:
        @pl.when(pl.program_id(0) % kv_first[1] >= kv_first[0])
        def _():
            rows = kv_ref.shape[1]
            for j in range(2):
                for h in range(ATTN_HEADS):
                    c0 = ATTN_GROUP_DIM + (j * ATTN_HEADS + h) * ATTN_HEAD_DIM
                    kv_ref[0, :, j, h, :] = acc[kv_row0:kv_row0 + rows, c0:c0 + ATTN_HEAD_DIM]


def _norm_proj(x2d, g_row, w_bf, batch, seq, dilation, tm, tn, kv_keep=None):
    m, d = x2d.shape
    n = w_bf.shape[1]
    tps = seq // tm
    out_specs = [pl.BlockSpec((1, dilation, tm // dilation, tn), lambda i, j: (i // tps, 0, i % tps, j))]
    out_shape = [jax.ShapeDtypeStruct((batch, dilation, seq // dilation, n), BF16)]
    kv_first, kv_row0 = None, 0
    if kv_keep is not None:
        assert n == tn == QKV_W
        kv_tiles = max(kv_keep // tm, 1)
        rows = min(kv_keep, tm)
        first = tps - kv_tiles
        kv_first, kv_row0 = (first, tps), tm - rows
        out_specs.append(pl.BlockSpec((1, rows, 2, ATTN_HEADS, ATTN_HEAD_DIM),
                                      lambda i, j: (i // tps, jnp.maximum(i % tps - first, 0), 0, 0, 0)))
        out_shape.append(jax.ShapeDtypeStruct((batch, kv_keep, 2, ATTN_HEADS, ATTN_HEAD_DIM), F32))
    scratch = [pltpu.VMEM((tm, d), BF16)]
    if dilation > 1:
        scratch.append(pltpu.VMEM((tn // LANES, tm, LANES), F32))
    res = pl.pallas_call(
        functools.partial(_proj_kernel, dilation=dilation, kv_first=kv_first, kv_row0=kv_row0),
        grid=(m // tm, n // tn),
        in_specs=[pl.BlockSpec((tm, d), lambda i, j: (i, 0)),
                  pl.BlockSpec((1, d), lambda i, j: (0, 0)),
                  pl.BlockSpec((d, tn), lambda i, j: (0, j))],
        out_specs=out_specs,
        out_shape=out_shape,
        scratch_shapes=scratch,
        compiler_params=_cparams("arbitrary", "arbitrary"),
        name=f"norm_proj_d{dilation}",
    )(x2d, g_row, w_bf)
    return res if kv_keep is not None else res[0]


def _proj_precise_kernel(x_ref, g_ref, w_ref, o_ref):
    o_ref[...] = _dot_hi(_rmsnorm(x_ref[...], g_ref[...]), w_ref[...])


def _norm_proj_precise(x2d, g_row, w_f32, tn):
    m, d = x2d.shape
    n = w_f32.shape[1]
    return pl.pallas_call(
        _proj_precise_kernel,
        grid=(pl.cdiv(n, tn),),
        in_specs=[pl.BlockSpec((m, d), lambda j: (0, 0)),
                  pl.BlockSpec((1, d), lambda j: (0, 0)),
                  pl.BlockSpec((d, tn), lambda j: (0, j))],
        out_specs=pl.BlockSpec((m, tn), lambda j: (0, j)),
        out_shape=jax.ShapeDtypeStruct((m, n), F32),
        compiler_params=_cparams("parallel"),
        name="norm_proj_sample",
    )(x2d, g_row, w_f32)


def _gla_kernel(q_ref, k_ref, v_ref, r_ref, gk_ref, wup_ref, bup_ref, gn_ref,
                u_ref, st_ref, s_scr, o_scr, *, n_chunks):
    t = pl.program_id(1)
    kd, vd = GLA_HEAD_K, GLA_HEAD_V

    @pl.when(t == 0)
    def _():
        s_scr[...] = jnp.zeros_like(s_scr)

    gk = _dot(gk_ref[...], wup_ref[...]) + bup_ref[...]
    log_a = _log_sigmoid(gk) * (1.0 / GLA_GATE_NORMALIZER)
    c_sz = GLA_CHUNK
    row = lax.broadcasted_iota(jnp.int32, (c_sz, c_sz), 0)
    col = lax.broadcasted_iota(jnp.int32, (c_sz, c_sz), 1)
    tri = row >= col
    tri_bf = jnp.where(tri, 1.0, 0.0).astype(BF16)
    scale = kd ** -0.5

    for c in range(n_chunks):
        sl = slice(c * c_sz, (c + 1) * c_sz)
        g_hi, g_mid, g_lo = _split3(log_a[sl])
        b_all = _dot(tri_bf, g_hi) + _dot(tri_bf, g_mid) + _dot(tri_bf, g_lo)
        for h in range(GLA_HEADS):
            kc, vc = slice(h * kd, (h + 1) * kd), slice(h * vd, (h + 1) * vd)
            b = b_all[:, kc]
            b_last = b[c_sz - 1:c_sz, :]
            q = q_ref[sl, kc].astype(F32) * scale
            k = k_ref[sl, kc].astype(F32)
            v_bf = v_ref[sl, vc]
            q_in = (q * jnp.exp(b)).astype(BF16)
            k_in = (k * jnp.exp(-b)).astype(BF16)
            k_out = (k * jnp.exp(b_last - b)).astype(BF16)
            a = jnp.where(tri, _dot_nt(q_in, k_in), 0.0)
            st = s_scr[h]
            o_scr[sl, vc] = _dot_nt(q_in, st.astype(BF16)) + _dot(a.astype(BF16), v_bf)
            s_scr[h] = st * jnp.exp(b_last) + pl.dot(v_bf, k_out, trans_a=True)

    for h in range(GLA_HEADS):
        vc = slice(h * vd, (h + 1) * vd)
        r = r_ref[:, vc].astype(F32)
        u_ref[:, vc] = (_rmsnorm(o_scr[:, vc], gn_ref[...]) * (r * jax.nn.sigmoid(r))).astype(u_ref.dtype)

    @pl.when(t == pl.num_programs(1) - 1)
    def _():
        st_ref[0] = s_scr[...]


def _gla_prompt(proj, w_up_bf, b_up, g_norm, batch, seq, tc):
    nt = seq // tc
    kw, vw = GLA_HEADS * GLA_HEAD_K, GLA_HEADS * GLA_HEAD_V
    rows = lambda b, t: b * nt + t
    kern = functools.partial(_gla_kernel, n_chunks=tc // GLA_CHUNK)
    full = lambda a: pl.BlockSpec(a.shape, lambda b, t: (0, 0))
    return pl.pallas_call(
        kern,
        grid=(batch, nt),
        in_specs=[
            pl.BlockSpec((tc, kw), lambda b, t: (rows(b, t), OFF_QG // kw)),
            pl.BlockSpec((tc, kw), lambda b, t: (rows(b, t), OFF_KG // kw)),
            pl.BlockSpec((tc, vw), lambda b, t: (rows(b, t), OFF_VG // vw)),
            pl.BlockSpec((tc, vw), lambda b, t: (rows(b, t), OFF_RG // vw)),
            pl.BlockSpec((tc, LANES), lambda b, t: (rows(b, t), OFF_GK // LANES)),
            full(w_up_bf), full(b_up), full(g_norm),
        ],
        out_specs=[
            pl.BlockSpec((tc, vw), lambda b, t: (rows(b, t), 0)),
            pl.BlockSpec((1, GLA_HEADS, GLA_HEAD_V, GLA_HEAD_K), lambda b, t: (b, 0, 0, 0)),
        ],
        out_shape=[jax.ShapeDtypeStruct((batch * seq, vw), BF16),
                   jax.ShapeDtypeStruct((batch, GLA_HEADS, GLA_HEAD_V, GLA_HEAD_K), F32)],
        scratch_shapes=[pltpu.VMEM((GLA_HEADS, GLA_HEAD_V, GLA_HEAD_K), F32), pltpu.VMEM((tc, vw), F32)],
        compiler_params=_cparams("parallel", "arbitrary"),
        name="gla_prompt",
    )(proj, proj, proj, proj, proj, w_up_bf, b_up, g_norm)


def _attn_kernel(q_ref, kc_ref, kp_ref, vc_ref, vp_ref, o_ref, l_ref, *, group, dilation, blk, n_qb):
    n = pl.program_id(2)
    e = ATTN_HEAD_DIM
    i_idx = lax.broadcasted_iota(jnp.int32, (blk, 2 * blk), 0)
    j_idx = lax.broadcasted_iota(jnp.int32, (blk, 2 * blk), 1)
    rel = i_idx + blk - j_idx
    band = (rel >= 0) & (rel <= blk)
    dist = (dilation * rel).astype(F32)
    bias = [jnp.where(band, -_alibi_slope(group, h) * dist, NEG_BIG) for h in range(ATTN_HEADS)]
    has_prev = (j_idx >= blk) | (n > 0)
    lane = lax.broadcasted_iota(jnp.int32, (blk, LANES), 1)
    for rr, qb in [(rr, qb) for rr in range(q_ref.shape[1]) for qb in range(n_qb)]:
        rows = slice(qb * blk, (qb + 1) * blk)
        q = q_ref[0, rr, rows, :]
        if qb == 0:
            k_prev, v_prev = kp_ref[0, rr], vp_ref[0, rr]
        else:
            prev = slice((qb - 1) * blk, qb * blk)
            k_prev, v_prev = kc_ref[0, rr, prev, :], vc_ref[0, rr, prev, :]
        kk = jnp.concatenate([k_prev, kc_ref[0, rr, rows, :]], axis=0)
        vv = jnp.concatenate([v_prev, vc_ref[0, rr, rows, :]], axis=0)
        lse_tile = jnp.zeros((blk, LANES), F32)
        for h in range(ATTN_HEADS):
            cols = slice(h * e, (h + 1) * e)
            b_h = jnp.where(has_prev, bias[h], NEG_BIG) if qb == 0 else bias[h]
            s = _dot_nt(q[:, cols], kk[:, cols]) * (e ** -0.5) + b_h
            m = jnp.max(s, axis=-1, keepdims=True)
            p = jnp.exp(s - m)
            den = jnp.sum(p, axis=-1, keepdims=True)
            o = _dot(p.astype(BF16), vv[:, cols]) / den
            o_ref[0, rr, rows, cols] = o.astype(o_ref.dtype)
            lse_tile = jnp.where(lane == h, m + jnp.log(den), lse_tile)
        l_ref[0, rr, rows, :] = lse_tile


def _attn_prompt(src, col0, group, tq, rows_per_step):
    batch, dilation, n_sub, _ = src.shape
    window, dil = ATTN_GROUPS[group]
    assert dil == dilation
    blk = window // dilation
    n_qb = tq // blk
    gw = ATTN_GROUP_DIM
    qc, kc, vc = col0 // gw, col0 // gw + 1, col0 // gw + 2
    n_res = max(1, min(dilation, rows_per_step // tq))
    cur = lambda c: pl.BlockSpec((1, n_res, tq, gw), lambda b, r, n: (b, r, n, c))
    prev = lambda c: pl.BlockSpec((1, n_res, blk, gw), lambda b, r, n: (b, r, jnp.maximum(n * n_qb - 1, 0), c))
    kern = functools.partial(_attn_kernel, group=group, dilation=dilation, blk=blk, n_qb=n_qb)
    out_spec = pl.BlockSpec((1, n_res, tq, gw), lambda b, r, n: (b, r, n, 0))
    return pl.pallas_call(
        kern,
        grid=(batch, dilation // n_res, n_sub // tq),
        in_specs=[cur(qc), cur(kc), prev(kc), cur(vc), prev(vc)],
        out_specs=[out_spec, pl.BlockSpec((1, n_res, tq, LANES), lambda b, r, n: (b, r, n, 0))],
        out_shape=[jax.ShapeDtypeStruct((batch, dilation, n_sub, gw), BF16),
                   jax.ShapeDtypeStruct((batch, dilation, n_sub, LANES), F32)],
        compiler_params=_cparams("parallel", "parallel", "arbitrary"),
        name=f"attn_prompt_g{group}",
    )(src, src, src, src, src)


def _merge_kernel(x_ref, u_ref, gg_ref, ga_ref, o1_ref, o2_ref, o3_ref, l1_ref, l2_ref, l3_ref,
                  wbg_ref, wba_ref, wo_ref, gf_ref, wr_hi_ref, wr_lo_ref, br_ref,
                  x1_ref, h2_ref, route_ref, cpad_ref, *scr, n_experts, dilations, precise, ts):
    tm = x_ref.shape[0]
    scr = list(scr)

    def natural(ref, dilation):
        if dilation == 1:
            return ref[0, 0].astype(F32)
        buf = scr.pop(0)
        n = tm // dilation
        for c in range(buf.shape[0]):
            for r in range(dilation):
                buf[c, pl.ds(r, n, stride=dilation), :] = ref[0, r, :, c * LANES:(c + 1) * LANES].astype(F32)
        return jnp.concatenate([buf[c] for c in range(buf.shape[0])], axis=1)

    o1, o2, o3 = (natural(r, d) for r, d in zip((o1_ref, o2_ref, o3_ref), dilations))
    l1, l2, l3 = (natural(r, d) for r, d in zip((l1_ref, l2_ref, l3_ref), dilations))
    mm = _dot_hi if precise else (lambda a, b: _dot(a.astype(BF16), b))
    lm = jnp.maximum(jnp.maximum(l1, l2), l3)
    e1, e2, e3 = jnp.exp(l1 - lm), jnp.exp(l2 - lm), jnp.exp(l3 - lm)
    inv = 1.0 / (e1 + e2 + e3)
    head = lax.broadcasted_iota(jnp.int32, (LANES, ATTN_GROUP_DIM), 0)
    owner = lax.broadcasted_iota(jnp.int32, (LANES, ATTN_GROUP_DIM), 1) // ATTN_HEAD_DIM
    spread = jnp.where(head == owner, 1.0, 0.0)

    def per_lane(w):
        if precise:
            return _dot_hi(w, spread)
        return _dot(w.astype(BF16), spread.astype(BF16))

    o_att = per_lane(e1 * inv) * o1 + per_lane(e2 * inv) * o2 + per_lane(e3 * inv) * o3
    merged = (jax.nn.sigmoid(gg_ref[...].astype(F32)) * mm(u_ref[...], wbg_ref[...])
              + jax.nn.sigmoid(ga_ref[...].astype(F32)) * mm(o_att, wba_ref[...]))
    x1 = x_ref[...] + mm(merged, wo_ref[...])
    x1_ref[...] = x1
    h2 = _rmsnorm(x1, gf_ref[...])
    h2_ref[...] = h2.astype(BF16)

    if precise:
        logits = _dot_hi(h2, wr_hi_ref[...]) + br_ref[...]
    else:
        h_hi, h_mid, _ = _split3(h2)
        both = _dot(h_hi, wr_lo_ref[...])
        logits = both[:, :LANES] + both[:, LANES:] + _dot(h_mid, wr_hi_ref[...]) + br_ref[...]
    lane = lax.broadcasted_iota(jnp.int32, (ts, LANES), 1).astype(F32)
    ri = lax.broadcasted_iota(jnp.int32, (ts, ts), 0)
    ci = lax.broadcasted_iota(jnp.int32, (ts, ts), 1)
    earlier = jnp.where(ci < ri, 1.0, 0.0).astype(BF16)
    li = lax.broadcasted_iota(jnp.int32, (LANES, LANES), 0)
    lj = lax.broadcasted_iota(jnp.int32, (LANES, LANES), 1)
    before = jnp.where(li < lj, 1.0, 0.0).astype(BF16)
    for s in range(tm // ts):
        rows = slice(s * ts, (s + 1) * ts)
        cur = jnp.where(lane < n_experts, logits[rows, :], -jnp.inf)
        tops, sels = [], []
        for _ in range(TOP_K):
            m = jnp.max(cur, axis=-1, keepdims=True)
            idx = jnp.min(jnp.where(cur == m, lane, float(LANES)), axis=-1, keepdims=True)
            sel = lane == idx
            tops.append(m)
            sels.append(sel)
            cur = jnp.where(sel, -jnp.inf, cur)
        ex = [jnp.exp(m - tops[0]) for m in tops]
        den = ex[0] + ex[1] + ex[2] + ex[3]
        mem = jnp.zeros((ts, LANES), F32)
        for sel in sels:
            mem = jnp.where(sel, 1.0, mem)
        rank = _dot(earlier, mem.astype(BF16))
        count = jnp.sum(mem, axis=0, keepdims=True)
        cpad = jnp.ceil(count * (1.0 / ROW_ALIGN)) * ROW_ALIGN
        cpad8 = jnp.broadcast_to(cpad, (8, LANES))
        seg_start = _dot(cpad8.astype(BF16), before)[0:1]
        pos = seg_start + rank
        route = jnp.zeros((ts, LANES), F32)
        for k in range(TOP_K):
            dest = jnp.sum(jnp.where(sels[k], pos, 0.0), axis=-1, keepdims=True)
            route = jnp.where(lane == float(k), dest, route)
            route = jnp.where(lane == float(TOP_K + k), ex[k] / den, route)
        route_ref[rows, :] = route
        cpad_ref[s * 8:(s + 1) * 8, :] = cpad8


def _merge(x2d, u, gate_src, outs, lses, wbg, wba, wo, g_ffn, wr_hi, wr_lo, b_r, n_experts, seq, tm, ts, precise):
    m, d = x2d.shape
    gw = ATTN_GROUP_DIM
    tps = seq // tm
    dilations = tuple(o.shape[1] for o in outs)
    row = lambda w: pl.BlockSpec((tm, w), lambda i: (i, 0))
    full = lambda a: pl.BlockSpec(a.shape, lambda i: (0, 0))
    grp = lambda dil, w: pl.BlockSpec((1, dil, tm // dil, w), lambda i: (i // tps, 0, i % tps, 0))
    kern = functools.partial(_merge_kernel, n_experts=n_experts, dilations=dilations, precise=precise, ts=ts)
    n_dilated = sum(1 for dil in dilations if dil > 1)
    interleave = lambda w: [pltpu.VMEM((w // LANES, tm, LANES), F32)] * n_dilated
    return pl.pallas_call(
        kern,
        grid=(m // tm,),
        in_specs=[row(d), row(d),
                  pl.BlockSpec((tm, d), lambda i: (i, OFF_GATE_GLA // d)),
                  pl.BlockSpec((tm, d), lambda i: (i, OFF_GATE_ATTN // d)),
                  *[grp(dil, gw) for dil in dilations], *[grp(dil, LANES) for dil in dilations],
                  full(wbg), full(wba), full(wo), full(g_ffn), full(wr_hi), full(wr_lo), full(b_r)],
        out_specs=[row(d), row(d), row(LANES), pl.BlockSpec((tm // ts * 8, LANES), lambda i: (i, 0))],
        out_shape=[jax.ShapeDtypeStruct((m, d), F32), jax.ShapeDtypeStruct((m, d), BF16),
                   jax.ShapeDtypeStruct((m, LANES), F32), jax.ShapeDtypeStruct((m // ts * 8, LANES), F32)],
        scratch_shapes=interleave(gw) + interleave(LANES),
        compiler_params=_cparams("parallel"),
        name="merge_router_sample" if precise else "merge_router",
    )(x2d, u, gate_src, gate_src, *outs, *lses, wbg, wba, wo, g_ffn, wr_hi, wr_lo, b_r)


def _round_up(x, mult):
    return (x + mult - 1) // mult * mult


def _tile_rows(n_tokens, n_experts):
    return _round_up(n_tokens * TOP_K + n_experts * (ROW_ALIGN - 1), ROW_ALIGN)


def _moe_plan(cpad, nc, n_experts, tm_e, n_tiles_max):
    tot = jnp.sum(cpad, axis=0)
    gsize = (tot + tm_e - 1) // tm_e * tm_e
    g_end = jnp.cumsum(gsize)
    g_start = g_end - gsize
    n_used = (g_end[-1] // tm_e).astype(jnp.int32)
    o_start = jnp.cumsum(cpad, axis=0) - cpad
    seg_end = jnp.cumsum(cpad, axis=1)
    seg_start = seg_end - cpad

    row0 = jnp.arange(nc, dtype=jnp.int32) * ROW_ALIGN
    e_of = jnp.sum((seg_end[:, None, :] <= row0[None, :, None]).astype(jnp.int32), axis=-1)
    e_of = jnp.minimum(e_of, n_experts - 1)
    pick = (e_of[:, :, None] == jnp.arange(n_experts)[None, None, :]).astype(jnp.int32)
    base = g_start[None, :] + o_start - seg_start
    dst = jnp.sum(pick * base[:, None, :], axis=-1) + row0[None, :]
    nch = (seg_end[:, -1] // ROW_ALIGN).astype(jnp.int32)
    dst = jnp.where(jnp.arange(nc)[None, :] < nch[:, None], dst, 0).astype(jnp.int32)

    t0 = jnp.arange(n_tiles_max, dtype=jnp.int32) * tm_e
    tile_expert = jnp.minimum(jnp.sum((g_end[None, :] <= t0[:, None]).astype(jnp.int32), axis=-1), n_experts - 1)
    tile_id = jnp.arange(n_tiles_max)
    last = jnp.sum(jnp.where(tile_id == n_used - 1, tile_expert, 0))
    tile_expert = jnp.where(tile_id < n_used, tile_expert, last).astype(jnp.int32)
    gap_start = (g_start + tot).astype(jnp.int32)
    gap_chunks = ((gsize - tot) // ROW_ALIGN).astype(jnp.int32)

    prev = jnp.concatenate([jnp.full((1,), -1, jnp.int32), tile_expert[:-1]])
    first = (tile_expert != prev) & (tile_id < n_used)
    slot = (jnp.cumsum(first.astype(jnp.int32)) - 1) % 2
    e_id = jnp.arange(n_experts)
    later = (e_id[None, :] > e_id[:, None]) & (gsize[None, :] > 0)
    next_of = jnp.min(jnp.where(later, e_id[None, :], n_experts), axis=1)
    next_of = jnp.where(next_of == n_experts, -1, next_of)
    nxt = jnp.sum(jnp.where(tile_expert[:, None] == e_id[None, :], next_of[None, :], 0), axis=1)
    group_info = jnp.stack([first.astype(jnp.int32), slot, nxt], axis=1).reshape(-1).astype(jnp.int32)
    return dst.reshape(-1), nch, tile_expert, n_used.reshape(1), gap_start, gap_chunks, group_info


def _loop(n, fn, unroll=1):
    def body(i, carry):
        for u in range(unroll):
            fn(i * unroll + u)
        return carry

    main = n // unroll if unroll > 1 else n
    lax.fori_loop(0, main, body, 0)
    if unroll > 1:
        lax.fori_loop(main * unroll, n, lambda i, carry: (fn(i), carry)[1], 0)


def _compact_kernel(dst_ref, nch_ref, gap_start_ref, gap_chunks_ref, nu_ref, hp_ref, rp_ref, hs_ref, rs_ref,
                    xs_ref, xc_scr, zero_scr, sem, *, nc):
    s = pl.program_id(0)
    last = pl.num_programs(0) - 1
    is_sample = s == last
    ts, r_pad = hp_ref.shape[0], xc_scr.shape[1]

    def chunk(tile, c):
        slot = tile % 2
        src = xc_scr.at[slot, pl.ds(pl.multiple_of(c * ROW_ALIGN, ROW_ALIGN), ROW_ALIGN)]
        dst = xs_ref.at[pl.ds(pl.multiple_of(dst_ref[tile * nc + c], ROW_ALIGN), ROW_ALIGN)]
        return pltpu.make_async_copy(src, dst, sem.at[slot])

    def drain(tile):
        slot = tile % 2
        one = pltpu.make_async_copy(xc_scr.at[slot, pl.ds(0, ROW_ALIGN)], xs_ref.at[pl.ds(0, ROW_ALIGN)], sem.at[slot])
        _loop(nch_ref[tile], lambda c: one.wait(), unroll=8)

    @pl.when(s >= 2)
    def _():
        drain(s - 2)

    h = jnp.where(is_sample, hs_ref[...], hp_ref[...])
    route = jnp.where(is_sample, rs_ref[...], rp_ref[...])
    dest_t = jnp.transpose(route)
    row_id = lax.broadcasted_iota(jnp.int32, (r_pad, ts), 0).astype(F32)
    perm = jnp.zeros((r_pad, ts), F32)
    for k in range(TOP_K):
        perm = jnp.where(row_id == dest_t[k:k + 1, :], 1.0, perm)
    xc_scr[s % 2] = _dot(perm.astype(BF16), h)
    _loop(nch_ref[s], lambda c: chunk(s, c).start(), unroll=4)

    @pl.when(is_sample)
    def _():
        @pl.when(s >= 1)
        def _():
            drain(s - 1)
        drain(s)

        zero_scr[...] = jnp.zeros_like(zero_scr)
        tile = zero_scr.shape[0]

        def gap(e, j):
            row = pl.multiple_of(gap_start_ref[e] + j * ROW_ALIGN, ROW_ALIGN)
            return pltpu.make_async_copy(zero_scr.at[pl.ds(0, ROW_ALIGN)], xs_ref.at[pl.ds(row, ROW_ALIGN)], sem.at[0])

        def tail(t):
            row = pl.multiple_of(t * tile, tile)
            return pltpu.make_async_copy(zero_scr, xs_ref.at[pl.ds(row, tile)], sem.at[0])

        n_exp = gap_start_ref.shape[0]
        n_tail = xs_ref.shape[0] // tile - nu_ref[0]
        _loop(n_exp, lambda e: _loop(gap_chunks_ref[e], lambda j: gap(e, j).start()))
        _loop(n_tail, lambda t: tail(nu_ref[0] + t).start())
        _loop(n_exp, lambda e: _loop(gap_chunks_ref[e], lambda j: gap(e, j).wait()))
        _loop(n_tail, lambda t: tail(nu_ref[0] + t).wait())


def _compact(h2_p, route_p, h2_s, route_s, plan, ts, r_pad, nc, n_tiles_e, tm_e):
    m, d = h2_p.shape
    n_p = m // ts
    dst, nch, _, n_used, gap_start, gap_chunks, _ = plan
    prompt = lambda w: pl.BlockSpec((ts, w), lambda i, *_: (jnp.minimum(i, n_p - 1), 0))
    sample = lambda w: pl.BlockSpec((ts, w), lambda i, *_: (0, 0))
    return pl.pallas_call(
        functools.partial(_compact_kernel, nc=nc),
        grid_spec=pltpu.PrefetchScalarGridSpec(
            num_scalar_prefetch=5,
            grid=(n_p + 1,),
            in_specs=[prompt(d), prompt(LANES), sample(d), sample(LANES)],
            out_specs=pl.BlockSpec(memory_space=pl.ANY),
            scratch_shapes=[pltpu.VMEM((2, r_pad, d), F32), pltpu.VMEM((tm_e, d), F32),
                            pltpu.SemaphoreType.DMA((2,))]),
        out_shape=jax.ShapeDtypeStruct((n_tiles_e * tm_e, d), F32),
        compiler_params=_cparams("arbitrary"),
        name="moe_compact",
    )(dst, nch, gap_start, gap_chunks, n_used, h2_p, route_p, h2_s, route_s)


def _expert_kernel(te_ref, nu_ref, nx_ref, x_ref, wgu_ref, bgu_ref, wd_ref, bd_ref, o_ref,
                   wgu_f32, wd_f32, wgu_bf, wd_bf, sem, *, d_ff, sub):
    t = pl.program_id(0)
    used = t < nu_ref[0]
    nf = 3

    def fetch(expert, slot):
        return (pltpu.make_async_copy(wgu_ref.at[expert], wgu_f32.at[slot], sem.at[0, slot]),
                pltpu.make_async_copy(wd_ref.at[expert], wd_f32.at[slot], sem.at[1, slot]))

    @pl.when(used)
    def _():
        @pl.when(nx_ref[nf * t] == 1)
        def _():
            slot, nxt = nx_ref[nf * t + 1], nx_ref[nf * t + 2]

            @pl.when(t == 0)
            def _():
                for cp in fetch(te_ref[0], slot):
                    cp.start()

            for cp in fetch(te_ref[t], slot):
                cp.wait()
            step = 128
            for i in range(wgu_bf.shape[0] // step):
                rows = slice(i * step, (i + 1) * step)
                wgu_bf[rows, :] = wgu_f32[slot, rows, :].astype(BF16)
            for i in range(wd_bf.shape[0] // step):
                rows = slice(i * step, (i + 1) * step)
                wd_bf[rows, :] = wd_f32[slot, rows, :].astype(BF16)

            @pl.when(nxt >= 0)
            def _():
                for cp in fetch(nxt, 1 - slot):
                    cp.start()

        for s in range(x_ref.shape[0] // sub):
            rows = slice(s * sub, (s + 1) * sub)
            gu = _dot(x_ref[rows, :].astype(BF16), wgu_bf[...]) + bgu_ref[0]
            gate = jnp.minimum(gu[:, :d_ff], SWIGLU_LIMIT)
            up = jnp.clip(gu[:, d_ff:], -SWIGLU_LIMIT, SWIGLU_LIMIT)
            act = (up + 1.0) * gate * jax.nn.sigmoid(SWIGLU_ALPHA * gate)
            o_ref[rows, :] = _dot(act.astype(BF16), wd_bf[...]) + bd_ref[0]

    @pl.when(jnp.logical_not(used))
    def _():
        o_ref[...] = jnp.zeros_like(o_ref)


def _experts(xs, tile_expert, n_used, group_info, wgu, bgu, wd, bd, tm_e):
    n_rows, d = xs.shape
    _, _, two_ff = wgu.shape
    d_ff = two_ff // 2
    kern = functools.partial(_expert_kernel, d_ff=d_ff, sub=tm_e)
    by_expert = lambda shape: pl.BlockSpec(shape, lambda t, te, nu, nx: (te[t], 0, 0))
    hbm = pl.BlockSpec(memory_space=pl.ANY)
    return pl.pallas_call(
        kern,
        grid_spec=pltpu.PrefetchScalarGridSpec(
            num_scalar_prefetch=3,
            grid=(n_rows // tm_e,),
            in_specs=[pl.BlockSpec((tm_e, d), lambda t, te, nu, nx: (jnp.minimum(t, nu[0] - 1), 0)),
                      hbm, by_expert((1, 1, two_ff)), hbm, by_expert((1, 1, d))],
            out_specs=pl.BlockSpec((tm_e, d), lambda t, te, nu, nx: (t, 0)),
            scratch_shapes=[pltpu.VMEM((2, d, two_ff), F32), pltpu.VMEM((2, d_ff, d), F32),
                            pltpu.VMEM((d, two_ff), BF16), pltpu.VMEM((d_ff, d), BF16),
                            pltpu.SemaphoreType.DMA((2, 2))]),
        out_shape=jax.ShapeDtypeStruct((n_rows, d), F32),
        compiler_params=pltpu.CompilerParams(dimension_semantics=("arbitrary",), vmem_limit_bytes=56 * 1024 * 1024),
        name="moe_experts",
    )(tile_expert, n_used, group_info, xs, wgu, bgu, wd, bd)


def _shift_kernel(c_ref, new_ref, o_ref):
    j = pl.program_id(1)
    last = pl.num_programs(1) - 1
    bn = o_ref.shape[1]

    @pl.when(j != last)
    def _():
        o_ref[...] = c_ref[...]

    @pl.when(j == last)
    def _():
        o_ref[0, 0:bn - 1] = c_ref[0, 1:bn]
        o_ref[0, bn - 1:bn] = new_ref[0]


def _shift_cache(cache, new_row, bn):
    bd, n = cache.shape[:2]
    tail = cache.shape[2:]
    zeros = (0,) * len(tail)
    return pl.pallas_call(
        _shift_kernel,
        grid=(bd, n // bn),
        in_specs=[pl.BlockSpec(tuple(pl.Element(s) for s in (1, bn) + tail),
                               lambda b, j: (b, jnp.minimum(j * bn + 1, n - bn)) + zeros),
                  pl.BlockSpec((1, 1) + tail, lambda b, j: (b, 0) + zeros)],
        out_specs=pl.BlockSpec((1, bn) + tail, lambda b, j: (b, j) + zeros),
        out_shape=jax.ShapeDtypeStruct(cache.shape, cache.dtype),
        compiler_params=_cparams("parallel", "arbitrary"),
        name="cache_shift",
    )(cache, new_row)


def _combine_kernel(dst_ref, nch_ref, route_ref, x1_ref, gfin_ref, ys_ref, y_ref, yc_scr, sem, *, nc, tile0):
    i = pl.program_id(0)

    def chunk(step, c):
        slot = step % 2
        tile = step + tile0
        src = ys_ref.at[pl.ds(pl.multiple_of(dst_ref[tile * nc + c], ROW_ALIGN), ROW_ALIGN)]
        dst = yc_scr.at[slot, pl.ds(pl.multiple_of(c * ROW_ALIGN, ROW_ALIGN), ROW_ALIGN)]
        return pltpu.make_async_copy(src, dst, sem.at[slot])

    def fetch(step):
        _loop(nch_ref[step + tile0], lambda c: chunk(step, c).start(), unroll=4)

    @pl.when(i == 0)
    def _():
        yc_scr[...] = jnp.zeros_like(yc_scr)
        fetch(i)

    @pl.when(i + 1 < pl.num_programs(0))
    def _():
        fetch(i + 1)

    one = pltpu.make_async_copy(ys_ref.at[pl.ds(0, ROW_ALIGN)], yc_scr.at[i % 2, pl.ds(0, ROW_ALIGN)], sem.at[i % 2])
    _loop(nch_ref[i + tile0], lambda c: one.wait(), unroll=8)

    ts, r_pad = x1_ref.shape[0], yc_scr.shape[1]
    route = route_ref[...]
    col_id = lax.broadcasted_iota(jnp.int32, (ts, r_pad), 1).astype(F32)
    weights = jnp.zeros((ts, r_pad), F32)
    for k in range(TOP_K):
        weights = jnp.where(col_id == route[:, k:k + 1], route[:, TOP_K + k:TOP_K + k + 1], weights)
    moe = _dot(weights.astype(BF16), yc_scr[i % 2].astype(BF16))
    y_ref[...] = _rmsnorm(x1_ref[...] + moe, gfin_ref[...])


def _combine(route, x1, g_final, ys, plan, tile0, ts, r_pad, nc):
    m, d = x1.shape
    dst, nch = plan[0], plan[1]
    return pl.pallas_call(
        functools.partial(_combine_kernel, nc=nc, tile0=tile0),
        grid_spec=pltpu.PrefetchScalarGridSpec(
            num_scalar_prefetch=2,
            grid=(m // ts,),
            in_specs=[pl.BlockSpec((ts, LANES), lambda i, *_: (i, 0)),
                      pl.BlockSpec((ts, d), lambda i, *_: (i, 0)),
                      pl.BlockSpec((1, d), lambda i, *_: (0, 0)),
                      pl.BlockSpec(memory_space=pl.ANY)],
            out_specs=pl.BlockSpec((ts, d), lambda i, *_: (i, 0)),
            scratch_shapes=[pltpu.VMEM((2, r_pad, d), F32), pltpu.SemaphoreType.DMA((2,))]),
        out_shape=jax.ShapeDtypeStruct((m, d), F32),
        compiler_params=_cparams("arbitrary"),
        name="moe_combine",
    )(dst, nch, route, x1, g_final, ys)


def _column(x_row):
    return jnp.transpose(jnp.broadcast_to(x_row, (LANES, LANES)))


def _gla_step_kernel(p_ref, s_ref, wup_ref, bup_ref, gn_ref, u_ref, so_ref):
    row = p_ref[0]
    gk8 = jnp.broadcast_to(row[:, OFF_GK:OFF_GK + LANES], (8, LANES))
    gk = _dot_hi(gk8, wup_ref[...]) + bup_ref[...]
    decay = jnp.exp(_log_sigmoid(gk[0:1, :]) * (1.0 / GLA_GATE_NORMALIZER))
    kd, vd = GLA_HEAD_K, GLA_HEAD_V
    wide = lambda c: jnp.concatenate([c, c], axis=1)
    for h in range(GLA_HEADS):
        q = row[:, OFF_QG + h * kd:OFF_QG + (h + 1) * kd] * (kd ** -0.5)
        k = row[:, OFF_KG + h * kd:OFF_KG + (h + 1) * kd]
        v = row[:, OFF_VG + h * vd:OFF_VG + (h + 1) * vd]
        r = row[:, OFF_RG + h * vd:OFF_RG + (h + 1) * vd]
        s_new = s_ref[0, h] * wide(_column(decay[:, h * kd:(h + 1) * kd])) + wide(_column(k)) * v
        so_ref[0, h] = s_new
        o = jnp.sum(wide(_column(q)) * s_new, axis=0, keepdims=True)
        u_ref[0, :, h * vd:(h + 1) * vd] = _rmsnorm(o, gn_ref[...]) * (r * jax.nn.sigmoid(r))


def _gla_sample(proj_s3, state, w_up, b_up, g_norm):
    bd = proj_s3.shape[0]
    full = lambda a: pl.BlockSpec(a.shape, lambda b: (0,) * a.ndim)
    st_spec = pl.BlockSpec((1, GLA_HEADS, GLA_HEAD_K, GLA_HEAD_V), lambda b: (b, 0, 0, 0))
    return pl.pallas_call(
        _gla_step_kernel,
        grid=(bd,),
        in_specs=[pl.BlockSpec((1, 1, PROJ_W), lambda b: (b, 0, 0)), st_spec,
                  full(w_up), full(b_up), full(g_norm)],
        out_specs=[pl.BlockSpec((1, 1, GLA_HEADS * GLA_HEAD_V), lambda b: (b, 0, 0)), st_spec],
        out_shape=[jax.ShapeDtypeStruct((bd, 1, GLA_HEADS * GLA_HEAD_V), F32),
                   jax.ShapeDtypeStruct(state.shape, state.dtype)],
        compiler_params=_cparams("parallel"),
        name="gla_sample",
    )(proj_s3, state, w_up, b_up, g_norm)


def _attn_step_kernel(p_ref, c1_ref, c2_ref, c3_ref, o_ref, l_ref):
    row = p_ref[0]
    gw, e = ATTN_GROUP_DIM, ATTN_HEAD_DIM
    for g, c_ref in enumerate((c1_ref, c2_ref, c3_ref)):
        window, dilation = ATTN_GROUPS[g]
        blk = window // dilation
        back = (dilation * (blk - lax.broadcasted_iota(jnp.int32, (blk, 1), 0))).astype(F32)
        lane = lax.broadcasted_iota(jnp.int32, (1, LANES), 1)
        lse_row = jnp.zeros((1, LANES), F32)
        for h in range(ATTN_HEADS):
            lo = OFF_QKV[g] + h * e
            q = row[:, lo:lo + e]
            k_new = row[:, lo + gw:lo + gw + e]
            v_new = row[:, lo + 2 * gw:lo + 2 * gw + e]
            k_c = c_ref[0, :, 0, h, :]
            v_c = c_ref[0, :, 1, h, :]
            s = jnp.sum(k_c * q, axis=-1, keepdims=True) * (e ** -0.5) - _alibi_slope(g, h) * back
            s0 = jnp.sum(k_new * q, axis=-1, keepdims=True) * (e ** -0.5)
            m = jnp.maximum(jnp.max(s, axis=0, keepdims=True), s0)
            p = jnp.exp(s - m)
            p0 = jnp.exp(s0 - m)
            den = jnp.sum(p, axis=0, keepdims=True) + p0
            o = (jnp.sum(p * v_c, axis=0, keepdims=True) + p0 * v_new) / den
            cols = slice(g * gw + h * e, g * gw + (h + 1) * e)
            o_ref[0, :, cols] = o
            lse_row = jnp.where(lane == h, m + jnp.log(den), lse_row)
        l_ref[0, :, g * LANES:(g + 1) * LANES] = lse_row


def _attn_sample(proj_s3, caches):
    bd = proj_s3.shape[0]
    gw = ATTN_GROUP_DIM
    views, specs = [], []
    for g, cache in enumerate(caches):
        window, dilation = ATTN_GROUPS[g]
        n_buf = cache.shape[1]
        assert n_buf == window, "cache must hold exactly one window of rows"
        blk = n_buf // dilation
        views.append(cache.reshape(bd, blk, dilation, 2, ATTN_HEADS, ATTN_HEAD_DIM))
        specs.append(pl.BlockSpec((1, blk, None, 2, ATTN_HEADS, ATTN_HEAD_DIM), lambda b: (b, 0, 0, 0, 0, 0)))
    out_spec = pl.BlockSpec((1, 1, N_GROUPS * gw), lambda b: (b, 0, 0))
    return pl.pallas_call(
        _attn_step_kernel,
        grid=(bd,),
        in_specs=[pl.BlockSpec((1, 1, PROJ_W), lambda b: (b, 0, 0))] + specs,
        out_specs=[out_spec, pl.BlockSpec((1, 1, N_GROUPS * LANES), lambda b: (b, 0, 0))],
        out_shape=[jax.ShapeDtypeStruct((bd, 1, N_GROUPS * gw), F32),
                   jax.ShapeDtypeStruct((bd, 1, N_GROUPS * LANES), F32)],
        compiler_params=_cparams("parallel"),
        name="attn_sample",
    )(proj_s3, *views)


def _reorder_cols(a, dtype):
    offs, acc = [], 0
    for s in PROJ_SPLITS[:-1]:
        acc += s
        offs.append(acc)
    q_g, k_g, v_g, r_g, gk, q_a, k_a, v_a, gg, ga = [p.astype(dtype) for p in jnp.split(a, offs, axis=1)]
    gw = ATTN_GROUP_DIM
    qkv = lambda g: [p[:, g * gw:(g + 1) * gw] for p in (q_a, k_a, v_a)]
    pad = jnp.zeros((a.shape[0], NAT_W - OFF_GK - GLA_GATE_RANK), dtype)
    nat = jnp.concatenate([q_g, k_g, v_g, r_g, gg, ga, *qkv(0), gk, pad], axis=1)
    return nat, jnp.concatenate(qkv(1), axis=1), jnp.concatenate(qkv(2), axis=1)


def _kv_rows(src, col_k, n_keep):
    b, dil, n_sub, _ = src.shape
    kv = src[:, :, n_sub - n_keep // dil:, col_k:col_k + 2 * ATTN_GROUP_DIM]
    kv = jnp.swapaxes(kv, 1, 2).astype(F32)
    return kv.reshape(b, n_keep, 2, ATTN_HEADS, ATTN_HEAD_DIM)


def kernel(x_prompt, x_sample, state_gla, cache_kv_w128, cache_kv_w512, cache_kv_w2048, g_norm_mix, w_in, w_gk_up,
           b_gk_up, g_gla_norm, w_branch_gla, w_branch_attn, w_out, g_norm_ffn, w_router, b_router, w_gate_up,
           b_gate_up, w_down, b_down, g_final):
    depth = g_norm_mix.shape[0]
    assert depth == 1, "single-layer trunk"
    batch, seq, d = x_prompt.shape
    bd, dec_seq, _ = x_sample.shape
    assert d == D_MODEL and dec_seq == 1 and seq % ATTN_GROUPS[-1][0] == 0
    n_experts = w_router.shape[-1]
    caches = (cache_kv_w128[0], cache_kv_w512[0], cache_kv_w2048[0])
    gw = ATTN_GROUP_DIM

    w_blocks = _reorder_cols(w_in[0], BF16)
    w_up =jnp.zeros((LANES, GLA_HEADS * GLA_HEAD_K), F32).at[:GLA_GATE_RANK].set(w_gk_up[0])
    w_up_bf = w_up.astype(BF16)
    b_up = b_gk_up[0][None, :]
    g_mix = g_norm_mix[0][None, :]
    g_gla = g_gla_norm[0][None, :]
    wbg, wba, wo = w_branch_gla[0], w_branch_attn[0], w_out[0]
    g_ffn = g_norm_ffn[0][None, :]
    w_r = jnp.zeros((d, LANES), F32).at[:, :n_experts].set(w_router[0])
    wr_hi = w_r.astype(BF16)
    wr_lo = jnp.concatenate([wr_hi, (w_r - wr_hi.astype(F32)).astype(BF16)], axis=1)
    b_r = jnp.zeros((1, LANES), F32).at[0, :n_experts].set(b_router[0])
    bgu, bdn = b_gate_up[0][:, None, :], b_down[0][:, None, :]
    g_fin = g_final[None, :]

    xp = x_prompt.reshape(batch * seq, d)
    nat = _norm_proj(xp, g_mix, w_blocks[0], batch, seq, 1, tm=1024, tn=NAT_W // 2)
    srcs, kv_p = [nat], [_kv_rows(nat, OFF_QKV0 + gw, min(ATTN_GROUPS[0][0], seq))]
    for g in (1, 2):
        src, kv = _norm_proj(xp, g_mix, w_blocks[g], batch, seq, ATTN_GROUPS[g][1], tm=1024, tn=QKV_W,
                             kv_keep=min(ATTN_GROUPS[g][0], seq))
        srcs.append(src)
        kv_p.append(kv)
    nat2d = nat.reshape(batch * seq, NAT_W)
    u_p, st_p = _gla_prompt(nat2d, w_up_bf, b_up, g_gla, batch, seq, tc=512)
    outs, lses = [], []
    for g in range(N_GROUPS):
        col0 = OFF_QKV0 if g == 0 else 0
        o, lse = _attn_prompt(srcs[g], col0, g, tq=min(512, seq // ATTN_GROUPS[g][1]), rows_per_step=512)
        outs.append(o)
        lses.append(lse)
    x1_p, h2_p, route_p, cpad_p = _merge(xp, u_p, nat2d, outs, lses, wbg.astype(BF16), wba.astype(BF16), wo.astype(BF16),
                                 g_ffn, wr_hi, wr_lo, b_r, n_experts, seq, tm=512, ts=MOE_TOKEN_TILE, precise=False)

    xs = x_sample.reshape(bd, d)
    proj_s = jnp.concatenate(_reorder_cols(_norm_proj_precise(xs, g_mix, w_in[0], tn=1024), F32), axis=1)
    proj_s3 = proj_s.reshape(bd, 1, PROJ_W)
    u_s, st_s = _gla_sample(proj_s3, state_gla[0], w_up, b_up, g_gla)
    o_s, l_s = _attn_sample(proj_s3, caches)
    outs_s = [o_s[:, 0, g * gw:(g + 1) * gw].reshape(1, 1, bd, gw) for g in range(N_GROUPS)]
    lses_s = [l_s[:, 0, g * LANES:(g + 1) * LANES].reshape(1, 1, bd, LANES) for g in range(N_GROUPS)]
    x1_s, h2_s, route_s, cpad_s = _merge(xs, u_s.reshape(bd, d), proj_s, outs_s, lses_s, wbg, wba, wo, g_ffn,
                                 w_r, w_r, b_r, n_experts, bd, tm=bd, ts=bd, precise=True)

    ts, tm_e = MOE_TOKEN_TILE, MOE_EXPERT_TILE
    assert bd <= ts
    n_tiles_p = batch * seq // ts
    rows = _tile_rows(ts, n_experts)
    r_pad, nc = _round_up(rows, LANES), rows // ROW_ALIGN
    n_tiles_e = -(-(n_tiles_p * rows + _tile_rows(bd, n_experts) + n_experts * (tm_e - 1)) // tm_e)
    cpad = jnp.concatenate([cpad_p[::8, :n_experts], cpad_s[::8, :n_experts]], axis=0).astype(jnp.int32)
    plan = _moe_plan(cpad, nc, n_experts, tm_e, n_tiles_e)
    pad_rows = lambda a, fill: jnp.concatenate([a, jnp.full((ts - bd, a.shape[1]), fill, a.dtype)], axis=0)
    h2_s, x1_s = pad_rows(h2_s, 0), pad_rows(x1_s, 0)
    route_s = pad_rows(route_s, -1.0)
    xs = _compact(h2_p, route_p, h2_s, route_s, plan, ts, r_pad, nc, n_tiles_e, tm_e)
    new_rows = [proj_s[:, OFF_QKV[g] + gw:OFF_QKV[g] + 3 * gw].reshape(bd, 1, 2, ATTN_HEADS, ATTN_HEAD_DIM)
                .astype(caches[g].dtype) for g in range(N_GROUPS)]
    ys = _experts(xs, plan[2], plan[3], plan[6], w_gate_up[0], bgu, w_down[0], bdn, tm_e)
    kv_s = [_shift_cache(caches[g], new_rows[g], min(512, caches[g].shape[1]))[None] for g in range(N_GROUPS)]
    y_p = _combine(route_p, x1_p, g_fin, ys, plan, 0, ts, r_pad, nc)
    y_s = _combine(route_s, x1_s, g_fin, ys, plan, n_tiles_p, ts, r_pad, nc)[:bd]

    return (y_p.reshape(batch, seq, d), y_s.reshape(bd, 1, d),
            jnp.swapaxes(st_p, -1, -2)[None], kv_p[0][None], kv_p[1][None], kv_p[2][None],
            st_s[None], kv_s[0], kv_s[1], kv_s[2])
```

```python
import functools

import jax
import jax.numpy as jnp
from jax import lax
from jax.experimental import pallas as pl
from jax.experimental.pallas import tpu as pltpu

F32 = jnp.float32
BF16 = jnp.bfloat16
HIGHEST = lax.Precision.HIGHEST

D_MODEL = 1024
NORM_EPS = 1e-5
GLA_HEADS = 4
GLA_HEAD_K = 128
GLA_HEAD_V = 256
GLA_GATE_RANK = 16
GLA_GATE_NORMALIZER = 16.0
GLA_CHUNK = 64
ATTN_GROUPS = ((128, 1), (512, 4), (2048, 16))
N_GROUPS = 3
ATTN_HEADS = 4
ATTN_HEAD_DIM = 128
ATTN_GROUP_DIM = ATTN_HEADS * ATTN_HEAD_DIM
TOP_K = 4
SWIGLU_LIMIT = 7.0
SWIGLU_ALPHA = 1.702
NEG_BIG = -1e30
LANES = 128
ROW_ALIGN = 8
MOE_TOKEN_TILE = 256
MOE_EXPERT_TILE = 512

OFF_QG, OFF_KG, OFF_VG, OFF_RG = 0, 512, 1024, 2048
OFF_GATE_GLA, OFF_GATE_ATTN = 3072, 4096
OFF_QKV0 = 5120
OFF_GK = 6656
NAT_W = 7168
QKV_W = 3 * ATTN_GROUP_DIM
PROJ_W = NAT_W + 2 * QKV_W
OFF_QKV = (OFF_QKV0, NAT_W, NAT_W + QKV_W)
PROJ_SPLITS = (512, 512, 1024, 1024, GLA_GATE_RANK, 1536, 1536, 1536, 1024, 1024)

VMEM_LIMIT = 48 * 1024 * 1024


def _cparams(*sem):
    return pltpu.CompilerParams(dimension_semantics=sem, vmem_limit_bytes=VMEM_LIMIT)


def _alibi_slope(group, head):
    n = N_GROUPS * ATTN_HEADS
    return 2.0 ** (-8.0 * (group * ATTN_HEADS + head + 1) / n)


def _log_sigmoid(x):
    return jnp.minimum(x, 0.0) - jnp.log1p(jnp.exp(-jnp.abs(x)))


def _dot_nt(a, b):
    return lax.dot_general(a, b, (((1,), (1,)), ((), ())), preferred_element_type=F32)


def _dot(a, b):
    return jnp.dot(a, b, preferred_element_type=F32)


def _dot_hi(a, b):
    return jnp.dot(a, b, preferred_element_type=F32, precision=HIGHEST)


def _split3(x):
    hi = x.astype(BF16)
    r1 = x - hi.astype(F32)
    mid = r1.astype(BF16)
    lo = (r1 - mid.astype(F32)).astype(BF16)
    return hi, mid, lo


def _rmsnorm(x, g):
    return x * lax.rsqrt(jnp.mean(x * x, axis=-1, keepdims=True) + NORM_EPS) * g


def _proj_kernel(x_ref, g_ref, w_ref, o_ref, *rest, dilation, kv_first, kv_row0):
    kv_ref = rest[0] if kv_first is not None else None
    h_ref = rest[-2] if dilation > 1 else rest[-1]

    @pl.when(pl.program_id(1) == 0)
    def _():
        h_ref[...] = _rmsnorm(x_ref[...], g_ref[...]).astype(BF16)

    acc = _dot(h_ref[...], w_ref[...])
    if dilation == 1:
        o_ref[0, 0] = acc.astype(o_ref.dtype)
    else:
        scr = rest[-1]
        n = acc.shape[0] // dilation
        for c in range(acc.shape[1] // LANES):
            cols = slice(c * LANES, (c + 1) * LANES)
            scr[c] = acc[:, cols]
            for r in range(dilation):
                o_ref[0, r, :, cols] = scr[c, pl.ds(r, n, stride=dilation), :].astype(o_ref.dtype)

    if kv_ref is not None:
        @pl.when(pl.program_id(0) % kv_first[1] >= kv_first[0])
        def _():
            rows = kv_ref.shape[1]
            for j in range(2):
                for h in range(ATTN_HEADS):
                    c0 = ATTN_GROUP_DIM + (j * ATTN_HEADS + h) * ATTN_HEAD_DIM
                    kv_ref[0, :, j, h, :] = acc[kv_row0:kv_row0 + rows, c0:c0 + ATTN_HEAD_DIM]


def _norm_proj(x2d, g_row, w_bf, batch, seq, dilation, tm, tn, kv_keep=None):
    m, d = x2d.shape
    n = w_bf.shape[1]
    tps = seq // tm
    out_specs = [pl.BlockSpec((1, dilation, tm // dilation, tn), lambda i, j: (i // tps, 0, i % tps, j))]
    out_shape = [jax.ShapeDtypeStruct((batch, dilation, seq // dilation, n), BF16)]
    kv_first, kv_row0 = None, 0
    if kv_keep is not None:
        assert n == tn == QKV_W
        kv_tiles = max(kv_keep // tm, 1)
        rows = min(kv_keep, tm)
        first = tps - kv_tiles
        kv_first, kv_row0 = (first, tps), tm - rows
        out_specs.append(pl.BlockSpec((1, rows, 2, ATTN_HEADS, ATTN_HEAD_DIM),
                                      lambda i, j: (i // tps, jnp.maximum(i % tps - first, 0), 0, 0, 0)))
        out_shape.append(jax.ShapeDtypeStruct((batch, kv_keep, 2, ATTN_HEADS, ATTN_HEAD_DIM), F32))
    scratch = [pltpu.VMEM((tm, d), BF16)]
    if dilation > 1:
        scratch.append(pltpu.VMEM((tn // LANES, tm, LANES), F32))
    res = pl.pallas_call(
        functools.partial(_proj_kernel, dilation=dilation, kv_first=kv_first, kv_row0=kv_row0),
        grid=(m // tm, n // tn),
        in_specs=[pl.BlockSpec((tm, d), lambda i, j: (i, 0)),
                  pl.BlockSpec((1, d), lambda i, j: (0, 0)),
                  pl.BlockSpec((d, tn), lambda i, j: (0, j))],
        out_specs=out_specs,
        out_shape=out_shape,
        scratch_shapes=scratch,
        compiler_params=_cparams("arbitrary", "arbitrary"),
        name=f"norm_proj_d{dilation}",
    )(x2d, g_row, w_bf)
    return res if kv_keep is not None else res[0]


def _w_in_kernel(w_ref, nat_ref, d1_ref, d2_ref):
    offs, acc = [0], 0
    for s in PROJ_SPLITS[:-1]:
        acc += s
        offs.append(acc)
    q_g, k_g, v_g, r_g, gk, q_a, k_a, v_a, gg, ga = offs
    gw = ATTN_GROUP_DIM

    def put(o_ref, dst, src, width):
        o_ref[:, dst:dst + width] = jnp.transpose(w_ref[src:src + width, :]).astype(BF16)

    put(nat_ref, OFF_QG, q_g, gk - q_g)
    put(nat_ref, OFF_GATE_GLA, gg, 2 * D_MODEL)
    for i, src in enumerate((q_a, k_a, v_a)):
        put(nat_ref, OFF_QKV0 + i * gw, src, gw)
        put(d1_ref, i * gw, src + gw, gw)
        put(d2_ref, i * gw, src + 2 * gw, gw)
    tr = w_ref.shape[1]
    lane = lax.broadcasted_iota(jnp.int32, (tr, LANES), 1)
    low = jnp.where(lane < GLA_GATE_RANK, jnp.transpose(w_ref[gk:gk + LANES, :]), 0.0)
    nat_ref[:, OFF_GK:OFF_GK + LANES] = low.astype(BF16)
    nat_ref[:, OFF_GK + LANES:] = jnp.zeros((tr, NAT_W - OFF_GK - LANES), BF16)


def _prep_w_in(w_t, tr):
    n, k = w_t.shape
    return pl.pallas_call(
        _w_in_kernel,
        grid=(k // tr,),
        in_specs=[pl.BlockSpec((n, tr), lambda i: (0, i))],
        out_specs=[pl.BlockSpec((tr, NAT_W), lambda i: (i, 0)), pl.BlockSpec((tr, QKV_W), lambda i: (i, 0)),
                   pl.BlockSpec((tr, QKV_W), lambda i: (i, 0))],
        out_shape=[jax.ShapeDtypeStruct((k, NAT_W), BF16), jax.ShapeDtypeStruct((k, QKV_W), BF16),
                   jax.ShapeDtypeStruct((k, QKV_W), BF16)],
        compiler_params=_cparams("parallel"),
        name="w_in_prep",
    )(w_t)


def _proj_precise_kernel(x_ref, g_ref, w_ref, o_ref):
    h = _rmsnorm(x_ref[...], g_ref[...])
    o_ref[...] = lax.dot_general(h, w_ref[...], (((1,), (1,)), ((), ())), preferred_element_type=F32,
                                 precision=HIGHEST)


def _norm_proj_precise(x2d, g_row, w_t, tn):
    m, d = x2d.shape
    n = w_t.shape[0]
    return pl.pallas_call(
        _proj_precise_kernel,
        grid=(pl.cdiv(n, tn),),
        in_specs=[pl.BlockSpec((m, d), lambda j: (0, 0)),
                  pl.BlockSpec((1, d), lambda j: (0, 0)),
                  pl.BlockSpec((tn, d), lambda j: (j, 0))],
        out_specs=pl.BlockSpec((m, tn), lambda j: (0, j)),
        out_shape=jax.ShapeDtypeStruct((m, n), F32),
        compiler_params=_cparams("parallel"),
        name="norm_proj_sample",
    )(x2d, g_row, w_t)


def _gla_kernel(q_ref, k_ref, v_ref, r_ref, gk_ref, wup_ref, bup_ref, gn_ref,
                u_ref, st_ref, s_scr, o_scr, *, n_chunks):
    t = pl.program_id(1)
    kd, vd = GLA_HEAD_K, GLA_HEAD_V

    @pl.when(t == 0)
    def _():
        s_scr[...] = jnp.zeros_like(s_scr)

    gk = _dot(gk_ref[...], wup_ref[...]) + bup_ref[...]
    log_a = _log_sigmoid(gk) * (1.0 / GLA_GATE_NORMALIZER)
    c_sz = GLA_CHUNK
    row = lax.broadcasted_iota(jnp.int32, (c_sz, c_sz), 0)
    col = lax.broadcasted_iota(jnp.int32, (c_sz, c_sz), 1)
    tri = row >= col
    tri_bf = jnp.where(tri, 1.0, 0.0).astype(BF16)
    scale = kd ** -0.5

    for c in range(n_chunks):
        sl = slice(c * c_sz, (c + 1) * c_sz)
        g_hi, g_mid, g_lo = _split3(log_a[sl])
        b_all = _dot(tri_bf, g_hi) + _dot(tri_bf, g_mid) + _dot(tri_bf, g_lo)
        for h in range(GLA_HEADS):
            kc, vc = slice(h * kd, (h + 1) * kd), slice(h * vd, (h + 1) * vd)
            b = b_all[:, kc]
            b_last = b[c_sz - 1:c_sz, :]
            q = q_ref[sl, kc].astype(F32) * scale
            k = k_ref[sl, kc].astype(F32)
            v_bf = v_ref[sl, vc]
            q_in = (q * jnp.exp(b)).astype(BF16)
            k_in = (k * jnp.exp(-b)).astype(BF16)
            k_out = (k * jnp.exp(b_last - b)).astype(BF16)
            a = jnp.where(tri, _dot_nt(q_in, k_in), 0.0)
            st = s_scr[h]
            o_scr[sl, vc] = _dot_nt(q_in, st.astype(BF16)) + _dot(a.astype(BF16), v_bf)
            s_scr[h] = st * jnp.exp(b_last) + pl.dot(v_bf, k_out, trans_a=True)

    for h in range(GLA_HEADS):
        vc = slice(h * vd, (h + 1) * vd)
        r = r_ref[:, vc].astype(F32)
        u_ref[:, vc] = (_rmsnorm(o_scr[:, vc], gn_ref[...]) * (r * jax.nn.sigmoid(r))).astype(u_ref.dtype)

    @pl.when(t == pl.num_programs(1) - 1)
    def _():
        st_ref[0] = s_scr[...]


def _gla_prompt(proj, w_up_bf, b_up, g_norm, batch, seq, tc):
    nt = seq // tc
    kw, vw = GLA_HEADS * GLA_HEAD_K, GLA_HEADS * GLA_HEAD_V
    rows = lambda b, t: b * nt + t
    kern = functools.partial(_gla_kernel, n_chunks=tc // GLA_CHUNK)
    full = lambda a: pl.BlockSpec(a.shape, lambda b, t: (0, 0))
    return pl.pallas_call(
        kern,
        grid=(batch, nt),
        in_specs=[
            pl.BlockSpec((tc, kw), lambda b, t: (rows(b, t), OFF_QG // kw)),
            pl.BlockSpec((tc, kw), lambda b, t: (rows(b, t), OFF_KG // kw)),
            pl.BlockSpec((tc, vw), lambda b, t: (rows(b, t), OFF_VG // vw)),
            pl.BlockSpec((tc, vw), lambda b, t: (rows(b, t), OFF_RG // vw)),
            pl.BlockSpec((tc, LANES), lambda b, t: (rows(b, t), OFF_GK // LANES)),
            full(w_up_bf), full(b_up), full(g_norm),
        ],
        out_specs=[
            pl.BlockSpec((tc, vw), lambda b, t: (rows(b, t), 0)),
            pl.BlockSpec((1, GLA_HEADS, GLA_HEAD_V, GLA_HEAD_K), lambda b, t: (b, 0, 0, 0)),
        ],
        out_shape=[jax.ShapeDtypeStruct((batch * seq, vw), BF16),
                   jax.ShapeDtypeStruct((batch, GLA_HEADS, GLA_HEAD_V, GLA_HEAD_K), F32)],
        scratch_shapes=[pltpu.VMEM((GLA_HEADS, GLA_HEAD_V, GLA_HEAD_K), F32), pltpu.VMEM((tc, vw), F32)],
        compiler_params=_cparams("parallel", "arbitrary"),
        name="gla_prompt",
    )(proj, proj, proj, proj, proj, w_up_bf, b_up, g_norm)


def _attn_kernel(q_ref, kc_ref, kp_ref, vc_ref, vp_ref, o_ref, l_ref, *, group, dilation, blk, n_qb):
    n = pl.program_id(2)
    e = ATTN_HEAD_DIM
    i_idx = lax.broadcasted_iota(jnp.int32, (blk, 2 * blk), 0)
    j_idx = lax.broadcasted_iota(jnp.int32, (blk, 2 * blk), 1)
    rel = i_idx + blk - j_idx
    band = (rel >= 0) & (rel <= blk)
    dist = (dilation * rel).astype(F32)
    bias = [jnp.where(band, -_alibi_slope(group, h) * dist, NEG_BIG) for h in range(ATTN_HEADS)]
    has_prev = (j_idx >= blk) | (n > 0)
    lane = lax.broadcasted_iota(jnp.int32, (blk, LANES), 1)
    for rr, qb in [(rr, qb) for rr in range(q_ref.shape[1]) for qb in range(n_qb)]:
        rows = slice(qb * blk, (qb + 1) * blk)
        q = q_ref[0, rr, rows, :]
        if qb == 0:
            k_prev, v_prev = kp_ref[0, rr], vp_ref[0, rr]
        else:
            prev = slice((qb - 1) * blk, qb * blk)
            k_prev, v_prev = kc_ref[0, rr, prev, :], vc_ref[0, rr, prev, :]
        kk = jnp.concatenate([k_prev, kc_ref[0, rr, rows, :]], axis=0)
        vv = jnp.concatenate([v_prev, vc_ref[0, rr, rows, :]], axis=0)
        lse_tile = jnp.zeros((blk, LANES), F32)
        for h in range(ATTN_HEADS):
            cols = slice(h * e, (h + 1) * e)
            b_h = jnp.where(has_prev, bias[h], NEG_BIG) if qb == 0 else bias[h]
            s = _dot_nt(q[:, cols], kk[:, cols]) * (e ** -0.5) + b_h
            m = jnp.max(s, axis=-1, keepdims=True)
            p = jnp.exp(s - m)
            den = jnp.sum(p, axis=-1, keepdims=True)
            o = _dot(p.astype(BF16), vv[:, cols]) / den
            o_ref[0, rr, rows, cols] = o.astype(o_ref.dtype)
            lse_tile = jnp.where(lane == h, m + jnp.log(den), lse_tile)
        l_ref[0, rr, rows, :] = lse_tile


def _attn_prompt(src, col0, group, tq, rows_per_step):
    batch, dilation, n_sub, _ = src.shape
    window, dil = ATTN_GROUPS[group]
    assert dil == dilation
    blk = window // dilation
    n_qb = tq // blk
    gw = ATTN_GROUP_DIM
    qc, kc, vc = col0 // gw, col0 // gw + 1, col0 // gw + 2
    n_res = max(1, min(dilation, rows_per_step // tq))
    cur = lambda c: pl.BlockSpec((1, n_res, tq, gw), lambda b, r, n: (b, r, n, c))
    prev = lambda c: pl.BlockSpec((1, n_res, blk, gw), lambda b, r, n: (b, r, jnp.maximum(n * n_qb - 1, 0), c))
    kern = functools.partial(_attn_kernel, group=group, dilation=dilation, blk=blk, n_qb=n_qb)
    out_spec = pl.BlockSpec((1, n_res, tq, gw), lambda b, r, n: (b, r, n, 0))
    return pl.pallas_call(
        kern,
        grid=(batch, dilation // n_res, n_sub // tq),
        in_specs=[cur(qc), cur(kc), prev(kc), cur(vc), prev(vc)],
        out_specs=[out_spec, pl.BlockSpec((1, n_res, tq, LANES), lambda b, r, n: (b, r, n, 0))],
        out_shape=[jax.ShapeDtypeStruct((batch, dilation, n_sub, gw), BF16),
                   jax.ShapeDtypeStruct((batch, dilation, n_sub, LANES), F32)],
        compiler_params=_cparams("parallel", "parallel", "arbitrary"),
        name=f"attn_prompt_g{group}",
    )(src, src, src, src, src)


def _merge_kernel(x_ref, u_ref, gg_ref, ga_ref, o1_ref, o2_ref, o3_ref, l1_ref, l2_ref, l3_ref,
                  wbg_ref, wba_ref, wo_ref, gf_ref, wr_hi_ref, wr_lo_ref, br_ref,
                  x1_ref, h2_ref, route_ref, cpad_ref, *scr, n_experts, dilations, precise, ts):
    tm = x_ref.shape[0]
    scr = list(scr)

    def natural(ref, dilation):
        if dilation == 1:
            return ref[0, 0].astype(F32)
        buf = scr.pop(0)
        n = tm // dilation
        for c in range(buf.shape[0]):
            for r in range(dilation):
                buf[c, pl.ds(r, n, stride=dilation), :] = ref[0, r, :, c * LANES:(c + 1) * LANES].astype(F32)
        return jnp.concatenate([buf[c] for c in range(buf.shape[0])], axis=1)

    o1, o2, o3 = (natural(r, d) for r, d in zip((o1_ref, o2_ref, o3_ref), dilations))
    l1, l2, l3 = (natural(r, d) for r, d in zip((l1_ref, l2_ref, l3_ref), dilations))
    mm = _dot_hi if precise else (lambda a, b: _dot(a.astype(BF16), b))
    lm = jnp.maximum(jnp.maximum(l1, l2), l3)
    e1, e2, e3 = jnp.exp(l1 - lm), jnp.exp(l2 - lm), jnp.exp(l3 - lm)
    inv = 1.0 / (e1 + e2 + e3)
    head = lax.broadcasted_iota(jnp.int32, (LANES, ATTN_GROUP_DIM), 0)
    owner = lax.broadcasted_iota(jnp.int32, (LANES, ATTN_GROUP_DIM), 1) // ATTN_HEAD_DIM
    spread = jnp.where(head == owner, 1.0, 0.0)

    def per_lane(w):
        if precise:
            return _dot_hi(w, spread)
        return _dot(w.astype(BF16), spread.astype(BF16))

    o_att = per_lane(e1 * inv) * o1 + per_lane(e2 * inv) * o2 + per_lane(e3 * inv) * o3
    merged = (jax.nn.sigmoid(gg_ref[...].astype(F32)) * mm(u_ref[...], wbg_ref[...])
              + jax.nn.sigmoid(ga_ref[...].astype(F32)) * mm(o_att, wba_ref[...]))
    x1 = x_ref[...] + mm(merged, wo_ref[...])
    x1_ref[...] = x1
    h2 = _rmsnorm(x1, gf_ref[...])
    h2_ref[...] = h2.astype(BF16)

    if precise:
        logits = _dot_hi(h2, wr_hi_ref[...]) + br_ref[...]
    else:
        h_hi, h_mid, _ = _split3(h2)
        both = _dot(h_hi, wr_lo_ref[...])
        logits = both[:, :LANES] + both[:, LANES:] + _dot(h_mid, wr_hi_ref[...]) + br_ref[...]
    lane = lax.broadcasted_iota(jnp.int32, (ts, LANES), 1).astype(F32)
    ri = lax.broadcasted_iota(jnp.int32, (ts, ts), 0)
    ci = lax.broadcasted_iota(jnp.int32, (ts, ts), 1)
    earlier = jnp.where(ci < ri, 1.0, 0.0).astype(BF16)
    li = lax.broadcasted_iota(jnp.int32, (LANES, LANES), 0)
    lj = lax.broadcasted_iota(jnp.int32, (LANES, LANES), 1)
    before = jnp.where(li < lj, 1.0, 0.0).astype(BF16)
    for s in range(tm // ts):
        rows = slice(s * ts, (s + 1) * ts)
        cur = jnp.where(lane < n_experts, logits[rows, :], -jnp.inf)
        tops, sels = [], []
        for _ in range(TOP_K):
            m = jnp.max(cur, axis=-1, keepdims=True)
            idx = jnp.min(jnp.where(cur == m, lane, float(LANES)), axis=-1, keepdims=True)
            sel = lane == idx
            tops.append(m)
            sels.append(sel)
            cur = jnp.where(sel, -jnp.inf, cur)
        ex = [jnp.exp(m - tops[0]) for m in tops]
        den = ex[0] + ex[1] + ex[2] + ex[3]
        mem = jnp.zeros((ts, LANES), F32)
        for sel in sels:
            mem = jnp.where(sel, 1.0, mem)
        rank = _dot(earlier, mem.astype(BF16))
        count = jnp.sum(mem, axis=0, keepdims=True)
        cpad = jnp.ceil(count * (1.0 / ROW_ALIGN)) * ROW_ALIGN
        cpad8 = jnp.broadcast_to(cpad, (8, LANES))
        seg_start = _dot(cpad8.astype(BF16), before)[0:1]
        pos = seg_start + rank
        route = jnp.zeros((ts, LANES), F32)
        for k in range(TOP_K):
            dest = jnp.sum(jnp.where(sels[k], pos, 0.0), axis=-1, keepdims=True)
            route = jnp.where(lane == float(k), dest, route)
            route = jnp.where(lane == float(TOP_K + k), ex[k] / den, route)
        route_ref[rows, :] = route
        cpad_ref[s * 8:(s + 1) * 8, :] = cpad8


def _merge(x2d, u, gate_src, outs, lses, wbg, wba, wo, g_ffn, wr_hi, wr_lo, b_r, n_experts, seq, tm, ts, precise):
    m, d = x2d.shape
    gw = ATTN_GROUP_DIM
    tps = seq // tm
    dilations = tuple(o.shape[1] for o in outs)
    row = lambda w: pl.BlockSpec((tm, w), lambda i: (i, 0))
    full = lambda a: pl.BlockSpec(a.shape, lambda i: (0, 0))
    grp = lambda dil, w: pl.BlockSpec((1, dil, tm // dil, w), lambda i: (i // tps, 0, i % tps, 0))
    kern = functools.partial(_merge_kernel, n_experts=n_experts, dilations=dilations, precise=precise, ts=ts)
    n_dilated = sum(1 for dil in dilations if dil > 1)
    interleave = lambda w: [pltpu.VMEM((w // LANES, tm, LANES), F32)] * n_dilated
    return pl.pallas_call(
        kern,
        grid=(m // tm,),
        in_specs=[row(d), row(d),
                  pl.BlockSpec((tm, d), lambda i: (i, OFF_GATE_GLA // d)),
                  pl.BlockSpec((tm, d), lambda i: (i, OFF_GATE_ATTN // d)),
                  *[grp(dil, gw) for dil in dilations], *[grp(dil, LANES) for dil in dilations],
                  full(wbg), full(wba), full(wo), full(g_ffn), full(wr_hi), full(wr_lo), full(b_r)],
        out_specs=[row(d), row(d), row(LANES), pl.BlockSpec((tm // ts * 8, LANES), lambda i: (i, 0))],
        out_shape=[jax.ShapeDtypeStruct((m, d), F32), jax.ShapeDtypeStruct((m, d), BF16),
                   jax.ShapeDtypeStruct((m, LANES), F32), jax.ShapeDtypeStruct((m // ts * 8, LANES), F32)],
        scratch_shapes=interleave(gw) + interleave(LANES),
        compiler_params=_cparams("parallel"),
        name="merge_router_sample" if precise else "merge_router",
    )(x2d, u, gate_src, gate_src, *outs, *lses, wbg, wba, wo, g_ffn, wr_hi, wr_lo, b_r)


def _round_up(x, mult):
    return (x + mult - 1) // mult * mult


def _tile_rows(n_tokens, n_experts):
    return _round_up(n_tokens * TOP_K + n_experts * (ROW_ALIGN - 1), ROW_ALIGN)


def _moe_plan(cpad, nc, n_experts, tm_e, n_tiles_max):
    tot = jnp.sum(cpad, axis=0)
    gsize = (tot + tm_e - 1) // tm_e * tm_e
    g_end = jnp.cumsum(gsize)
    g_start = g_end - gsize
    n_used = (g_end[-1] // tm_e).astype(jnp.int32)
    o_start = jnp.cumsum(cpad, axis=0) - cpad
    seg_end = jnp.cumsum(cpad, axis=1)
    seg_start = seg_end - cpad

    row0 = jnp.arange(nc, dtype=jnp.int32) * ROW_ALIGN
    e_of = jnp.sum((seg_end[:, None, :] <= row0[None, :, None]).astype(jnp.int32), axis=-1)
    e_of = jnp.minimum(e_of, n_experts - 1)
    pick = (e_of[:, :, None] == jnp.arange(n_experts)[None, None, :]).astype(jnp.int32)
    base = g_start[None, :] + o_start - seg_start
    dst = jnp.sum(pick * base[:, None, :], axis=-1) + row0[None, :]
    nch = (seg_end[:, -1] // ROW_ALIGN).astype(jnp.int32)
    dst = jnp.where(jnp.arange(nc)[None, :] < nch[:, None], dst, 0).astype(jnp.int32)

    t0 = jnp.arange(n_tiles_max, dtype=jnp.int32) * tm_e
    tile_expert = jnp.minimum(jnp.sum((g_end[None, :] <= t0[:, None]).astype(jnp.int32), axis=-1), n_experts - 1)
    tile_id = jnp.arange(n_tiles_max)
    last = jnp.sum(jnp.where(tile_id == n_used - 1, tile_expert, 0))
    tile_expert = jnp.where(tile_id < n_used, tile_expert, last).astype(jnp.int32)
    gap_start = (g_start + tot).astype(jnp.int32)
    gap_chunks = ((gsize - tot) // ROW_ALIGN).astype(jnp.int32)

    prev = jnp.concatenate([jnp.full((1,), -1, jnp.int32), tile_expert[:-1]])
    first = (tile_expert != prev) & (tile_id < n_used)
    slot = (jnp.cumsum(first.astype(jnp.int32)) - 1) % 2
    e_id = jnp.arange(n_experts)
    later = (e_id[None, :] > e_id[:, None]) & (gsize[None, :] > 0)
    next_of = jnp.min(jnp.where(later, e_id[None, :], n_experts), axis=1)
    next_of = jnp.where(next_of == n_experts, -1, next_of)
    nxt = jnp.sum(jnp.where(tile_expert[:, None] == e_id[None, :], next_of[None, :], 0), axis=1)
    group_info = jnp.stack([first.astype(jnp.int32), slot, nxt], axis=1).reshape(-1).astype(jnp.int32)
    return dst.reshape(-1), nch, tile_expert, n_used.reshape(1), gap_start, gap_chunks, group_info


def _loop(n, fn, unroll=1):
    def body(i, carry):
        for u in range(unroll):
            fn(i * unroll + u)
        return carry

    main = n // unroll if unroll > 1 else n
    lax.fori_loop(0, main, body, 0)
    if unroll > 1:
        lax.fori_loop(main * unroll, n, lambda i, carry: (fn(i), carry)[1], 0)


def _compact_kernel(dst_ref, nch_ref, gap_start_ref, gap_chunks_ref, nu_ref, hp_ref, rp_ref, hs_ref, rs_ref,
                    xs_ref, xc_scr, zero_scr, sem, *, nc):
    s = pl.program_id(0)
    last = pl.num_programs(0) - 1
    is_sample = s == last
    ts, r_pad = hp_ref.shape[0], xc_scr.shape[1]

    def chunk(tile, c):
        slot = tile % 2
        src = xc_scr.at[slot, pl.ds(pl.multiple_of(c * ROW_ALIGN, ROW_ALIGN), ROW_ALIGN)]
        dst = xs_ref.at[pl.ds(pl.multiple_of(dst_ref[tile * nc + c], ROW_ALIGN), ROW_ALIGN)]
        return pltpu.make_async_copy(src, dst, sem.at[slot])

    def drain(tile):
        slot = tile % 2
        one = pltpu.make_async_copy(xc_scr.at[slot, pl.ds(0, ROW_ALIGN)], xs_ref.at[pl.ds(0, ROW_ALIGN)], sem.at[slot])
        _loop(nch_ref[tile], lambda c: one.wait(), unroll=8)

    @pl.when(s >= 2)
    def _():
        drain(s - 2)

    h = jnp.where(is_sample, hs_ref[...], hp_ref[...])
    route = jnp.where(is_sample, rs_ref[...], rp_ref[...])
    dest_t = jnp.transpose(route)
    row_id = lax.broadcasted_iota(jnp.int32, (r_pad, ts), 0).astype(F32)
    perm = jnp.zeros((r_pad, ts), F32)
    for k in range(TOP_K):
        perm = jnp.where(row_id == dest_t[k:k + 1, :], 1.0, perm)
    xc_scr[s % 2] = _dot(perm.astype(BF16), h)
    _loop(nch_ref[s], lambda c: chunk(s, c).start(), unroll=4)

    @pl.when(is_sample)
    def _():
        @pl.when(s >= 1)
        def _():
            drain(s - 1)
        drain(s)

        zero_scr[...] = jnp.zeros_like(zero_scr)
        tile = zero_scr.shape[0]

        def gap(e, j):
            row = pl.multiple_of(gap_start_ref[e] + j * ROW_ALIGN, ROW_ALIGN)
            return pltpu.make_async_copy(zero_scr.at[pl.ds(0, ROW_ALIGN)], xs_ref.at[pl.ds(row, ROW_ALIGN)], sem.at[0])

        def tail(t):
            row = pl.multiple_of(t * tile, tile)
            return pltpu.make_async_copy(zero_scr, xs_ref.at[pl.ds(row, tile)], sem.at[0])

        n_exp = gap_start_ref.shape[0]
        n_tail = xs_ref.shape[0] // tile - nu_ref[0]
        _loop(n_exp, lambda e: _loop(gap_chunks_ref[e], lambda j: gap(e, j).start()))
        _loop(n_tail, lambda t: tail(nu_ref[0] + t).start())
        _loop(n_exp, lambda e: _loop(gap_chunks_ref[e], lambda j: gap(e, j).wait()))
        _loop(n_tail, lambda t: tail(nu_ref[0] + t).wait())


def _compact(h2_p, route_p, h2_s, route_s, plan, ts, r_pad, nc, n_tiles_e, tm_e):
    m, d = h2_p.shape
    n_p = m // ts
    dst, nch, _, n_used, gap_start, gap_chunks, _ = plan
    prompt = lambda w: pl.BlockSpec((ts, w), lambda i, *_: (jnp.minimum(i, n_p - 1), 0))
    sample = lambda w: pl.BlockSpec((ts, w), lambda i, *_: (0, 0))
    return pl.pallas_call(
        functools.partial(_compact_kernel, nc=nc),
        grid_spec=pltpu.PrefetchScalarGridSpec(
            num_scalar_prefetch=5,
            grid=(n_p + 1,),
            in_specs=[prompt(d), prompt(LANES), sample(d), sample(LANES)],
            out_specs=pl.BlockSpec(memory_space=pl.ANY),
            scratch_shapes=[pltpu.VMEM((2, r_pad, d), F32), pltpu.VMEM((tm_e, d), F32),
                            pltpu.SemaphoreType.DMA((2,))]),
        out_shape=jax.ShapeDtypeStruct((n_tiles_e * tm_e, d), F32),
        compiler_params=_cparams("arbitrary"),
        name="moe_compact",
    )(dst, nch, gap_start, gap_chunks, n_used, h2_p, route_p, h2_s, route_s)


def _expert_kernel(te_ref, nu_ref, nx_ref, x_ref, wgu_ref, bgu_ref, wd_ref, bd_ref, o_ref,
                   wgu_f32, wd_f32, wgu_bf, wd_bf, sem, *, d_ff, sub):
    t = pl.program_id(0)
    used = t < nu_ref[0]
    nf = 3

    def fetch(expert, slot):
        return (pltpu.make_async_copy(wgu_ref.at[expert], wgu_f32.at[slot], sem.at[0, slot]),
                pltpu.make_async_copy(wd_ref.at[expert], wd_f32.at[slot], sem.at[1, slot]))

    @pl.when(used)
    def _():
        @pl.when(nx_ref[nf * t] == 1)
        def _():
            slot, nxt = nx_ref[nf * t + 1], nx_ref[nf * t + 2]

            @pl.when(t == 0)
            def _():
                for cp in fetch(te_ref[0], slot):
                    cp.start()

            for cp in fetch(te_ref[t], slot):
                cp.wait()
            step = 128
            for i in range(wgu_bf.shape[0] // step):
                rows = slice(i * step, (i + 1) * step)
                wgu_bf[rows, :] = wgu_f32[slot, rows, :].astype(BF16)
            for i in range(wd_bf.shape[0] // step):
                rows = slice(i * step, (i + 1) * step)
                wd_bf[rows, :] = wd_f32[slot, rows, :].astype(BF16)

            @pl.when(nxt >= 0)
            def _():
                for cp in fetch(nxt, 1 - slot):
                    cp.start()

        for s in range(x_ref.shape[0] // sub):
            rows = slice(s * sub, (s + 1) * sub)
            gu = _dot(x_ref[rows, :].astype(BF16), wgu_bf[...]) + bgu_ref[0]
            gate = jnp.minimum(gu[:, :d_ff], SWIGLU_LIMIT)
            up = jnp.clip(gu[:, d_ff:], -SWIGLU_LIMIT, SWIGLU_LIMIT)
            act = (up + 1.0) * gate * jax.nn.sigmoid(SWIGLU_ALPHA * gate)
            o_ref[rows, :] = _dot(act.astype(BF16), wd_bf[...]) + bd_ref[0]

    @pl.when(jnp.logical_not(used))
    def _():
        o_ref[...] = jnp.zeros_like(o_ref)


def _experts(xs, tile_expert, n_used, group_info, wgu, bgu, wd, bd, tm_e):
    n_rows, d = xs.shape
    _, _, two_ff = wgu.shape
    d_ff = two_ff // 2
    kern = functools.partial(_expert_kernel, d_ff=d_ff, sub=tm_e)
    by_expert = lambda shape: pl.BlockSpec(shape, lambda t, te, nu, nx: (te[t], 0, 0))
    hbm = pl.BlockSpec(memory_space=pl.ANY)
    return pl.pallas_call(
        kern,
        grid_spec=pltpu.PrefetchScalarGridSpec(
            num_scalar_prefetch=3,
            grid=(n_rows // tm_e,),
            in_specs=[pl.BlockSpec((tm_e, d), lambda t, te, nu, nx: (jnp.minimum(t, nu[0] - 1), 0)),
                      hbm, by_expert((1, 1, two_ff)), hbm, by_expert((1, 1, d))],
            out_specs=pl.BlockSpec((tm_e, d), lambda t, te, nu, nx: (t, 0)),
            scratch_shapes=[pltpu.VMEM((2, d, two_ff), F32), pltpu.VMEM((2, d_ff, d), F32),
                            pltpu.VMEM((d, two_ff), BF16), pltpu.VMEM((d_ff, d), BF16),
                            pltpu.SemaphoreType.DMA((2, 2))]),
        out_shape=jax.ShapeDtypeStruct((n_rows, d), F32),
        compiler_params=pltpu.CompilerParams(dimension_semantics=("arbitrary",), vmem_limit_bytes=56 * 1024 * 1024),
        name="moe_experts",
    )(tile_expert, n_used, group_info, xs, wgu, bgu, wd, bd)


def _shift_kernel(c_ref, new_ref, o_ref):
    j = pl.program_id(1)
    last = pl.num_programs(1) - 1
    bn = o_ref.shape[1]

    @pl.when(j != last)
    def _():
        o_ref[...] = c_ref[...]

    @pl.when(j == last)
    def _():
        o_ref[0, 0:bn - 1] = c_ref[0, 1:bn]
        o_ref[0, bn - 1:bn] = new_ref[0]


def _shift_cache(cache, new_row, bn):
    bd, n = cache.shape[:2]
    tail = cache.shape[2:]
    zeros = (0,) * len(tail)
    return pl.pallas_call(
        _shift_kernel,
        grid=(bd, n // bn),
        in_specs=[pl.BlockSpec(tuple(pl.Element(s) for s in (1, bn) + tail),
                               lambda b, j: (b, jnp.minimum(j * bn + 1, n - bn)) + zeros),
                  pl.BlockSpec((1, 1) + tail, lambda b, j: (b, 0) + zeros)],
        out_specs=pl.BlockSpec((1, bn) + tail, lambda b, j: (b, j) + zeros),
        out_shape=jax.ShapeDtypeStruct(cache.shape, cache.dtype),
        compiler_params=_cparams("parallel", "arbitrary"),
        name="cache_shift",
    )(cache, new_row)


def _combine_kernel(dst_ref, nch_ref, route_ref, x1_ref, gfin_ref, ys_ref, y_ref, yc_scr, sem, *, nc, tile0):
    i = pl.program_id(0)

    def chunk(step, c):
        slot = step % 2
        tile = step + tile0
        src = ys_ref.at[pl.ds(pl.multiple_of(dst_ref[tile * nc + c], ROW_ALIGN), ROW_ALIGN)]
        dst = yc_scr.at[slot, pl.ds(pl.multiple_of(c * ROW_ALIGN, ROW_ALIGN), ROW_ALIGN)]
        return pltpu.make_async_copy(src, dst, sem.at[slot])

    def fetch(step):
        _loop(nch_ref[step + tile0], lambda c: chunk(step, c).start(), unroll=4)

    @pl.when(i == 0)
    def _():
        yc_scr[...] = jnp.zeros_like(yc_scr)
        fetch(i)

    @pl.when(i + 1 < pl.num_programs(0))
    def _():
        fetch(i + 1)

    one = pltpu.make_async_copy(ys_ref.at[pl.ds(0, ROW_ALIGN)], yc_scr.at[i % 2, pl.ds(0, ROW_ALIGN)], sem.at[i % 2])
    _loop(nch_ref[i + tile0], lambda c: one.wait(), unroll=8)

    ts, r_pad = x1_ref.shape[0], yc_scr.shape[1]
    route = route_ref[...]
    col_id = lax.broadcasted_iota(jnp.int32, (ts, r_pad), 1).astype(F32)
    weights = jnp.zeros((ts, r_pad), F32)
    for k in range(TOP_K):
        weights = jnp.where(col_id == route[:, k:k + 1], route[:, TOP_K + k:TOP_K + k + 1], weights)
    moe = _dot(weights.astype(BF16), yc_scr[i % 2].astype(BF16))
    y_ref[...] = _rmsnorm(x1_ref[...] + moe, gfin_ref[...])


def _combine(route, x1, g_final, ys, plan, tile0, ts, r_pad, nc):
    m, d = x1.shape
    dst, nch = plan[0], plan[1]
    return pl.pallas_call(
        functools.partial(_combine_kernel, nc=nc, tile0=tile0),
        grid_spec=pltpu.PrefetchScalarGridSpec(
            num_scalar_prefetch=2,
            grid=(m // ts,),
            in_specs=[pl.BlockSpec((ts, LANES), lambda i, *_: (i, 0)),
                      pl.BlockSpec((ts, d), lambda i, *_: (i, 0)),
                      pl.BlockSpec((1, d), lambda i, *_: (0, 0)),
                      pl.BlockSpec(memory_space=pl.ANY)],
            out_specs=pl.BlockSpec((ts, d), lambda i, *_: (i, 0)),
            scratch_shapes=[pltpu.VMEM((2, r_pad, d), F32), pltpu.SemaphoreType.DMA((2,))]),
        out_shape=jax.ShapeDtypeStruct((m, d), F32),
        compiler_params=_cparams("arbitrary"),
        name="moe_combine",
    )(dst, nch, route, x1, g_final, ys)


def _column(x_row):
    return jnp.transpose(jnp.broadcast_to(x_row, (LANES, LANES)))


def _gla_step_kernel(p_ref, s_ref, wup_ref, bup_ref, gn_ref, u_ref, so_ref):
    row = p_ref[0]
    gk8 = jnp.broadcast_to(row[:, OFF_GK:OFF_GK + LANES], (8, LANES))
    gk = _dot_hi(gk8, wup_ref[...]) + bup_ref[...]
    decay = jnp.exp(_log_sigmoid(gk[0:1, :]) * (1.0 / GLA_GATE_NORMALIZER))
    kd, vd = GLA_HEAD_K, GLA_HEAD_V
    wide = lambda c: jnp.concatenate([c, c], axis=1)
    for h in range(GLA_HEADS):
        q = row[:, OFF_QG + h * kd:OFF_QG + (h + 1) * kd] * (kd ** -0.5)
        k = row[:, OFF_KG + h * kd:OFF_KG + (h + 1) * kd]
        v = row[:, OFF_VG + h * vd:OFF_VG + (h + 1) * vd]
        r = row[:, OFF_RG + h * vd:OFF_RG + (h + 1) * vd]
        s_new = s_ref[0, h] * wide(_column(decay[:, h * kd:(h + 1) * kd])) + wide(_column(k)) * v
        so_ref[0, h] = s_new
        o = jnp.sum(wide(_column(q)) * s_new, axis=0, keepdims=True)
        u_ref[0, :, h * vd:(h + 1) * vd] = _rmsnorm(o, gn_ref[...]) * (r * jax.nn.sigmoid(r))


def _gla_sample(proj_s3, state, w_up, b_up, g_norm):
    bd = proj_s3.shape[0]
    full = lambda a: pl.BlockSpec(a.shape, lambda b: (0,) * a.ndim)
    st_spec = pl.BlockSpec((1, GLA_HEADS, GLA_HEAD_K, GLA_HEAD_V), lambda b: (b, 0, 0, 0))
    return pl.pallas_call(
        _gla_step_kernel,
        grid=(bd,),
        in_specs=[pl.BlockSpec((1, 1, PROJ_W), lambda b: (b, 0, 0)), st_spec,
                  full(w_up), full(b_up), full(g_norm)],
        out_specs=[pl.BlockSpec((1, 1, GLA_HEADS * GLA_HEAD_V), lambda b: (b, 0, 0)), st_spec],
        out_shape=[jax.ShapeDtypeStruct((bd, 1, GLA_HEADS * GLA_HEAD_V), F32),
                   jax.ShapeDtypeStruct(state.shape, state.dtype)],
        compiler_params=_cparams("parallel"),
        name="gla_sample",
    )(proj_s3, state, w_up, b_up, g_norm)


def _attn_step_kernel(p_ref, c1_ref, c2_ref, c3_ref, o_ref, l_ref):
    row = p_ref[0]
    gw, e = ATTN_GROUP_DIM, ATTN_HEAD_DIM
    for g, c_ref in enumerate((c1_ref, c2_ref, c3_ref)):
        window, dilation = ATTN_GROUPS[g]
        blk = window // dilation
        back = (dilation * (blk - lax.broadcasted_iota(jnp.int32, (blk, 1), 0))).astype(F32)
        lane = lax.broadcasted_iota(jnp.int32, (1, LANES), 1)
        lse_row = jnp.zeros((1, LANES), F32)
        for h in range(ATTN_HEADS):
            lo = OFF_QKV[g] + h * e
            q = row[:, lo:lo + e]
            k_new = row[:, lo + gw:lo + gw + e]
            v_new = row[:, lo + 2 * gw:lo + 2 * gw + e]
            k_c = c_ref[0, :, 0, h, :]
            v_c = c_ref[0, :, 1, h, :]
            s = jnp.sum(k_c * q, axis=-1, keepdims=True) * (e ** -0.5) - _alibi_slope(g, h) * back
            s0 = jnp.sum(k_new * q, axis=-1, keepdims=True) * (e ** -0.5)
            m = jnp.maximum(jnp.max(s, axis=0, keepdims=True), s0)
            p = jnp.exp(s - m)
            p0 = jnp.exp(s0 - m)
            den = jnp.sum(p, axis=0, keepdims=True) + p0
            o = (jnp.sum(p * v_c, axis=0, keepdims=True) + p0 * v_new) / den
            cols = slice(g * gw + h * e, g * gw + (h + 1) * e)
            o_ref[0, :, cols] = o
            lse_row = jnp.where(lane == h, m + jnp.log(den), lse_row)
        l_ref[0, :, g * LANES:(g + 1) * LANES] = lse_row


def _attn_sample(proj_s3, caches):
    bd = proj_s3.shape[0]
    gw = ATTN_GROUP_DIM
    views, specs = [], []
    for g, cache in enumerate(caches):
        window, dilation = ATTN_GROUPS[g]
        n_buf = cache.shape[1]
        assert n_buf == window, "cache must hold exactly one window of rows"
        blk = n_buf // dilation
        views.append(cache.reshape(bd, blk, dilation, 2, ATTN_HEADS, ATTN_HEAD_DIM))
        specs.append(pl.BlockSpec((1, blk, None, 2, ATTN_HEADS, ATTN_HEAD_DIM), lambda b: (b, 0, 0, 0, 0, 0)))
    out_spec = pl.BlockSpec((1, 1, N_GROUPS * gw), lambda b: (b, 0, 0))
    return pl.pallas_call(
        _attn_step_kernel,
        grid=(bd,),
        in_specs=[pl.BlockSpec((1, 1, PROJ_W), lambda b: (b, 0, 0))] + specs,
        out_specs=[out_spec, pl.BlockSpec((1, 1, N_GROUPS * LANES), lambda b: (b, 0, 0))],
        out_shape=[jax.ShapeDtypeStruct((bd, 1, N_GROUPS * gw), F32),
                   jax.ShapeDtypeStruct((bd, 1, N_GROUPS * LANES), F32)],
        compiler_params=_cparams("parallel"),
        name="attn_sample",
    )(proj_s3, *views)


def _reorder_cols(a, dtype):
    offs, acc = [], 0
    for s in PROJ_SPLITS[:-1]:
        acc += s
        offs.append(acc)
    q_g, k_g, v_g, r_g, gk, q_a, k_a, v_a, gg, ga = [p.astype(dtype) for p in jnp.split(a, offs, axis=1)]
    gw = ATTN_GROUP_DIM
    qkv = lambda g: [p[:, g * gw:(g + 1) * gw] for p in (q_a, k_a, v_a)]
    pad = jnp.zeros((a.shape[0], NAT_W - OFF_GK - GLA_GATE_RANK), dtype)
    nat = jnp.concatenate([q_g, k_g, v_g, r_g, gg, ga, *qkv(0), gk, pad], axis=1)
    return nat, jnp.concatenate(qkv(1), axis=1), jnp.concatenate(qkv(2), axis=1)


def _kv_rows(src, col_k, n_keep):
    b, dil, n_sub, _ = src.shape
    kv = src[:, :, n_sub - n_keep // dil:, col_k:col_k + 2 * ATTN_GROUP_DIM]
    kv = jnp.swapaxes(kv, 1, 2).astype(F32)
    return kv.reshape(b, n_keep, 2, ATTN_HEADS, ATTN_HEAD_DIM)


def kernel(x_prompt, x_sample, state_gla, cache_kv_w128, cache_kv_w512, cache_kv_w2048, g_norm_mix, w_in, w_gk_up,
           b_gk_up, g_gla_norm, w_branch_gla, w_branch_attn, w_out, g_norm_ffn, w_router, b_router, w_gate_up,
           b_gate_up, w_down, b_down, g_final):
    depth = g_norm_mix.shape[0]
    assert depth == 1, "single-layer trunk"
    batch, seq, d = x_prompt.shape
    bd, dec_seq, _ = x_sample.shape
    assert d == D_MODEL and dec_seq == 1 and seq % ATTN_GROUPS[-1][0] == 0
    n_experts = w_router.shape[-1]
    caches = (cache_kv_w128[0], cache_kv_w512[0], cache_kv_w2048[0])
    gw = ATTN_GROUP_DIM

    w_in_t = jnp.swapaxes(w_in[0], 0, 1)
    w_blocks = _prep_w_in(w_in_t, tr=128)
    w_up =jnp.zeros((LANES, GLA_HEADS * GLA_HEAD_K), F32).at[:GLA_GATE_RANK].set(w_gk_up[0])
    w_up_bf = w_up.astype(BF16)
    b_up = b_gk_up[0][None, :]
    g_mix = g_norm_mix[0][None, :]
    g_gla = g_gla_norm[0][None, :]
    wbg, wba, wo = w_branch_gla[0], w_branch_attn[0], w_out[0]
    g_ffn = g_norm_ffn[0][None, :]
    w_r = jnp.zeros((d, LANES), F32).at[:, :n_experts].set(w_router[0])
    wr_hi = w_r.astype(BF16)
    wr_lo = jnp.concatenate([wr_hi, (w_r - wr_hi.astype(F32)).astype(BF16)], axis=1)
    b_r = jnp.zeros((1, LANES), F32).at[0, :n_experts].set(b_router[0])
    bgu, bdn = b_gate_up[0][:, None, :], b_down[0][:, None, :]
    g_fin = g_final[None, :]

    xp = x_prompt.reshape(batch * seq, d)
    nat = _norm_proj(xp, g_mix, w_blocks[0], batch, seq, 1, tm=1024, tn=NAT_W // 2)
    srcs, kv_p = [nat], [_kv_rows(nat, OFF_QKV0 + gw, min(ATTN_GROUPS[0][0], seq))]
    for g in (1, 2):
        src, kv = _norm_proj(xp, g_mix, w_blocks[g], batch, seq, ATTN_GROUPS[g][1], tm=1024, tn=QKV_W,
                             kv_keep=min(ATTN_GROUPS[g][0], seq))
        srcs.append(src)
        kv_p.append(kv)
    nat2d = nat.reshape(batch * seq, NAT_W)
    u_p, st_p = _gla_prompt(nat2d, w_up_bf, b_up, g_gla, batch, seq, tc=512)
    outs, lses = [], []
    for g in range(N_GROUPS):
        col0 = OFF_QKV0 if g == 0 else 0
        o, lse = _attn_prompt(srcs[g], col0, g, tq=min(512, seq // ATTN_GROUPS[g][1]), rows_per_step=512)
        outs.append(o)
        lses.append(lse)
    x1_p, h2_p, route_p, cpad_p = _merge(xp, u_p, nat2d, outs, lses, wbg.astype(BF16), wba.astype(BF16), wo.astype(BF16),
                                 g_ffn, wr_hi, wr_lo, b_r, n_experts, seq, tm=512, ts=MOE_TOKEN_TILE, precise=False)

    xs = x_sample.reshape(bd, d)
    proj_s = jnp.concatenate(_reorder_cols(_norm_proj_precise(xs, g_mix, w_in_t, tn=1024), F32), axis=1)
    proj_s3 = proj_s.reshape(bd, 1, PROJ_W)
    u_s, st_s = _gla_sample(proj_s3, state_gla[0], w_up, b_up, g_gla)
    o_s, l_s = _attn_sample(proj_s3, caches)
    outs_s = [o_s[:, 0, g * gw:(g + 1) * gw].reshape(1, 1, bd, gw) for g in range(N_GROUPS)]
    lses_s = [l_s[:, 0, g * LANES:(g + 1) * LANES].reshape(1, 1, bd, LANES) for g in range(N_GROUPS)]
    x1_s, h2_s, route_s, cpad_s = _merge(xs, u_s.reshape(bd, d), proj_s, outs_s, lses_s, wbg, wba, wo, g_ffn,
                                 w_r, w_r, b_r, n_experts, bd, tm=bd, ts=bd, precise=True)

    ts, tm_e = MOE_TOKEN_TILE, MOE_EXPERT_TILE
    assert bd <= ts
    n_tiles_p = batch * seq // ts
    rows = _tile_rows(ts, n_experts)
    r_pad, nc = _round_up(rows, LANES), rows // ROW_ALIGN
    n_tiles_e = -(-(n_tiles_p * rows + _tile_rows(bd, n_experts) + n_experts * (tm_e - 1)) // tm_e)
    cpad = jnp.concatenate([cpad_p[::8, :n_experts], cpad_s[::8, :n_experts]], axis=0).astype(jnp.int32)
    plan = _moe_plan(cpad, nc, n_experts, tm_e, n_tiles_e)
    pad_rows = lambda a, fill: jnp.concatenate([a, jnp.full((ts - bd, a.shape[1]), fill, a.dtype)], axis=0)
    h2_s, x1_s = pad_rows(h2_s, 0), pad_rows(x1_s, 0)
    route_s = pad_rows(route_s, -1.0)
    xs = _compact(h2_p, route_p, h2_s, route_s, plan, ts, r_pad, nc, n_tiles_e, tm_e)
    new_rows = [proj_s[:, OFF_QKV[g] + gw:OFF_QKV[g] + 3 * gw].reshape(bd, 1, 2, ATTN_HEADS, ATTN_HEAD_DIM)
                .astype(caches[g].dtype) for g in range(N_GROUPS)]
    ys = _experts(xs, plan[2], plan[3], plan[6], w_gate_up[0], bgu, w_down[0], bdn, tm_e)
    kv_s = [_shift_cache(caches[g], new_rows[g], min(512, caches[g].shape[1]))[None] for g in range(N_GROUPS)]
    y_p = _combine(route_p, x1_p, g_fin, ys, plan, 0, ts, r_pad, nc)
    y_s = _combine(route_s, x1_s, g_fin, ys, plan, n_tiles_p, ts, r_pad, nc)[:bd]

    return (y_p.reshape(batch, seq, d), y_s.reshape(bd, 1, d),
            jnp.swapaxes(st_p, -1, -2)[None], kv_p[0][None], kv_p[1][None], kv_p[2][None],
            st_s[None], kv_s[0], kv_s[1], kv_s[2])
```

```python
import functools

import jax
import jax.numpy as jnp
from jax import lax
from jax.experimental import pallas as pl
from jax.experimental.pallas import tpu as pltpu

F32 = jnp.float32
BF16 = jnp.bfloat16
HIGHEST = lax.Precision.HIGHEST

D_MODEL = 1024
NORM_EPS = 1e-5
GLA_HEADS = 4
GLA_HEAD_K = 128
GLA_HEAD_V = 256
GLA_GATE_RANK = 16
GLA_GATE_NORMALIZER = 16.0
GLA_CHUNK = 64
ATTN_GROUPS = ((128, 1), (512, 4), (2048, 16))
N_GROUPS = 3
ATTN_HEADS = 4
ATTN_HEAD_DIM = 128
ATTN_GROUP_DIM = ATTN_HEADS * ATTN_HEAD_DIM
TOP_K = 4
SWIGLU_LIMIT = 7.0
SWIGLU_ALPHA = 1.702
NEG_BIG = -1e30
LANES = 128
ROW_ALIGN = 8
MOE_TOKEN_TILE = 256
MOE_EXPERT_TILE = 512

OFF_QG, OFF_KG, OFF_VG, OFF_RG = 0, 512, 1024, 2048
OFF_GATE_GLA, OFF_GATE_ATTN = 3072, 4096
OFF_QKV0 = 5120
OFF_GK = 6656
NAT_W = 7168
QKV_W = 3 * ATTN_GROUP_DIM
PROJ_W = NAT_W + 2 * QKV_W
OFF_QKV = (OFF_QKV0, NAT_W, NAT_W + QKV_W)
PROJ_SPLITS = (512, 512, 1024, 1024, GLA_GATE_RANK, 1536, 1536, 1536, 1024, 1024)

VMEM_LIMIT = 48 * 1024 * 1024


def _cparams(*sem):
    return pltpu.CompilerParams(dimension_semantics=sem, vmem_limit_bytes=VMEM_LIMIT)


def _alibi_slope(group, head):
    n = N_GROUPS * ATTN_HEADS
    return 2.0 ** (-8.0 * (group * ATTN_HEADS + head + 1) / n)


def _log_sigmoid(x):
    return jnp.minimum(x, 0.0) - jnp.log1p(jnp.exp(-jnp.abs(x)))


def _dot_nt(a, b):
    return lax.dot_general(a, b, (((1,), (1,)), ((), ())), preferred_element_type=F32)


def _dot(a, b):
    return jnp.dot(a, b, preferred_element_type=F32)


def _dot_hi(a, b):
    return jnp.dot(a, b, preferred_element_type=F32, precision=HIGHEST)


def _split3(x):
    hi = x.astype(BF16)
    r1 = x - hi.astype(F32)
    mid = r1.astype(BF16)
    lo = (r1 - mid.astype(F32)).astype(BF16)
    return hi, mid, lo


def _rmsnorm(x, g):
    return x * lax.rsqrt(jnp.mean(x * x, axis=-1, keepdims=True) + NORM_EPS) * g


def _proj_kernel(x_ref, g_ref, w_ref, o_ref, *rest, dilation, kv_first, kv_row0):
    kv_ref = rest[0] if kv_first is not None else None
    h_ref = rest[-2] if dilation > 1 else rest[-1]

    @pl.when(pl.program_id(1) == 0)
    def _():
        h_ref[...] = _rmsnorm(x_ref[...], g_ref[...]).astype(BF16)

    acc = _dot(h_ref[...], w_ref[...])
    if dilation == 1:
        o_ref[0, 0] = acc.astype(o_ref.dtype)
    else:
        scr = rest[-1]
        n = acc.shape[0] // dilation
        for c in range(acc.shape[1] // LANES):
            cols = slice(c * LANES, (c + 1) * LANES)
            scr[c] = acc[:, cols]
            for r in range(dilation):
                o_ref[0, r, :, cols] = scr[c, pl.ds(r, n, stride=dilation), :].astype(o_ref.dtype)

    if kv_ref is not None:
        @pl.when(pl.program_id(0) % kv_first[1] >= kv_first[0])
        def _():
            rows = kv_ref.shape[1]
            for j in range(2):
                for h in range(ATTN_HEADS):
                    c0 = ATTN_GROUP_DIM + (j * ATTN_HEADS + h) * ATTN_HEAD_DIM
                    kv_ref[0, :, j, h, :] = acc[kv_row0:kv_row0 + rows, c0:c0 + ATTN_HEAD_DIM]


def _norm_proj(x2d, g_row, w_bf, batch, seq, dilation, tm, tn, kv_keep=None):
    m, d = x2d.shape
    n = w_bf.shape[1]
    tps = seq // tm
    out_specs = [pl.BlockSpec((1, dilation, tm // dilation, tn), lambda i, j: (i // tps, 0, i % tps, j))]
    out_shape = [jax.ShapeDtypeStruct((batch, dilation, seq // dilation, n), BF16)]
    kv_first, kv_row0 = None, 0
    if kv_keep is not None:
        assert n == tn == QKV_W
        kv_tiles = max(kv_keep // tm, 1)
        rows = min(kv_keep, tm)
        first = tps - kv_tiles
        kv_first, kv_row0 = (first, tps), tm - rows
        out_specs.append(pl.BlockSpec((1, rows, 2, ATTN_HEADS, ATTN_HEAD_DIM),
                                      lambda i, j: (i // tps, jnp.maximum(i % tps - first, 0), 0, 0, 0)))
        out_shape.append(jax.ShapeDtypeStruct((batch, kv_keep, 2, ATTN_HEADS, ATTN_HEAD_DIM), F32))
    scratch = [pltpu.VMEM((tm, d), BF16)]
    if dilation > 1:
        scratch.append(pltpu.VMEM((tn // LANES, tm, LANES), F32))
    res = pl.pallas_call(
        functools.partial(_proj_kernel, dilation=dilation, kv_first=kv_first, kv_row0=kv_row0),
        grid=(m // tm, n // tn),
        in_specs=[pl.BlockSpec((tm, d), lambda i, j: (i, 0)),
                  pl.BlockSpec((1, d), lambda i, j: (0, 0)),
                  pl.BlockSpec((d, tn), lambda i, j: (0, j))],
        out_specs=out_specs,
        out_shape=out_shape,
        scratch_shapes=scratch,
        compiler_params=_cparams("arbitrary", "arbitrary"),
        name=f"norm_proj_d{dilation}",
    )(x2d, g_row, w_bf)
    return res if kv_keep is not None else res[0]


def _w_in_kernel(w_ref, nat_ref, d1_ref, d2_ref):
    offs, acc = [0], 0
    for s in PROJ_SPLITS[:-1]:
        acc += s
        offs.append(acc)
    q_g, k_g, v_g, r_g, gk, q_a, k_a, v_a, gg, ga = offs
    gw = ATTN_GROUP_DIM

    def put(o_ref, dst, src, width):
        o_ref[:, dst:dst + width] = jnp.transpose(w_ref[src:src + width, :]).astype(BF16)

    put(nat_ref, OFF_QG, q_g, gk - q_g)
    put(nat_ref, OFF_GATE_GLA, gg, 2 * D_MODEL)
    for i, src in enumerate((q_a, k_a, v_a)):
        put(nat_ref, OFF_QKV0 + i * gw, src, gw)
        put(d1_ref, i * gw, src + gw, gw)
        put(d2_ref, i * gw, src + 2 * gw, gw)
    tr = w_ref.shape[1]
    lane = lax.broadcasted_iota(jnp.int32, (tr, LANES), 1)
    low = jnp.where(lane < GLA_GATE_RANK, jnp.transpose(w_ref[gk:gk + LANES, :]), 0.0)
    nat_ref[:, OFF_GK:OFF_GK + LANES] = low.astype(BF16)
    nat_ref[:, OFF_GK + LANES:] = jnp.zeros((tr, NAT_W - OFF_GK - LANES), BF16)


def _prep_w_in(w_t, tr):
    n, k = w_t.shape
    return pl.pallas_call(
        _w_in_kernel,
        grid=(k // tr,),
        in_specs=[pl.BlockSpec((n, tr), lambda i: (0, i))],
        out_specs=[pl.BlockSpec((tr, NAT_W), lambda i: (i, 0)), pl.BlockSpec((tr, QKV_W), lambda i: (i, 0)),
                   pl.BlockSpec((tr, QKV_W), lambda i: (i, 0))],
        out_shape=[jax.ShapeDtypeStruct((k, NAT_W), BF16), jax.ShapeDtypeStruct((k, QKV_W), BF16),
                   jax.ShapeDtypeStruct((k, QKV_W), BF16)],
        compiler_params=_cparams("parallel"),
        name="w_in_prep",
    )(w_t)


def _proj_precise_kernel(x_ref, g_ref, w_ref, o_ref):
    h = _rmsnorm(x_ref[...], g_ref[...])
    o_ref[...] = lax.dot_general(h, w_ref[...], (((1,), (1,)), ((), ())), preferred_element_type=F32,
                                 precision=HIGHEST)


def _norm_proj_precise(x2d, g_row, w_t, tn):
    m, d = x2d.shape
    n = w_t.shape[0]
    return pl.pallas_call(
        _proj_precise_kernel,
        grid=(pl.cdiv(n, tn),),
        in_specs=[pl.BlockSpec((m, d), lambda j: (0, 0)),
                  pl.BlockSpec((1, d), lambda j: (0, 0)),
                  pl.BlockSpec((tn, d), lambda j: (j, 0))],
        out_specs=pl.BlockSpec((m, tn), lambda j: (0, j)),
        out_shape=jax.ShapeDtypeStruct((m, n), F32),
        compiler_params=_cparams("parallel"),
        name="norm_proj_sample",
    )(x2d, g_row, w_t)


def _gla_kernel(q_ref, k_ref, v_ref, r_ref, gk_ref, wup_ref, bup_ref, gn_ref,
                u_ref, st_ref, s_scr, o_scr, *, n_chunks):
    t = pl.program_id(1)
    kd, vd = GLA_HEAD_K, GLA_HEAD_V

    @pl.when(t == 0)
    def _():
        s_scr[...] = jnp.zeros_like(s_scr)

    gk = _dot(gk_ref[...], wup_ref[...]) + bup_ref[...]
    log_a = _log_sigmoid(gk) * (1.0 / GLA_GATE_NORMALIZER)
    c_sz = GLA_CHUNK
    row = lax.broadcasted_iota(jnp.int32, (c_sz, c_sz), 0)
    col = lax.broadcasted_iota(jnp.int32, (c_sz, c_sz), 1)
    tri = row >= col
    tri_bf = jnp.where(tri, 1.0, 0.0).astype(BF16)
    scale = kd ** -0.5

    for c in range(n_chunks):
        sl = slice(c * c_sz, (c + 1) * c_sz)
        g_hi, g_mid, g_lo = _split3(log_a[sl])
        b_all = _dot(tri_bf, g_hi) + _dot(tri_bf, g_mid) + _dot(tri_bf, g_lo)
        for h in range(GLA_HEADS):
            kc, vc = slice(h * kd, (h + 1) * kd), slice(h * vd, (h + 1) * vd)
            b = b_all[:, kc]
            b_last = b[c_sz - 1:c_sz, :]
            q = q_ref[sl, kc].astype(F32) * scale
            k = k_ref[sl, kc].astype(F32)
            v_bf = v_ref[sl, vc]
            q_in = (q * jnp.exp(b)).astype(BF16)
            k_in = (k * jnp.exp(-b)).astype(BF16)
            k_out = (k * jnp.exp(b_last - b)).astype(BF16)
            a = jnp.where(tri, _dot_nt(q_in, k_in), 0.0)
            st = s_scr[h]
            o_scr[sl, vc] = _dot_nt(q_in, st.astype(BF16)) + _dot(a.astype(BF16), v_bf)
            s_scr[h] = st * jnp.exp(b_last) + pl.dot(v_bf, k_out, trans_a=True)

    for h in range(GLA_HEADS):
        vc = slice(h * vd, (h + 1) * vd)
        r = r_ref[:, vc].astype(F32)
        u_ref[:, vc] = (_rmsnorm(o_scr[:, vc], gn_ref[...]) * (r * jax.nn.sigmoid(r))).astype(u_ref.dtype)

    @pl.when(t == pl.num_programs(1) - 1)
    def _():
        st_ref[0] = s_scr[...]


def _gla_prompt(proj, w_up_bf, b_up, g_norm, batch, seq, tc):
    nt = seq // tc
    kw, vw = GLA_HEADS * GLA_HEAD_K, GLA_HEADS * GLA_HEAD_V
    rows = lambda b, t: b * nt + t
    kern = functools.partial(_gla_kernel, n_chunks=tc // GLA_CHUNK)
    full = lambda a: pl.BlockSpec(a.shape, lambda b, t: (0, 0))
    return pl.pallas_call(
        kern,
        grid=(batch, nt),
        in_specs=[
            pl.BlockSpec((tc, kw), lambda b, t: (rows(b, t), OFF_QG // kw)),
            pl.BlockSpec((tc, kw), lambda b, t: (rows(b, t), OFF_KG // kw)),
            pl.BlockSpec((tc, vw), lambda b, t: (rows(b, t), OFF_VG // vw)),
            pl.BlockSpec((tc, vw), lambda b, t: (rows(b, t), OFF_RG // vw)),
            pl.BlockSpec((tc, LANES), lambda b, t: (rows(b, t), OFF_GK // LANES)),
            full(w_up_bf), full(b_up), full(g_norm),
        ],
        out_specs=[
            pl.BlockSpec((tc, vw), lambda b, t: (rows(b, t), 0)),
            pl.BlockSpec((1, GLA_HEADS, GLA_HEAD_V, GLA_HEAD_K), lambda b, t: (b, 0, 0, 0)),
        ],
        out_shape=[jax.ShapeDtypeStruct((batch * seq, vw), BF16),
                   jax.ShapeDtypeStruct((batch, GLA_HEADS, GLA_HEAD_V, GLA_HEAD_K), F32)],
        scratch_shapes=[pltpu.VMEM((GLA_HEADS, GLA_HEAD_V, GLA_HEAD_K), F32), pltpu.VMEM((tc, vw), F32)],
        compiler_params=_cparams("parallel", "arbitrary"),
        name="gla_prompt",
    )(proj, proj, proj, proj, proj, w_up_bf, b_up, g_norm)


def _attn_kernel(q_ref, kc_ref, kp_ref, vc_ref, vp_ref, o_ref, l_ref, *, group, dilation, blk, n_qb):
    n = pl.program_id(2)
    e = ATTN_HEAD_DIM
    i_idx = lax.broadcasted_iota(jnp.int32, (blk, 2 * blk), 0)
    j_idx = lax.broadcasted_iota(jnp.int32, (blk, 2 * blk), 1)
    rel = i_idx + blk - j_idx
    band = (rel >= 0) & (rel <= blk)
    dist = (dilation * rel).astype(F32)
    bias = [jnp.where(band, -_alibi_slope(group, h) * dist, NEG_BIG) for h in range(ATTN_HEADS)]
    has_prev = (j_idx >= blk) | (n > 0)
    lane = lax.broadcasted_iota(jnp.int32, (blk, LANES), 1)
    for rr, qb in [(rr, qb) for rr in range(q_ref.shape[1]) for qb in range(n_qb)]:
        rows = slice(qb * blk, (qb + 1) * blk)
        q = q_ref[0, rr, rows, :]
        if qb == 0:
            k_prev, v_prev = kp_ref[0, rr], vp_ref[0, rr]
        else:
            prev = slice((qb - 1) * blk, qb * blk)
            k_prev, v_prev = kc_ref[0, rr, prev, :], vc_ref[0, rr, prev, :]
        kk = jnp.concatenate([k_prev, kc_ref[0, rr, rows, :]], axis=0)
        vv = jnp.concatenate([v_prev, vc_ref[0, rr, rows, :]], axis=0)
        lse_tile = jnp.zeros((blk, LANES), F32)
        for h in range(ATTN_HEADS):
            cols = slice(h * e, (h + 1) * e)
            b_h = jnp.where(has_prev, bias[h], NEG_BIG) if qb == 0 else bias[h]
            s = _dot_nt(q[:, cols], kk[:, cols]) * (e ** -0.5) + b_h
            m = jnp.max(s, axis=-1, keepdims=True)
            p = jnp.exp(s - m)
            den = jnp.sum(p, axis=-1, keepdims=True)
            o = _dot(p.astype(BF16), vv[:, cols]) / den
            o_ref[0, rr, rows, cols] = o.astype(o_ref.dtype)
            lse_tile = jnp.where(lane == h, m + jnp.log(den), lse_tile)
        l_ref[0, rr, rows, :] = lse_tile


def _attn_prompt(src, col0, group, tq, rows_per_step):
    batch, dilation, n_sub, _ = src.shape
    window, dil = ATTN_GROUPS[group]
    assert dil == dilation
    blk = window // dilation
    n_qb = tq // blk
    gw = ATTN_GROUP_DIM
    qc, kc, vc = col0 // gw, col0 // gw + 1, col0 // gw + 2
    n_res = max(1, min(dilation, rows_per_step // tq))
    cur = lambda c: pl.BlockSpec((1, n_res, tq, gw), lambda b, r, n: (b, r, n, c))
    prev = lambda c: pl.BlockSpec((1, n_res, blk, gw), lambda b, r, n: (b, r, jnp.maximum(n * n_qb - 1, 0), c))
    kern = functools.partial(_attn_kernel, group=group, dilation=dilation, blk=blk, n_qb=n_qb)
    out_spec = pl.BlockSpec((1, n_res, tq, gw), lambda b, r, n: (b, r, n, 0))
    return pl.pallas_call(
        kern,
        grid=(batch, dilation // n_res, n_sub // tq),
        in_specs=[cur(qc), cur(kc), prev(kc), cur(vc), prev(vc)],
        out_specs=[out_spec, pl.BlockSpec((1, n_res, tq, LANES), lambda b, r, n: (b, r, n, 0))],
        out_shape=[jax.ShapeDtypeStruct((batch, dilation, n_sub, gw), BF16),
                   jax.ShapeDtypeStruct((batch, dilation, n_sub, LANES), F32)],
        compiler_params=_cparams("parallel", "parallel", "arbitrary"),
        name=f"attn_prompt_g{group}",
    )(src, src, src, src, src)


def _merge_kernel(x_ref, u_ref, gg_ref, ga_ref, o1_ref, o2_ref, o3_ref, l1_ref, l2_ref, l3_ref,
                  wbg_ref, wba_ref, wo_ref, gf_ref, wr_hi_ref, wr_lo_ref, br_ref,
                  x1_ref, h2_ref, route_ref, cpad_ref, *scr, n_experts, dilations, precise, ts):
    tm = x_ref.shape[0]
    scr = list(scr)

    def natural(ref, dilation):
        if dilation == 1:
            return ref[0, 0].astype(F32)
        buf = scr.pop(0)
        n = tm // dilation
        for c in range(buf.shape[0]):
            for r in range(dilation):
                buf[c, pl.ds(r, n, stride=dilation), :] = ref[0, r, :, c * LANES:(c + 1) * LANES].astype(F32)
        return jnp.concatenate([buf[c] for c in range(buf.shape[0])], axis=1)

    o1, o2, o3 = (natural(r, d) for r, d in zip((o1_ref, o2_ref, o3_ref), dilations))
    l1, l2, l3 = (natural(r, d) for r, d in zip((l1_ref, l2_ref, l3_ref), dilations))
    mm = _dot_hi if precise else (lambda a, b: _dot(a.astype(BF16), b))
    lm = jnp.maximum(jnp.maximum(l1, l2), l3)
    e1, e2, e3 = jnp.exp(l1 - lm), jnp.exp(l2 - lm), jnp.exp(l3 - lm)
    inv = 1.0 / (e1 + e2 + e3)
    head = lax.broadcasted_iota(jnp.int32, (LANES, ATTN_GROUP_DIM), 0)
    owner = lax.broadcasted_iota(jnp.int32, (LANES, ATTN_GROUP_DIM), 1) // ATTN_HEAD_DIM
    spread = jnp.where(head == owner, 1.0, 0.0)

    def per_lane(w):
        if precise:
            return _dot_hi(w, spread)
        return _dot(w.astype(BF16), spread.astype(BF16))

    o_att = per_lane(e1 * inv) * o1 + per_lane(e2 * inv) * o2 + per_lane(e3 * inv) * o3
    merged = (jax.nn.sigmoid(gg_ref[...].astype(F32)) * mm(u_ref[...], wbg_ref[...])
              + jax.nn.sigmoid(ga_ref[...].astype(F32)) * mm(o_att, wba_ref[...]))
    x1 = x_ref[...] + mm(merged, wo_ref[...])
    x1_ref[...] = x1
    h2 = _rmsnorm(x1, gf_ref[...])
    h2_ref[...] = h2.astype(BF16)

    if precise:
        logits = _dot_hi(h2, wr_hi_ref[...]) + br_ref[...]
    else:
        h_hi, h_mid, _ = _split3(h2)
        both = _dot(h_hi, wr_lo_ref[...])
        logits = both[:, :LANES] + both[:, LANES:] + _dot(h_mid, wr_hi_ref[...]) + br_ref[...]
    lane = lax.broadcasted_iota(jnp.int32, (ts, LANES), 1).astype(F32)
    ri = lax.broadcasted_iota(jnp.int32, (ts, ts), 0)
    ci = lax.broadcasted_iota(jnp.int32, (ts, ts), 1)
    earlier = jnp.where(ci < ri, 1.0, 0.0).astype(BF16)
    li = lax.broadcasted_iota(jnp.int32, (LANES, LANES), 0)
    lj = lax.broadcasted_iota(jnp.int32, (LANES, LANES), 1)
    before = jnp.where(li < lj, 1.0, 0.0).astype(BF16)
    for s in range(tm // ts):
        rows = slice(s * ts, (s + 1) * ts)
        cur = jnp.where(lane < n_experts, logits[rows, :], -jnp.inf)
        tops, sels = [], []
        for _ in range(TOP_K):
            m = jnp.max(cur, axis=-1, keepdims=True)
            idx = jnp.min(jnp.where(cur == m, lane, float(LANES)), axis=-1, keepdims=True)
            sel = lane == idx
            tops.append(m)
            sels.append(sel)
            cur = jnp.where(sel, -jnp.inf, cur)
        ex = [jnp.exp(m - tops[0]) for m in tops]
        den = ex[0] + ex[1] + ex[2] + ex[3]
        mem = jnp.zeros((ts, LANES), F32)
        for sel in sels:
            mem = jnp.where(sel, 1.0, mem)
        rank = _dot(earlier, mem.astype(BF16))
        count = jnp.sum(mem, axis=0, keepdims=True)
        cpad = jnp.ceil(count * (1.0 / ROW_ALIGN)) * ROW_ALIGN
        cpad8 = jnp.broadcast_to(cpad, (8, LANES))
        seg_start = _dot(cpad8.astype(BF16), before)[0:1]
        pos = seg_start + rank
        route = jnp.zeros((ts, LANES), F32)
        for k in range(TOP_K):
            dest = jnp.sum(jnp.where(sels[k], pos, 0.0), axis=-1, keepdims=True)
            route = jnp.where(lane == float(k), dest, route)
            route = jnp.where(lane == float(TOP_K + k), ex[k] / den, route)
        route_ref[rows, :] = route
        cpad_ref[s * 8:(s + 1) * 8, :] = cpad8


def _merge(x2d, u, gate_src, outs, lses, wbg, wba, wo, g_ffn, wr_hi, wr_lo, b_r, n_experts, seq, tm, ts, precise):
    m, d = x2d.shape
    gw = ATTN_GROUP_DIM
    tps = seq // tm
    dilations = tuple(o.shape[1] for o in outs)
    row = lambda w: pl.BlockSpec((tm, w), lambda i: (i, 0))
    full = lambda a: pl.BlockSpec(a.shape, lambda i: (0, 0))
    grp = lambda dil, w: pl.BlockSpec((1, dil, tm // dil, w), lambda i: (i // tps, 0, i % tps, 0))
    kern = functools.partial(_merge_kernel, n_experts=n_experts, dilations=dilations, precise=precise, ts=ts)
    n_dilated = sum(1 for dil in dilations if dil > 1)
    interleave = lambda w: [pltpu.VMEM((w // LANES, tm, LANES), F32)] * n_dilated
    return pl.pallas_call(
        kern,
        grid=(m // tm,),
        in_specs=[row(d), row(d),
                  pl.BlockSpec((tm, d), lambda i: (i, OFF_GATE_GLA // d)),
                  pl.BlockSpec((tm, d), lambda i: (i, OFF_GATE_ATTN // d)),
                  *[grp(dil, gw) for dil in dilations], *[grp(dil, LANES) for dil in dilations],
                  full(wbg), full(wba), full(wo), full(g_ffn), full(wr_hi), full(wr_lo), full(b_r)],
        out_specs=[row(d), row(d), row(LANES), pl.BlockSpec((tm // ts * 8, LANES), lambda i: (i, 0))],
        out_shape=[jax.ShapeDtypeStruct((m, d), F32), jax.ShapeDtypeStruct((m, d), BF16),
                   jax.ShapeDtypeStruct((m, LANES), F32), jax.ShapeDtypeStruct((m // ts * 8, LANES), F32)],
        scratch_shapes=interleave(gw) + interleave(LANES),
        compiler_params=_cparams("parallel"),
        name="merge_router_sample" if precise else "merge_router",
    )(x2d, u, gate_src, gate_src, *outs, *lses, wbg, wba, wo, g_ffn, wr_hi, wr_lo, b_r)


def _round_up(x, mult):
    return (x + mult - 1) // mult * mult


def _tile_rows(n_tokens, n_experts):
    return _round_up(n_tokens * TOP_K + n_experts * (ROW_ALIGN - 1), ROW_ALIGN)


def _moe_plan(cpad, nc, n_experts, tm_e, n_tiles_max):
    tot = jnp.sum(cpad, axis=0)
    gsize = (tot + tm_e - 1) // tm_e * tm_e
    g_end = jnp.cumsum(gsize)
    g_start = g_end - gsize
    n_used = (g_end[-1] // tm_e).astype(jnp.int32)
    o_start = jnp.cumsum(cpad, axis=0) - cpad
    seg_end = jnp.cumsum(cpad, axis=1)
    seg_start = seg_end - cpad

    row0 = jnp.arange(nc, dtype=jnp.int32) * ROW_ALIGN
    e_of = jnp.sum((seg_end[:, None, :] <= row0[None, :, None]).astype(jnp.int32), axis=-1)
    e_of = jnp.minimum(e_of, n_experts - 1)
    pick = (e_of[:, :, None] == jnp.arange(n_experts)[None, None, :]).astype(jnp.int32)
    base = g_start[None, :] + o_start - seg_start
    dst = jnp.sum(pick * base[:, None, :], axis=-1) + row0[None, :]
    nch = (seg_end[:, -1] // ROW_ALIGN).astype(jnp.int32)
    dst = jnp.where(jnp.arange(nc)[None, :] < nch[:, None], dst, 0).astype(jnp.int32)

    t0 = jnp.arange(n_tiles_max, dtype=jnp.int32) * tm_e
    tile_expert = jnp.minimum(jnp.sum((g_end[None, :] <= t0[:, None]).astype(jnp.int32), axis=-1), n_experts - 1)
    tile_id = jnp.arange(n_tiles_max)
    last = jnp.sum(jnp.where(tile_id == n_used - 1, tile_expert, 0))
    tile_expert = jnp.where(tile_id < n_used, tile_expert, last).astype(jnp.int32)
    gap_start = (g_start + tot).astype(jnp.int32)
    gap_chunks = ((gsize - tot) // ROW_ALIGN).astype(jnp.int32)

    prev = jnp.concatenate([jnp.full((1,), -1, jnp.int32), tile_expert[:-1]])
    first = (tile_expert != prev) & (tile_id < n_used)
    slot = (jnp.cumsum(first.astype(jnp.int32)) - 1) % 2
    e_id = jnp.arange(n_experts)
    later = (e_id[None, :] > e_id[:, None]) & (gsize[None, :] > 0)
    next_of = jnp.min(jnp.where(later, e_id[None, :], n_experts), axis=1)
    next_of = jnp.where(next_of == n_experts, -1, next_of)
    nxt = jnp.sum(jnp.where(tile_expert[:, None] == e_id[None, :], next_of[None, :], 0), axis=1)
    group_info = jnp.stack([first.astype(jnp.int32), slot, nxt], axis=1).reshape(-1).astype(jnp.int32)
    return dst.reshape(-1), nch, tile_expert, n_used.reshape(1), gap_start, gap_chunks, group_info


def _loop(n, fn, unroll=1):
    def body(i, carry):
        for u in range(unroll):
            fn(i * unroll + u)
        return carry

    main = n // unroll if unroll > 1 else n
    lax.fori_loop(0, main, body, 0)
    if unroll > 1:
        lax.fori_loop(main * unroll, n, lambda i, carry: (fn(i), carry)[1], 0)


def _compact_kernel(dst_ref, nch_ref, gap_start_ref, gap_chunks_ref, nu_ref, hp_ref, rp_ref, hs_ref, rs_ref,
                    xs_ref, xc_scr, zero_scr, sem, *, nc):
    s = pl.program_id(0)
    last = pl.num_programs(0) - 1
    is_sample = s == last
    ts, r_pad = hp_ref.shape[0], xc_scr.shape[1]

    def chunk(tile, c):
        slot = tile % 2
        src = xc_scr.at[slot, pl.ds(pl.multiple_of(c * ROW_ALIGN, ROW_ALIGN), ROW_ALIGN)]
        dst = xs_ref.at[pl.ds(pl.multiple_of(dst_ref[tile * nc + c], ROW_ALIGN), ROW_ALIGN)]
        return pltpu.make_async_copy(src, dst, sem.at[slot])

    def drain(tile):
        slot = tile % 2
        one = pltpu.make_async_copy(xc_scr.at[slot, pl.ds(0, ROW_ALIGN)], xs_ref.at[pl.ds(0, ROW_ALIGN)], sem.at[slot])
        _loop(nch_ref[tile], lambda c: one.wait(), unroll=8)

    @pl.when(s >= 2)
    def _():
        drain(s - 2)

    h = jnp.where(is_sample, hs_ref[...], hp_ref[...])
    route = jnp.where(is_sample, rs_ref[...], rp_ref[...])
    dest_t = jnp.transpose(route)
    row_id = lax.broadcasted_iota(jnp.int32, (r_pad, ts), 0).astype(F32)
    perm = jnp.zeros((r_pad, ts), F32)
    for k in range(TOP_K):
        perm = jnp.where(row_id == dest_t[k:k + 1, :], 1.0, perm)
    xc_scr[s % 2] = _dot(perm.astype(BF16), h)
    _loop(nch_ref[s], lambda c: chunk(s, c).start(), unroll=8)

    @pl.when(is_sample)
    def _():
        @pl.when(s >= 1)
        def _():
            drain(s - 1)
        drain(s)

        zero_scr[...] = jnp.zeros_like(zero_scr)
        tile = zero_scr.shape[0]

        def gap(e, j):
            row = pl.multiple_of(gap_start_ref[e] + j * ROW_ALIGN, ROW_ALIGN)
            return pltpu.make_async_copy(zero_scr.at[pl.ds(0, ROW_ALIGN)], xs_ref.at[pl.ds(row, ROW_ALIGN)], sem.at[0])

        def tail(t):
            row = pl.multiple_of(t * tile, tile)
            return pltpu.make_async_copy(zero_scr, xs_ref.at[pl.ds(row, tile)], sem.at[0])

        n_exp = gap_start_ref.shape[0]
        n_tail = xs_ref.shape[0] // tile - nu_ref[0]
        _loop(n_exp, lambda e: _loop(gap_chunks_ref[e], lambda j: gap(e, j).start()))
        _loop(n_tail, lambda t: tail(nu_ref[0] + t).start())
        _loop(n_exp, lambda e: _loop(gap_chunks_ref[e], lambda j: gap(e, j).wait()))
        _loop(n_tail, lambda t: tail(nu_ref[0] + t).wait())


def _compact(h2_p, route_p, h2_s, route_s, plan, ts, r_pad, nc, n_tiles_e, tm_e):
    m, d = h2_p.shape
    n_p = m // ts
    dst, nch, _, n_used, gap_start, gap_chunks, _ = plan
    prompt = lambda w: pl.BlockSpec((ts, w), lambda i, *_: (jnp.minimum(i, n_p - 1), 0))
    sample = lambda w: pl.BlockSpec((ts, w), lambda i, *_: (0, 0))
    return pl.pallas_call(
        functools.partial(_compact_kernel, nc=nc),
        grid_spec=pltpu.PrefetchScalarGridSpec(
            num_scalar_prefetch=5,
            grid=(n_p + 1,),
            in_specs=[prompt(d), prompt(LANES), sample(d), sample(LANES)],
            out_specs=pl.BlockSpec(memory_space=pl.ANY),
            scratch_shapes=[pltpu.VMEM((2, r_pad, d), F32), pltpu.VMEM((tm_e, d), F32),
                            pltpu.SemaphoreType.DMA((2,))]),
        out_shape=jax.ShapeDtypeStruct((n_tiles_e * tm_e, d), F32),
        compiler_params=_cparams("arbitrary"),
        name="moe_compact",
    )(dst, nch, gap_start, gap_chunks, n_used, h2_p, route_p, h2_s, route_s)


def _expert_kernel(te_ref, nu_ref, nx_ref, x_ref, wgu_ref, bgu_ref, wd_ref, bd_ref, o_ref,
                   wgu_f32, wd_f32, wgu_bf, wd_bf, sem, *, d_ff, sub):
    t = pl.program_id(0)
    used = t < nu_ref[0]
    nf = 3

    def fetch(expert, slot):
        return (pltpu.make_async_copy(wgu_ref.at[expert], wgu_f32.at[slot], sem.at[0, slot]),
                pltpu.make_async_copy(wd_ref.at[expert], wd_f32.at[slot], sem.at[1, slot]))

    @pl.when(used)
    def _():
        @pl.when(nx_ref[nf * t] == 1)
        def _():
            slot, nxt = nx_ref[nf * t + 1], nx_ref[nf * t + 2]

            @pl.when(t == 0)
            def _():
                for cp in fetch(te_ref[0], slot):
                    cp.start()

            for cp in fetch(te_ref[t], slot):
                cp.wait()
            step = 128
            for i in range(wgu_bf.shape[0] // step):
                rows = slice(i * step, (i + 1) * step)
                wgu_bf[rows, :] = wgu_f32[slot, rows, :].astype(BF16)
            for i in range(wd_bf.shape[0] // step):
                rows = slice(i * step, (i + 1) * step)
                wd_bf[rows, :] = wd_f32[slot, rows, :].astype(BF16)

            @pl.when(nxt >= 0)
            def _():
                for cp in fetch(nxt, 1 - slot):
                    cp.start()

        for s in range(x_ref.shape[0] // sub):
            rows = slice(s * sub, (s + 1) * sub)
            gu = _dot(x_ref[rows, :].astype(BF16), wgu_bf[...]) + bgu_ref[0]
            gate = jnp.minimum(gu[:, :d_ff], SWIGLU_LIMIT)
            up = jnp.clip(gu[:, d_ff:], -SWIGLU_LIMIT, SWIGLU_LIMIT)
            act = (up + 1.0) * gate * jax.nn.sigmoid(SWIGLU_ALPHA * gate)
            o_ref[rows, :] = _dot(act.astype(BF16), wd_bf[...]) + bd_ref[0]

    @pl.when(jnp.logical_not(used))
    def _():
        o_ref[...] = jnp.zeros_like(o_ref)


def _experts(xs, tile_expert, n_used, group_info, wgu, bgu, wd, bd, tm_e):
    n_rows, d = xs.shape
    _, _, two_ff = wgu.shape
    d_ff = two_ff // 2
    kern = functools.partial(_expert_kernel, d_ff=d_ff, sub=tm_e)
    by_expert = lambda shape: pl.BlockSpec(shape, lambda t, te, nu, nx: (te[t], 0, 0))
    hbm = pl.BlockSpec(memory_space=pl.ANY)
    return pl.pallas_call(
        kern,
        grid_spec=pltpu.PrefetchScalarGridSpec(
            num_scalar_prefetch=3,
            grid=(n_rows // tm_e,),
            in_specs=[pl.BlockSpec((tm_e, d), lambda t, te, nu, nx: (jnp.minimum(t, nu[0] - 1), 0)),
                      hbm, by_expert((1, 1, two_ff)), hbm, by_expert((1, 1, d))],
            out_specs=pl.BlockSpec((tm_e, d), lambda t, te, nu, nx: (t, 0)),
            scratch_shapes=[pltpu.VMEM((2, d, two_ff), F32), pltpu.VMEM((2, d_ff, d), F32),
                            pltpu.VMEM((d, two_ff), BF16), pltpu.VMEM((d_ff, d), BF16),
                            pltpu.SemaphoreType.DMA((2, 2))]),
        out_shape=jax.ShapeDtypeStruct((n_rows, d), F32),
        compiler_params=pltpu.CompilerParams(dimension_semantics=("arbitrary",), vmem_limit_bytes=56 * 1024 * 1024),
        name="moe_experts",
    )(tile_expert, n_used, group_info, xs, wgu, bgu, wd, bd)


def _shift_kernel(c_ref, new_ref, o_ref):
    j = pl.program_id(1)
    last = pl.num_programs(1) - 1
    bn = o_ref.shape[1]

    @pl.when(j != last)
    def _():
        o_ref[...] = c_ref[...]

    @pl.when(j == last)
    def _():
        o_ref[0, 0:bn - 1] = c_ref[0, 1:bn]
        o_ref[0, bn - 1:bn] = new_ref[0]


def _shift_cache(cache, new_row, bn):
    bd, n = cache.shape[:2]
    tail = cache.shape[2:]
    zeros = (0,) * len(tail)
    return pl.pallas_call(
        _shift_kernel,
        grid=(bd, n // bn),
        in_specs=[pl.BlockSpec(tuple(pl.Element(s) for s in (1, bn) + tail),
                               lambda b, j: (b, jnp.minimum(j * bn + 1, n - bn)) + zeros),
                  pl.BlockSpec((1, 1) + tail, lambda b, j: (b, 0) + zeros)],
        out_specs=pl.BlockSpec((1, bn) + tail, lambda b, j: (b, j) + zeros),
        out_shape=jax.ShapeDtypeStruct(cache.shape, cache.dtype),
        compiler_params=_cparams("parallel", "arbitrary"),
        name="cache_shift",
    )(cache, new_row)


def _combine_kernel(dst_ref, nch_ref, route_ref, x1_ref, gfin_ref, ys_ref, y_ref, yc_scr, sem, *, nc, tile0):
    i = pl.program_id(0)

    def chunk(step, c):
        slot = step % 2
        tile = step + tile0
        src = ys_ref.at[pl.ds(pl.multiple_of(dst_ref[tile * nc + c], ROW_ALIGN), ROW_ALIGN)]
        dst = yc_scr.at[slot, pl.ds(pl.multiple_of(c * ROW_ALIGN, ROW_ALIGN), ROW_ALIGN)]
        return pltpu.make_async_copy(src, dst, sem.at[slot])

    def fetch(step):
        _loop(nch_ref[step + tile0], lambda c: chunk(step, c).start(), unroll=8)

    @pl.when(i == 0)
    def _():
        yc_scr[...] = jnp.zeros_like(yc_scr)
        fetch(i)

    @pl.when(i + 1 < pl.num_programs(0))
    def _():
        fetch(i + 1)

    one = pltpu.make_async_copy(ys_ref.at[pl.ds(0, ROW_ALIGN)], yc_scr.at[i % 2, pl.ds(0, ROW_ALIGN)], sem.at[i % 2])
    _loop(nch_ref[i + tile0], lambda c: one.wait(), unroll=8)

    ts, r_pad = x1_ref.shape[0], yc_scr.shape[1]
    route = route_ref[...]
    col_id = lax.broadcasted_iota(jnp.int32, (ts, r_pad), 1).astype(F32)
    weights = jnp.zeros((ts, r_pad), F32)
    for k in range(TOP_K):
        weights = jnp.where(col_id == route[:, k:k + 1], route[:, TOP_K + k:TOP_K + k + 1], weights)
    moe = _dot(weights.astype(BF16), yc_scr[i % 2].astype(BF16))
    y_ref[...] = _rmsnorm(x1_ref[...] + moe, gfin_ref[...])


def _combine(route, x1, g_final, ys, plan, tile0, ts, r_pad, nc):
    m, d = x1.shape
    dst, nch = plan[0], plan[1]
    return pl.pallas_call(
        functools.partial(_combine_kernel, nc=nc, tile0=tile0),
        grid_spec=pltpu.PrefetchScalarGridSpec(
            num_scalar_prefetch=2,
            grid=(m // ts,),
            in_specs=[pl.BlockSpec((ts, LANES), lambda i, *_: (i, 0)),
                      pl.BlockSpec((ts, d), lambda i, *_: (i, 0)),
                      pl.BlockSpec((1, d), lambda i, *_: (0, 0)),
                      pl.BlockSpec(memory_space=pl.ANY)],
            out_specs=pl.BlockSpec((ts, d), lambda i, *_: (i, 0)),
            scratch_shapes=[pltpu.VMEM((2, r_pad, d), F32), pltpu.SemaphoreType.DMA((2,))]),
        out_shape=jax.ShapeDtypeStruct((m, d), F32),
        compiler_params=_cparams("arbitrary"),
        name="moe_combine",
    )(dst, nch, route, x1, g_final, ys)


def _column(x_row):
    return jnp.transpose(jnp.broadcast_to(x_row, (LANES, LANES)))


def _gla_step_kernel(p_ref, s_ref, wup_ref, bup_ref, gn_ref, u_ref, so_ref):
    row = p_ref[0]
    gk8 = jnp.broadcast_to(row[:, OFF_GK:OFF_GK + LANES], (8, LANES))
    gk = _dot_hi(gk8, wup_ref[...]) + bup_ref[...]
    decay = jnp.exp(_log_sigmoid(gk[0:1, :]) * (1.0 / GLA_GATE_NORMALIZER))
    kd, vd = GLA_HEAD_K, GLA_HEAD_V
    wide = lambda c: jnp.concatenate([c, c], axis=1)
    for h in range(GLA_HEADS):
        q = row[:, OFF_QG + h * kd:OFF_QG + (h + 1) * kd] * (kd ** -0.5)
        k = row[:, OFF_KG + h * kd:OFF_KG + (h + 1) * kd]
        v = row[:, OFF_VG + h * vd:OFF_VG + (h + 1) * vd]
        r = row[:, OFF_RG + h * vd:OFF_RG + (h + 1) * vd]
        s_new = s_ref[0, h] * wide(_column(decay[:, h * kd:(h + 1) * kd])) + wide(_column(k)) * v
        so_ref[0, h] = s_new
        o = jnp.sum(wide(_column(q)) * s_new, axis=0, keepdims=True)
        u_ref[0, :, h * vd:(h + 1) * vd] = _rmsnorm(o, gn_ref[...]) * (r * jax.nn.sigmoid(r))


def _gla_sample(proj_s3, state, w_up, b_up, g_norm):
    bd = proj_s3.shape[0]
    full = lambda a: pl.BlockSpec(a.shape, lambda b: (0,) * a.ndim)
    st_spec = pl.BlockSpec((1, GLA_HEADS, GLA_HEAD_K, GLA_HEAD_V), lambda b: (b, 0, 0, 0))
    return pl.pallas_call(
        _gla_step_kernel,
        grid=(bd,),
        in_specs=[pl.BlockSpec((1, 1, PROJ_W), lambda b: (b, 0, 0)), st_spec,
                  full(w_up), full(b_up), full(g_norm)],
        out_specs=[pl.BlockSpec((1, 1, GLA_HEADS * GLA_HEAD_V), lambda b: (b, 0, 0)), st_spec],
        out_shape=[jax.ShapeDtypeStruct((bd, 1, GLA_HEADS * GLA_HEAD_V), F32),
                   jax.ShapeDtypeStruct(state.shape, state.dtype)],
        compiler_params=_cparams("parallel"),
        name="gla_sample",
    )(proj_s3, state, w_up, b_up, g_norm)


def _attn_step_kernel(p_ref, c1_ref, c2_ref, c3_ref, o_ref, l_ref):
    row = p_ref[0]
    gw, e = ATTN_GROUP_DIM, ATTN_HEAD_DIM
    for g, c_ref in enumerate((c1_ref, c2_ref, c3_ref)):
        window, dilation = ATTN_GROUPS[g]
        blk = window // dilation
        back = (dilation * (blk - lax.broadcasted_iota(jnp.int32, (blk, 1), 0))).astype(F32)
        lane = lax.broadcasted_iota(jnp.int32, (1, LANES), 1)
        lse_row = jnp.zeros((1, LANES), F32)
        for h in range(ATTN_HEADS):
            lo = OFF_QKV[g] + h * e
            q = row[:, lo:lo + e]
            k_new = row[:, lo + gw:lo + gw + e]
            v_new = row[:, lo + 2 * gw:lo + 2 * gw + e]
            k_c = c_ref[0, :, 0, h, :]
            v_c = c_ref[0, :, 1, h, :]
            s = jnp.sum(k_c * q, axis=-1, keepdims=True) * (e ** -0.5) - _alibi_slope(g, h) * back
            s0 = jnp.sum(k_new * q, axis=-1, keepdims=True) * (e ** -0.5)
            m = jnp.maximum(jnp.max(s, axis=0, keepdims=True), s0)
            p = jnp.exp(s - m)
            p0 = jnp.exp(s0 - m)
            den = jnp.sum(p, axis=0, keepdims=True) + p0
            o = (jnp.sum(p * v_c, axis=0, keepdims=True) + p0 * v_new) / den
            cols = slice(g * gw + h * e, g * gw + (h + 1) * e)
            o_ref[0, :, cols] = o
            lse_row = jnp.where(lane == h, m + jnp.log(den), lse_row)
        l_ref[0, :, g * LANES:(g + 1) * LANES] = lse_row


def _attn_sample(proj_s3, caches):
    bd = proj_s3.shape[0]
    gw = ATTN_GROUP_DIM
    views, specs = [], []
    for g, cache in enumerate(caches):
        window, dilation = ATTN_GROUPS[g]
        n_buf = cache.shape[1]
        assert n_buf == window, "cache must hold exactly one window of rows"
        blk = n_buf // dilation
        views.append(cache.reshape(bd, blk, dilation, 2, ATTN_HEADS, ATTN_HEAD_DIM))
        specs.append(pl.BlockSpec((1, blk, None, 2, ATTN_HEADS, ATTN_HEAD_DIM), lambda b: (b, 0, 0, 0, 0, 0)))
    out_spec = pl.BlockSpec((1, 1, N_GROUPS * gw), lambda b: (b, 0, 0))
    return pl.pallas_call(
        _attn_step_kernel,
        grid=(bd,),
        in_specs=[pl.BlockSpec((1, 1, PROJ_W), lambda b: (b, 0, 0))] + specs,
        out_specs=[out_spec, pl.BlockSpec((1, 1, N_GROUPS * LANES), lambda b: (b, 0, 0))],
        out_shape=[jax.ShapeDtypeStruct((bd, 1, N_GROUPS * gw), F32),
                   jax.ShapeDtypeStruct((bd, 1, N_GROUPS * LANES), F32)],
        compiler_params=_cparams("parallel"),
        name="attn_sample",
    )(proj_s3, *views)


def _reorder_cols(a, dtype):
    offs, acc = [], 0
    for s in PROJ_SPLITS[:-1]:
        acc += s
        offs.append(acc)
    q_g, k_g, v_g, r_g, gk, q_a, k_a, v_a, gg, ga = [p.astype(dtype) for p in jnp.split(a, offs, axis=1)]
    gw = ATTN_GROUP_DIM
    qkv = lambda g: [p[:, g * gw:(g + 1) * gw] for p in (q_a, k_a, v_a)]
    pad = jnp.zeros((a.shape[0], NAT_W - OFF_GK - GLA_GATE_RANK), dtype)
    nat = jnp.concatenate([q_g, k_g, v_g, r_g, gg, ga, *qkv(0), gk, pad], axis=1)
    return nat, jnp.concatenate(qkv(1), axis=1), jnp.concatenate(qkv(2), axis=1)


def _kv_rows(src, col_k, n_keep):
    b, dil, n_sub, _ = src.shape
    kv = src[:, :, n_sub - n_keep // dil:, col_k:col_k + 2 * ATTN_GROUP_DIM]
    kv = jnp.swapaxes(kv, 1, 2).astype(F32)
    return kv.reshape(b, n_keep, 2, ATTN_HEADS, ATTN_HEAD_DIM)


def kernel(x_prompt, x_sample, state_gla, cache_kv_w128, cache_kv_w512, cache_kv_w2048, g_norm_mix, w_in, w_gk_up,
           b_gk_up, g_gla_norm, w_branch_gla, w_branch_attn, w_out, g_norm_ffn, w_router, b_router, w_gate_up,
           b_gate_up, w_down, b_down, g_final):
    depth = g_norm_mix.shape[0]
    assert depth == 1, "single-layer trunk"
    batch, seq, d = x_prompt.shape
    bd, dec_seq, _ = x_sample.shape
    assert d == D_MODEL and dec_seq == 1 and seq % ATTN_GROUPS[-1][0] == 0
    n_experts = w_router.shape[-1]
    caches = (cache_kv_w128[0], cache_kv_w512[0], cache_kv_w2048[0])
    gw = ATTN_GROUP_DIM

    w_in_t = jnp.swapaxes(w_in[0], 0, 1)
    w_blocks = _prep_w_in(w_in_t, tr=128)
    w_up =jnp.zeros((LANES, GLA_HEADS * GLA_HEAD_K), F32).at[:GLA_GATE_RANK].set(w_gk_up[0])
    w_up_bf = w_up.astype(BF16)
    b_up = b_gk_up[0][None, :]
    g_mix = g_norm_mix[0][None, :]
    g_gla = g_gla_norm[0][None, :]
    wbg, wba, wo = w_branch_gla[0], w_branch_attn[0], w_out[0]
    g_ffn = g_norm_ffn[0][None, :]
    w_r = jnp.zeros((d, LANES), F32).at[:, :n_experts].set(w_router[0])
    wr_hi = w_r.astype(BF16)
    wr_lo = jnp.concatenate([wr_hi, (w_r - wr_hi.astype(F32)).astype(BF16)], axis=1)
    b_r = jnp.zeros((1, LANES), F32).at[0, :n_experts].set(b_router[0])
    bgu, bdn = b_gate_up[0][:, None, :], b_down[0][:, None, :]
    g_fin = g_final[None, :]

    xp = x_prompt.reshape(batch * seq, d)
    nat = _norm_proj(xp, g_mix, w_blocks[0], batch, seq, 1, tm=1024, tn=NAT_W // 2)
    srcs, kv_p = [nat], [_kv_rows(nat, OFF_QKV0 + gw, min(ATTN_GROUPS[0][0], seq))]
    for g in (1, 2):
        src, kv = _norm_proj(xp, g_mix, w_blocks[g], batch, seq, ATTN_GROUPS[g][1], tm=1024, tn=QKV_W,
                             kv_keep=min(ATTN_GROUPS[g][0], seq))
        srcs.append(src)
        kv_p.append(kv)
    nat2d = nat.reshape(batch * seq, NAT_W)
    u_p, st_p = _gla_prompt(nat2d, w_up_bf, b_up, g_gla, batch, seq, tc=1024)
    outs, lses = [], []
    for g in range(N_GROUPS):
        col0 = OFF_QKV0 if g == 0 else 0
        o, lse = _attn_prompt(srcs[g], col0, g, tq=min(1024, seq // ATTN_GROUPS[g][1]), rows_per_step=1024)
        outs.append(o)
        lses.append(lse)
    x1_p, h2_p, route_p, cpad_p = _merge(xp, u_p, nat2d, outs, lses, wbg.astype(BF16), wba.astype(BF16), wo.astype(BF16),
                                 g_ffn, wr_hi, wr_lo, b_r, n_experts, seq, tm=512, ts=MOE_TOKEN_TILE, precise=False)

    xs = x_sample.reshape(bd, d)
    proj_s = jnp.concatenate(_reorder_cols(_norm_proj_precise(xs, g_mix, w_in_t, tn=2048), F32), axis=1)
    proj_s3 = proj_s.reshape(bd, 1, PROJ_W)
    u_s, st_s = _gla_sample(proj_s3, state_gla[0], w_up, b_up, g_gla)
    o_s, l_s = _attn_sample(proj_s3, caches)
    outs_s = [o_s[:, 0, g * gw:(g + 1) * gw].reshape(1, 1, bd, gw) for g in range(N_GROUPS)]
    lses_s = [l_s[:, 0, g * LANES:(g + 1) * LANES].reshape(1, 1, bd, LANES) for g in range(N_GROUPS)]
    x1_s, h2_s, route_s, cpad_s = _merge(xs, u_s.reshape(bd, d), proj_s, outs_s, lses_s, wbg, wba, wo, g_ffn,
                                 w_r, w_r, b_r, n_experts, bd, tm=bd, ts=bd, precise=True)

    ts, tm_e = MOE_TOKEN_TILE, MOE_EXPERT_TILE
    assert bd <= ts
    n_tiles_p = batch * seq // ts
    rows = _tile_rows(ts, n_experts)
    r_pad, nc = _round_up(rows, LANES), rows // ROW_ALIGN
    n_tiles_e = -(-(n_tiles_p * rows + _tile_rows(bd, n_experts) + n_experts * (tm_e - 1)) // tm_e)
    cpad = jnp.concatenate([cpad_p[::8, :n_experts], cpad_s[::8, :n_experts]], axis=0).astype(jnp.int32)
    plan = _moe_plan(cpad, nc, n_experts, tm_e, n_tiles_e)
    pad_rows = lambda a, fill: jnp.concatenate([a, jnp.full((ts - bd, a.shape[1]), fill, a.dtype)], axis=0)
    h2_s, x1_s = pad_rows(h2_s, 0), pad_rows(x1_s, 0)
    route_s = pad_rows(route_s, -1.0)
    xs = _compact(h2_p, route_p, h2_s, route_s, plan, ts, r_pad, nc, n_tiles_e, tm_e)
    new_rows = [proj_s[:, OFF_QKV[g] + gw:OFF_QKV[g] + 3 * gw].reshape(bd, 1, 2, ATTN_HEADS, ATTN_HEAD_DIM)
                .astype(caches[g].dtype) for g in range(N_GROUPS)]
    ys = _experts(xs, plan[2], plan[3], plan[6], w_gate_up[0], bgu, w_down[0], bdn, tm_e)
    kv_s = [_shift_cache(caches[g], new_rows[g], min(1024, caches[g].shape[1]))[None] for g in range(N_GROUPS)]
    y_p = _combine(route_p, x1_p, g_fin, ys, plan, 0, ts, r_pad, nc)
    y_s = _combine(route_s, x1_s, g_fin, ys, plan, n_tiles_p, ts, r_pad, nc)[:bd]

    return (y_p.reshape(batch, seq, d), y_s.reshape(bd, 1, d),
            jnp.swapaxes(st_p, -1, -2)[None], kv_p[0][None], kv_p[1][None], kv_p[2][None],
            st_s[None], kv_s[0], kv_s[1], kv_s[2])
```
